```python
import math
import jax, jax.numpy as jnp
from jax import lax
import numpy as np

D_MODEL = 1024
BATCH = 4
SEQ = 8192
DEPTH = 1

D_MIX = D_MODEL
HG_HEADS = 4
HG_DK = 128
HG_DV = 128
HG_CHUNK = 64
HG_WIDTH = HG_HEADS * HG_DV
NSA_HEADS = 8
NSA_KV_HEADS = 2
NSA_HD = 64
NSA_WIDTH = NSA_HEADS * NSA_HD
NSA_KVW = NSA_KV_HEADS * NSA_HD
CMP_LEN = 32
CMP_STRIDE = 16
CMP_HIDDEN = 256
SLC_LEN = 64
SLC_TOPK = 16
WIN = 512
Q_BLOCK = 128
FORCE_SCORE = 1.0e4
PEER_HEADS = 8
PEER_DQ = 256
PEER_NKEYS = 128
PEER_TOPK = 16
PEER_N_EXPERTS = PEER_NKEYS * PEER_NKEYS
PEER_TOK_BLOCK = 128
DN_ALPHA = (2.0 * DEPTH) ** 0.25
DN_BETA = (8.0 * DEPTH) ** -0.25
LN_EPS = 1e-5
NEG = -1e30
IN_SPLITS = (HG_HEADS * HG_DK, HG_HEADS * HG_DK, HG_HEADS * HG_DV, HG_WIDTH,
             NSA_WIDTH, NSA_KVW, NSA_KVW, NSA_KVW, NSA_KVW, NSA_KVW, NSA_KVW, NSA_HEADS * 3)
D_IN = sum(IN_SPLITS)

kernel_name = "hybrid_hgrn2_nsa_peer_deepnorm"


def _layernorm(t, g, b):
    t32 = t.astype(jnp.float32)
    mu = jnp.mean(t32, -1, keepdims=True)
    var = jnp.mean(jnp.square(t32 - mu), -1, keepdims=True)
    return ((t32 - mu) * lax.rsqrt(var + LN_EPS) * g + b).astype(t.dtype)


def _masked_softmax(s, mask):
    s = jnp.where(mask, s.astype(jnp.float32), NEG)
    m = jnp.max(s, axis=-1, keepdims=True)
    p = jnp.where(mask, jnp.exp(s - m), 0.0)
    return p / jnp.maximum(jnp.sum(p, -1, keepdims=True), 1e-30)


def _hgrn2(q, f_logit, i_val, g, lb, norm_g):
    B, S, _ = q.shape
    f32 = jnp.float32
    f = lb + (1.0 - lb) * jax.nn.sigmoid(f_logit.astype(f32))
    logf = jnp.log(f)
    k = 1.0 - f
    n_chunks = S // HG_CHUNK

    def to_chunks(t, d):
        return t.astype(f32).reshape(B, n_chunks, HG_CHUNK, HG_HEADS, d).transpose(1, 0, 3, 2, 4)

    qc, kc, lfc = to_chunks(q, HG_DK), to_chunks(k, HG_DK), to_chunks(logf, HG_DK)
    vc = to_chunks(i_val, HG_DV)
    causal = jnp.tril(jnp.ones((HG_CHUNK, HG_CHUNK), bool))[None, None, :, :, None]

    def chunk_step(state, inp):
        q_, k_, v_, lf = inp
        b = jnp.cumsum(lf, axis=2)
        o_inter = jnp.einsum('bhtd,bhde->bhte', q_ * jnp.exp(b), state)
        diff = b[:, :, :, None, :] - b[:, :, None, :, :]
        decay = jnp.exp(jnp.where(causal, diff, -jnp.inf))
        a = jnp.einsum('bhtd,bhsd,bhtsd->bhts', q_, k_, decay)
        o_intra = jnp.einsum('bhts,bhse->bhte', a, v_)
        b_last = b[:, :, -1:, :]
        new_state = (jnp.exp(b_last[:, :, 0, :])[..., None] * state
                     + jnp.einsum('bhsd,bhse->bhde', k_ * jnp.exp(b_last - b), v_))
        return new_state, o_inter + o_intra

    s0 = jnp.zeros((B, HG_HEADS, HG_DK, HG_DV), f32)
    _, o = lax.scan(chunk_step, s0, (qc, kc, vc, lfc))
    o = o.transpose(1, 0, 3, 2, 4).reshape(B, S, HG_HEADS, HG_DV)
    o = o * lax.rsqrt(jnp.mean(o * o, -1, keepdims=True) + LN_EPS) * norm_g
    return o.reshape(B, S, HG_WIDTH) * jax.nn.silu(g.astype(f32))


def _compress(kv, pe, w1, b1, w2, b2):
    B, S = kv.shape[0], kv.shape[1]
    nb = (S - CMP_LEN) // CMP_STRIDE + 1
    idx = np.arange(nb)[:, None] * CMP_STRIDE + np.arange(CMP_LEN)[None, :]
    blocks = kv[:, idx] + pe[:, None, :]
    blocks = blocks.transpose(0, 1, 3, 2, 4).reshape(B, nb, NSA_KV_HEADS, CMP_LEN * NSA_HD)
    h = jax.nn.gelu(blocks @ w1 + b1)
    return h @ w2 + b2


def _nsa(q, k_c, v_c, k_s, v_s, k_w, v_w, gate_logit, cmpk, cmpv):
    B, S, _ = q.shape
    G = NSA_HEADS // NSA_KV_HEADS
    scale = NSA_HD ** -0.5
    q = q.reshape(B, S, NSA_KV_HEADS, G, NSA_HD)
    kvr = lambda t: t.reshape(B, S, NSA_KV_HEADS, NSA_HD)
    kc = _compress(kvr(k_c), *cmpk)
    vc = _compress(kvr(v_c), *cmpv)
    nb = kc.shape[1]
    cmp_start = np.arange(nb) * CMP_STRIDE
    cmp_end = jnp.asarray(cmp_start + CMP_LEN - 1, jnp.int32)
    n_slc = S // SLC_LEN
    topk = min(SLC_TOPK, n_slc)
    sel_start = np.arange(n_slc) * SLC_LEN
    ov = np.clip(np.minimum(cmp_start[:, None] + CMP_LEN, sel_start[None, :] + SLC_LEN)
                 - np.maximum(cmp_start[:, None], sel_start[None, :]), 0, None) / CMP_LEN
    overlap = jnp.asarray(ov, jnp.float32)
    ks_blocks = kvr(k_s).reshape(B, n_slc, SLC_LEN, NSA_KV_HEADS, NSA_HD).transpose(0, 3, 1, 2, 4)
    vs_blocks = kvr(v_s).reshape(B, n_slc, SLC_LEN, NSA_KV_HEADS, NSA_HD).transpose(0, 3, 1, 2, 4)
    pad = ((0, 0), (WIN, 0), (0, 0), (0, 0))
    kw_pad = jnp.pad(kvr(k_w), pad)
    vw_pad = jnp.pad(kvr(v_w), pad)
    gates = jax.nn.sigmoid(gate_logit.astype(jnp.float32)).reshape(B, S, NSA_KV_HEADS, G, 3)
    gather = jax.vmap(jax.vmap(lambda blk, ix: blk[ix]))
    blk_ids = jnp.arange(n_slc)

    def block_step(qb):
        t0 = qb * Q_BLOCK
        pos = t0 + jnp.arange(Q_BLOCK)
        qq = lax.dynamic_slice_in_dim(q, t0, Q_BLOCK, axis=1) * scale
        s_c = jnp.einsum('btgrd,bngd->bgrtn', qq, kc)
        p_c = _masked_softmax(s_c, cmp_end[None, :] <= pos[:, None])
        o_c = jnp.einsum('bgrtn,bngd->btgrd', p_c, vc)
        imp = jnp.einsum('bgrtn,nj->bgtj', p_c, overlap)
        cur = pos // SLC_LEN
        forced = (blk_ids[None, :] == 0) | (blk_ids[None, :] == cur[:, None]) | (blk_ids[None, :] == cur[:, None] - 1)
        causal_blk = blk_ids[None, :] * SLC_LEN <= pos[:, None]
        score = jnp.where(forced, FORCE_SCORE, jnp.where(causal_blk, imp, -1.0))
        _, idx = lax.top_k(score, topk)
        kg = gather(ks_blocks, idx).reshape(B, NSA_KV_HEADS, Q_BLOCK, topk * SLC_LEN, NSA_HD)
        vg = gather(vs_blocks, idx).reshape(B, NSA_KV_HEADS, Q_BLOCK, topk * SLC_LEN, NSA_HD)
        key_pos = (idx[..., None] * SLC_LEN + jnp.arange(SLC_LEN)).reshape(B, NSA_KV_HEADS, Q_BLOCK, topk * SLC_LEN)
        s_s = jnp.einsum('btgrd,bgtkd->bgrtk', qq, kg)
        p_s = _masked_softmax(s_s, (key_pos <= pos[None, None, :, None])[:, :, None])
        o_s = jnp.einsum('bgrtk,bgtkd->btgrd', p_s, vg)
        kw = lax.dynamic_slice_in_dim(kw_pad, t0, WIN + Q_BLOCK, axis=1)
        vw = lax.dynamic_slice_in_dim(vw_pad, t0, WIN + Q_BLOCK, axis=1)
        key_w = t0 - WIN + jnp.arange(WIN + Q_BLOCK)
        delta = pos[:, None] - key_w[None, :]
        m_w = (delta >= 0) & (delta < WIN) & (key_w[None, :] >= 0)
        s_w = jnp.einsum('btgrd,bsgd->bgrts', qq, kw)
        p_w = _masked_softmax(s_w, m_w)
        o_w = jnp.einsum('bgrts,bsgd->btgrd', p_w, vw)
        gt = lax.dynamic_slice_in_dim(gates, t0, Q_BLOCK, axis=1)
        o = gt[..., 0:1] * o_c + gt[..., 1:2] * o_s + gt[..., 2:3] * o_w
        return o.reshape(B, Q_BLOCK, NSA_WIDTH)

    out = lax.map(block_step, jnp.arange(S // Q_BLOCK))
    return out.transpose(1, 0, 2, 3).reshape(B, S, NSA_WIDTH)


def _peer(x, w_q, sub_k1, sub_k2, u, v):
    B, S, D = x.shape
    n = B * S
    xf = x.reshape(n, D)
    half = PEER_DQ // 2
    q = (xf @ w_q).reshape(n, PEER_HEADS, PEER_DQ)
    s1 = jnp.einsum('nhd,kd->nhk', q[..., :half], sub_k1).astype(jnp.float32)
    s2 = jnp.einsum('nhd,kd->nhk', q[..., half:], sub_k2).astype(jnp.float32)
    v1, i1 = lax.top_k(s1, PEER_TOPK)
    v2, i2 = lax.top_k(s2, PEER_TOPK)
    cand = (v1[..., :, None] + v2[..., None, :]).reshape(n, PEER_HEADS, PEER_TOPK * PEER_TOPK)
    top_s, top_c = lax.top_k(cand, PEER_TOPK)
    e1 = jnp.take_along_axis(i1, top_c // PEER_TOPK, axis=-1)
    e2 = jnp.take_along_axis(i2, top_c % PEER_TOPK, axis=-1)
    hk = PEER_HEADS * PEER_TOPK
    experts = (e1 * PEER_NKEYS + e2).reshape(n, hk)
    gate = jax.nn.softmax(top_s, axis=-1).reshape(n, hk)
    nblk = n // PEER_TOK_BLOCK

    def block_step(args):
        xb, eb, gb = args
        h = jnp.einsum('tkd,td->tk', u[eb], xb)
        w = gb * jax.nn.gelu(h.astype(jnp.float32))
        return jnp.einsum('tk,tkd->td', w, v[eb])

    out = lax.map(block_step, (xf.reshape(nblk, PEER_TOK_BLOCK, D),
                               experts.reshape(nblk, PEER_TOK_BLOCK, hk),
                               gate.reshape(nblk, PEER_TOK_BLOCK, hk)))
    return out.reshape(B, S, D).astype(x.dtype)


def setup_inputs(seed: int = 0) -> dict:
    key = jax.random.key(seed)
    ks = jax.random.split(key, 24)
    nrm = lambda k, shape, s: jax.random.normal(k, shape, jnp.float32) * s
    L = DEPTH
    return {
        'x': nrm(ks[0], (BATCH, SEQ, D_MODEL), 1.0),
        'w_in': nrm(ks[1], (L, D_MODEL, D_IN), D_MODEL ** -0.5),
        'w_out': nrm(ks[2], (L, D_MIX, D_MODEL), D_MIX ** -0.5 * DN_BETA),
        'hg_lb': nrm(ks[3], (L + 1, HG_HEADS * HG_DK), 0.3),
        'hg_norm_g': 1.0 + nrm(ks[4], (L, HG_DV), 0.02),
        'cmpk_pe': nrm(ks[5], (L, CMP_LEN, NSA_HD), 0.1),
        'cmpk_w1': nrm(ks[6], (L, CMP_LEN * NSA_HD, CMP_HIDDEN), (CMP_LEN * NSA_HD) ** -0.5),
        'cmpk_b1': nrm(ks[7], (L, CMP_HIDDEN), 0.01),
        'cmpk_w2': nrm(ks[8], (L, CMP_HIDDEN, NSA_HD), CMP_HIDDEN ** -0.5),
        'cmpk_b2': nrm(ks[9], (L, NSA_HD), 0.01),
        'cmpv_pe': nrm(ks[10], (L, CMP_LEN, NSA_HD), 0.1),
        'cmpv_w1': nrm(ks[11], (L, CMP_LEN * NSA_HD, CMP_HIDDEN), (CMP_LEN * NSA_HD) ** -0.5),
        'cmpv_b1': nrm(ks[12], (L, CMP_HIDDEN), 0.01),
        'cmpv_w2': nrm(ks[13], (L, CMP_HIDDEN, NSA_HD), CMP_HIDDEN ** -0.5),
        'cmpv_b2': nrm(ks[14], (L, NSA_HD), 0.01),
        'ln1_g': 1.0 + nrm(ks[15], (L, D_MODEL), 0.02),
        'ln1_b': nrm(ks[16], (L, D_MODEL), 0.01),
        'peer_wq': nrm(ks[17], (L, D_MODEL, PEER_HEADS * PEER_DQ), D_MODEL ** -0.5),
        'peer_k1': nrm(ks[18], (L, PEER_NKEYS, PEER_DQ // 2), (PEER_DQ // 2) ** -0.5),
        'peer_k2': nrm(ks[19], (L, PEER_NKEYS, PEER_DQ // 2), (PEER_DQ // 2) ** -0.5),
        'peer_u': nrm(ks[20], (L, PEER_N_EXPERTS, D_MODEL), D_MODEL ** -0.5),
        'peer_v': nrm(ks[21], (L, PEER_N_EXPERTS, D_MODEL), DN_BETA),
        'ln2_g': 1.0 + nrm(ks[22], (L, D_MODEL), 0.02),
        'ln2_b': nrm(ks[23], (L, D_MODEL), 0.01),
    }


def reference(x, w_in, w_out, hg_lb, hg_norm_g, cmpk_pe, cmpk_w1, cmpk_b1, cmpk_w2, cmpk_b2,
              cmpv_pe, cmpv_w1, cmpv_b1, cmpv_w2, cmpv_b2, ln1_g, ln1_b,
              peer_wq, peer_k1, peer_k2, peer_u, peer_v, ln2_g, ln2_b):
    lb_all = jnp.cumsum(jax.nn.softmax(hg_lb.astype(jnp.float32), axis=0), axis=0)
    offsets = [int(o) for o in np.cumsum(IN_SPLITS)[:-1]]
    h = x
    for l in range(DEPTH):
        proj = h @ w_in[l]
        (hq, hf, hi, hgate, nq, kc, vc, ks_, vs_, kw, vw, ngate) = jnp.split(proj, offsets, axis=-1)
        o_hg = _hgrn2(hq, hf, hi, hgate, lb_all[l], hg_norm_g[l])
        o_nsa = _nsa(nq, kc, vc, ks_, vs_, kw, vw, ngate,
                     (cmpk_pe[l], cmpk_w1[l], cmpk_b1[l], cmpk_w2[l], cmpk_b2[l]),
                     (cmpv_pe[l], cmpv_w1[l], cmpv_b1[l], cmpv_w2[l], cmpv_b2[l]))
        mix = jnp.concatenate([o_hg, o_nsa], axis=-1).astype(h.dtype) @ w_out[l]
        h = _layernorm(DN_ALPHA * h + mix, ln1_g[l], ln1_b[l])
        ffn = _peer(h, peer_wq[l], peer_k1[l], peer_k2[l], peer_u[l], peer_v[l])
        h = _layernorm(DN_ALPHA * h + ffn, ln2_g[l], ln2_b[l])
    return h
```

```python
import functools
import math

import jax
import jax.numpy as jnp
import numpy as np
from jax import lax
from jax.experimental import pallas as pl
from jax.experimental.pallas import tpu as pltpu
from jax.experimental.pallas import tpu_sc as plsc

F32 = jnp.float32
BF16 = jnp.bfloat16

HG_HEADS = 4
HG_DK = 128
HG_DV = 128
HG_CHUNK = 64
HG_SUB = 16
HG_WIDTH = HG_HEADS * HG_DV
NSA_HEADS = 8
NSA_KV_HEADS = 2
NSA_GROUP = NSA_HEADS // NSA_KV_HEADS
NSA_HD = 64
NSA_WIDTH = NSA_HEADS * NSA_HD
CMP_LEN = 32
CMP_STRIDE = 16
CMP_HIDDEN = 256
SLC_LEN = 64
SLC_TOPK = 16
WIN = 512
FORCE_SCORE = 1.0e4
PEER_HEADS = 8
PEER_DQ = 256
PEER_NKEYS = 128
PEER_TOPK = 16
LN_EPS = 1e-5
NEG = -1e30

LANES = 128
PROJ_TM = 512
HG_TC = 256
ATT_TQ = 128
ATT_KT = 512
OUT_TM = 512
ROUTE_TR = 256
MIX_TT = 16
SC_WIN = 64
SC_WORKERS = 32
VMEM_LIMIT = 56 * 1024 * 1024


def _cparams(sem):
    return pltpu.CompilerParams(dimension_semantics=sem, vmem_limit_bytes=VMEM_LIMIT)


def _nt(a, b):
    return lax.dot_general(a, b, (((1,), (1,)), ((), ())), preferred_element_type=F32)


def _tn(a, b):
    return lax.dot_general(a, b, (((0,), (0,)), ((), ())), preferred_element_type=F32)


def _nn(a, b):
    return jnp.dot(a, b, preferred_element_type=F32)


def _proj_kernel(x_ref, whg_ref, wqT_ref, wkn_ref, wvT_ref, wgT_ref,
                 hg_ref, qT_ref, kn_ref, vsT_ref, vwT_ref, gT_ref):
    xb = x_ref[0].astype(BF16)
    hg_ref[0] = _nn(xb, whg_ref[...])
    qT_ref[0] = (_nt(wqT_ref[...], xb) * (NSA_HD ** -0.5)).astype(BF16)
    kn = _nn(xb, wkn_ref[...])
    for j in range(4):
        for g in range(NSA_KV_HEADS):
            lo = j * 2 * NSA_HD + g * NSA_HD
            kn_ref[0, j, g] = kn[:, lo:lo + NSA_HD].astype(BF16)
    vT = _nt(wvT_ref[...], xb).astype(BF16)
    tm = vT.shape[1]
    for g in range(NSA_KV_HEADS):
        for c in range(tm // ATT_KT):
            vsT_ref[0, g, c] = vT[g * NSA_HD:(g + 1) * NSA_HD, c * ATT_KT:(c + 1) * ATT_KT]
        for c in range(tm // LANES):
            vwT_ref[0, g, c] = vT[(2 + g) * NSA_HD:(3 + g) * NSA_HD, c * LANES:(c + 1) * LANES]
    gT = _nt(wgT_ref[...], xb)
    gT_ref[0, 0] = gT[0:16]
    gT_ref[0, 1] = gT[16:32]


def _project(x, w_in_l):
    B, S, D = x.shape
    tm = PROJ_TM
    o = np.cumsum((0, 512, 512, 512, 512, 512, 128, 128, 128, 128, 128, 128, 24))
    wb = w_in_l.astype(BF16)
    whg = wb[:, o[0]:o[4]]
    wqT = wb[:, o[4]:o[5]].T
    wkn = jnp.concatenate([wb[:, o[5]:o[6]], wb[:, o[6]:o[7]], wb[:, o[7]:o[8]], wb[:, o[9]:o[10]]], axis=1)
    wvT = jnp.concatenate([wb[:, o[8]:o[9]], wb[:, o[10]:o[11]]], axis=1).T
    wg = wb[:, o[11]:o[12]].reshape(D, NSA_KV_HEADS, NSA_GROUP * 3)
    wgT = jnp.pad(wg, ((0, 0), (0, 0), (0, 16 - NSA_GROUP * 3))).reshape(D, 32).T
    const = lambda shp: pl.BlockSpec(shp, lambda b, i: (0,) * len(shp))
    out_shape = (
        jax.ShapeDtypeStruct((B, S, 4 * 512), F32),
        jax.ShapeDtypeStruct((B, NSA_WIDTH, S), BF16),
        jax.ShapeDtypeStruct((B, 4, NSA_KV_HEADS, S, NSA_HD), BF16),
        jax.ShapeDtypeStruct((B, NSA_KV_HEADS, S // ATT_KT, NSA_HD, ATT_KT), BF16),
        jax.ShapeDtypeStruct((B, NSA_KV_HEADS, S // LANES, NSA_HD, LANES), BF16),
        jax.ShapeDtypeStruct((B, NSA_KV_HEADS, 16, S), F32),
    )
    out_specs = (
        pl.BlockSpec((1, tm, 2048), lambda b, i: (b, i, 0)),
        pl.BlockSpec((1, NSA_WIDTH, tm), lambda b, i: (b, 0, i)),
        pl.BlockSpec((1, 4, NSA_KV_HEADS, tm, NSA_HD), lambda b, i: (b, 0, 0, i, 0)),
        pl.BlockSpec((1, NSA_KV_HEADS, tm // ATT_KT, NSA_HD, ATT_KT), lambda b, i: (b, 0, i, 0, 0)),
        pl.BlockSpec((1, NSA_KV_HEADS, tm // LANES, NSA_HD, LANES), lambda b, i: (b, 0, i, 0, 0)),
        pl.BlockSpec((1, NSA_KV_HEADS, 16, tm), lambda b, i: (b, 0, 0, i)),
    )
    return pl.pallas_call(
        _proj_kernel,
        grid=(B, S // tm),
        in_specs=[pl.BlockSpec((1, tm, D), lambda b, i: (b, i, 0)),
                  const(whg.shape), const(wqT.shape), const(wkn.shape), const(wvT.shape), const(wgT.shape)],
        out_specs=out_specs,
        out_shape=out_shape,
        compiler_params=_cparams(("parallel", "arbitrary")),
        name="in_proj",
    )(x, whg, wqT, wkn, wvT, wgT)


def _split3(a):
    hi = a.astype(BF16)
    r = a - hi.astype(F32)
    mid = r.astype(BF16)
    lo = (r - mid.astype(F32)).astype(BF16)
    return hi, mid, lo


def _hgrn_kernel(layer, hg_ref, lb_ref, ng_ref, o_ref, st_ref):
    C, SB, H, DK = HG_CHUNK, HG_SUB, HG_HEADS, HG_DK
    nsb = C // SB

    @pl.when(pl.program_id(1) == 0)
    def _():
        st_ref[...] = jnp.zeros_like(st_ref)

    lbp = lb_ref[...]
    e = jnp.exp(lbp - jnp.max(lbp, axis=0, keepdims=True))
    lb = jnp.sum(e[:layer + 1], axis=0, keepdims=True) / jnp.sum(e, axis=0, keepdims=True)

    ri = lax.broadcasted_iota(jnp.int32, (C, C), 0)
    ci = lax.broadcasted_iota(jnp.int32, (C, C), 1)
    tril = (ci <= ri).astype(BF16)
    t_sub = lax.broadcasted_iota(jnp.int32, (SB, H * DK), 0)
    lane16 = lax.broadcasted_iota(jnp.int32, (SB, SB), 1)
    ng = ng_ref[...]

    def chunk(c, carry):
        r0 = pl.multiple_of(c * C, C)
        blk = hg_ref[0, pl.ds(r0, C), :]
        q = blk[:, 0:512]
        f = lb + (1.0 - lb) * jax.nn.sigmoid(blk[:, 512:1024])
        lf = jnp.log(f)
        k = 1.0 - f
        v = blk[:, 1024:1536]
        gate = blk[:, 1536:2048]
        hi, mid, lo = _split3(lf)
        b = _nn(tril, hi) + _nn(tril, mid) + _nn(tril, lo)
        b_last = b[C - 1:C, :]
        qe = (q * jnp.exp(b)).astype(BF16)
        ks = (k * jnp.exp(b_last - b)).astype(BF16)
        dec = jnp.exp(b_last)
        vb = v.astype(BF16)
        rblk = jnp.concatenate([jnp.broadcast_to(b[i * SB:i * SB + 1, :], (SB, H * DK)) for i in range(nsb)], axis=0)
        qn = (q * jnp.exp(b - rblk)).astype(BF16)
        outs = []
        for h in range(H):
            sl = slice(h * DK, (h + 1) * DK)
            stT = st_ref[h]
            o_h = _nt(qe[:, sl], stT.astype(BF16))
            rows = []
            for i in range(nsb):
                rs = slice(i * SB, (i + 1) * SB)
                acc = jnp.zeros((SB, HG_DV), F32)
                if i > 0:
                    ke = (k[0:i * SB, sl] * jnp.exp(b[i * SB:i * SB + 1, sl] - b[0:i * SB, sl])).astype(BF16)
                    a_off = _nt(qn[rs, sl], ke)
                    acc = acc + _nn(a_off.astype(BF16), vb[0:i * SB, sl])
                qi, ki, bi = q[rs, sl], k[rs, sl], b[rs, sl]
                a_d = jnp.zeros((SB, SB), F32)
                for s in range(SB):
                    w = qi * ki[s:s + 1, :] * jnp.exp(jnp.minimum(bi - bi[s:s + 1, :], 0.0))
                    col = jnp.sum(w, axis=1, keepdims=True)
                    a_d = a_d + jnp.where((lane16 == s) & (t_sub[:, 0:SB] >= s), col, 0.0)
                acc = acc + _nn(a_d.astype(BF16), vb[rs, sl])
                rows.append(acc)
            o_h = o_h + jnp.concatenate(rows, axis=0)
            st_ref[h] = dec[:, sl] * stT + _tn(vb[:, sl], ks[:, sl])
            o_h = o_h * lax.rsqrt(jnp.mean(o_h * o_h, axis=1, keepdims=True) + LN_EPS) * ng
            outs.append(o_h)
        o = jnp.concatenate(outs, axis=1) * jax.nn.silu(gate)
        o_ref[0, pl.ds(r0, C), :] = o.astype(BF16)
        return carry

    lax.fori_loop(0, hg_ref.shape[1] // C, chunk, 0)


def _hgrn2(hg, hg_lb, norm_g_l, layer):
    B, S, _ = hg.shape
    tc = HG_TC
    return pl.pallas_call(
        functools.partial(_hgrn_kernel, layer),
        grid=(B, S // tc),
        in_specs=[pl.BlockSpec((1, tc, 2048), lambda b, i: (b, i, 0)),
                  pl.BlockSpec(hg_lb.shape, lambda b, i: (0, 0)),
                  pl.BlockSpec((1, HG_DV), lambda b, i: (0, 0))],
        out_specs=pl.BlockSpec((1, tc, HG_WIDTH), lambda b, i: (b, i, 0)),
        out_shape=jax.ShapeDtypeStruct((B, S, HG_WIDTH), BF16),
        scratch_shapes=[pltpu.VMEM((HG_HEADS, HG_DV, HG_DK), F32)],
        compiler_params=_cparams(("parallel", "arbitrary")),
        name="hgrn2",
    )(hg, hg_lb.astype(F32), norm_g_l.reshape(1, HG_DV).astype(F32))


def _compress_kernel(c_ref, pe_ref, w1_ref, b1_ref, w2_ref, w2T_ref, b2_ref, b2T_ref, o_ref, oT_ref):
    half = CMP_STRIDE * NSA_HD
    c = c_ref[0, 0, 0].astype(F32)
    pe = pe_ref[0]
    ca = (c + pe[:, 0:half]).astype(BF16)
    cb = (c + pe[:, half:2 * half]).astype(BF16)
    pa = _nn(ca, w1_ref[0, 0:half, :])
    pb = _nn(cb, w1_ref[0, half:2 * half, :])
    nbp = pa.shape[0]
    h = pa + pltpu.roll(pb, nbp - 1, 0) + b1_ref[0]
    h = jax.nn.gelu(h).astype(BF16)
    o_ref[0, 0, 0] = (_nn(h, w2_ref[0]) + b2_ref[0]).astype(BF16)
    oT_ref[0, 0, 0] = (_nt(w2T_ref[0], h) + b2T_ref[0]).astype(BF16)


def _compress(kn, cmpk, cmpv):
    B, _, G, S, HD = kn.shape
    nbp = S // CMP_STRIDE
    c = kn[:, 0:2].reshape(B, 2, G, nbp, CMP_STRIDE * HD)
    stack = lambda a, b, f: jnp.stack([f(a), f(b)], axis=0)
    pe = stack(cmpk[0], cmpv[0], lambda t: t.reshape(1, CMP_LEN * HD).astype(F32))
    w1 = stack(cmpk[1], cmpv[1], lambda t: t.astype(BF16))
    b1 = stack(cmpk[2], cmpv[2], lambda t: t.reshape(1, CMP_HIDDEN).astype(F32))
    w2 = stack(cmpk[3], cmpv[3], lambda t: t.astype(BF16))
    w2T = stack(cmpk[3], cmpv[3], lambda t: t.astype(BF16).T)
    b2 = stack(cmpk[4], cmpv[4], lambda t: t.reshape(1, HD).astype(F32))
    b2T = stack(cmpk[4], cmpv[4], lambda t: t.reshape(HD, 1).astype(F32))
    wspec = lambda a: pl.BlockSpec((1,) + a.shape[1:], lambda b, t, g: (t,) + (0,) * (a.ndim - 1))
    return pl.pallas_call(
        _compress_kernel,
        grid=(B, 2, G),
        in_specs=[pl.BlockSpec((1, 1, 1, nbp, CMP_STRIDE * HD), lambda b, t, g: (b, t, g, 0, 0)),
                  wspec(pe), wspec(w1), wspec(b1), wspec(w2), wspec(w2T), wspec(b2), wspec(b2T)],
        out_specs=(pl.BlockSpec((1, 1, 1, nbp, HD), lambda b, t, g: (b, t, g, 0, 0)),
                   pl.BlockSpec((1, 1, 1, HD, nbp), lambda b, t, g: (b, t, g, 0, 0))),
        out_shape=(jax.ShapeDtypeStruct((B, 2, G, nbp, HD), BF16),
                   jax.ShapeDtypeStruct((B, 2, G, HD, nbp), BF16)),
        compiler_params=_cparams(("parallel", "parallel", "parallel")),
        name="kv_compress",
    )(c, pe, w1, b1, w2, w2T, b2, b2T)


def _softmax_cols(s, mask):
    s = jnp.where(mask, s, NEG)
    m = jnp.max(s, axis=0, keepdims=True)
    p = jnp.where(mask, jnp.exp(s - m), 0.0)
    return p, jnp.sum(p, axis=0, keepdims=True)


def _nsa_kernel(qT_ref, kc_ref, vcT_ref, ks_ref, vsT_ref, kw_ref, vwT_ref, gT_ref, ovT_ref,
                o_ref, sc_ref, sel_ref):
    TQ, G, HD, KT = ATT_TQ, NSA_GROUP, NSA_HD, ATT_KT
    qi = pl.program_id(2)
    t0 = qi * TQ
    qT = qT_ref[0]
    q4 = jnp.concatenate([qT[r * HD:(r + 1) * HD, :] for r in range(G)], axis=1)
    pos1 = t0 + lax.broadcasted_iota(jnp.int32, (1, TQ), 1)
    pos4 = jnp.concatenate([pos1] * G, axis=1)

    nbp = kc_ref.shape[3]
    s_c = _nn(kc_ref[0, 0, 0], q4)
    n_io = lax.broadcasted_iota(jnp.int32, (nbp, 1), 0)
    p_c, l_c = _softmax_cols(s_c, (n_io * CMP_STRIDE + (CMP_LEN - 1)) <= pos4)
    p_c = p_c / jnp.maximum(l_c, 1e-30)
    o_c = _nn(vcT_ref[0, 0, 0], p_c.astype(BF16))
    p_sum = p_c[:, 0:TQ]
    for r in range(1, G):
        p_sum = p_sum + p_c[:, r * TQ:(r + 1) * TQ]
    imp = _nn(ovT_ref[...], p_sum.astype(BF16))

    ns = imp.shape[0]
    j_io = lax.broadcasted_iota(jnp.int32, (ns, 1), 0)
    cur = pos1 >> int(math.log2(SLC_LEN))
    forced = (j_io == 0) | (j_io == cur) | (j_io == cur - 1)
    score = jnp.where(forced, FORCE_SCORE, jnp.where(j_io * SLC_LEN <= pos1, imp, -1.0))
    sc_ref[...] = score

    def rank_body(i, cnt):
        row = sc_ref[pl.ds(i, 1), :]
        ahead = (row > score) | ((row == score) & (i < j_io))
        return cnt + ahead.astype(jnp.int32)

    cnt = lax.fori_loop(0, ns, rank_body, jnp.zeros((ns, TQ), jnp.int32))
    sel_ref[...] = (cnt < min(SLC_TOPK, ns)).astype(F32)

    bpt = KT // SLC_LEN
    k_io = lax.broadcasted_iota(jnp.int32, (KT, 1), 0)

    def sel_body(kt, carry):
        m, l, acc = carry
        k0 = pl.multiple_of(kt * KT, KT)
        s = _nn(ks_ref[0, 0, 0, pl.ds(k0, KT), :], q4)
        selt = sel_ref[pl.ds(pl.multiple_of(kt * bpt, bpt), bpt), :]
        m1 = jnp.concatenate([jnp.broadcast_to(selt[jb:jb + 1, :], (SLC_LEN, TQ)) for jb in range(bpt)], axis=0)
        mask = (jnp.concatenate([m1] * G, axis=1) > 0.5) & ((k0 + k_io) <= pos4)
        s = jnp.where(mask, s, NEG)
        m_new = jnp.maximum(m, jnp.max(s, axis=0, keepdims=True))
        alpha = jnp.exp(m - m_new)
        p = jnp.where(mask, jnp.exp(s - m_new), 0.0)
        l = alpha * l + jnp.sum(p, axis=0, keepdims=True)
        acc = alpha * acc + _nn(vsT_ref[0, 0, kt], p.astype(BF16))
        return m_new, l, acc

    n_kt = (t0 + TQ - 1) // KT + 1
    init = (jnp.full((1, G * TQ), NEG, F32), jnp.zeros((1, G * TQ), F32), jnp.zeros((HD, G * TQ), F32))
    _, l_s, acc_s = lax.fori_loop(0, n_kt, sel_body, init)
    o_s = acc_s / jnp.maximum(l_s, 1e-30)

    nwt = WIN // TQ + 1
    kw_t, vw_t, kp_t = [], [], []
    r_io = lax.broadcasted_iota(jnp.int32, (TQ, 1), 0)
    for i in range(nwt):
        idx = qi - (nwt - 1) + i
        idc = jnp.maximum(idx, 0)
        kw_t.append(kw_ref[0, 0, 0, pl.ds(pl.multiple_of(idc * TQ, TQ), TQ), :])
        vw_t.append(vwT_ref[0, 0, idc])
        kp_t.append(idx * TQ + r_io)
    kpos = jnp.concatenate(kp_t, axis=0)
    s_w = _nn(jnp.concatenate(kw_t, axis=0), q4)
    delta = pos4 - kpos
    p_w, l_w = _softmax_cols(s_w, (delta >= 0) & (delta < WIN) & (kpos >= 0))
    o_w = _nn(jnp.concatenate(vw_t, axis=1), p_w.astype(BF16)) / jnp.maximum(l_w, 1e-30)

    gs = jax.nn.sigmoid(gT_ref[0, 0])
    gate = lambda br: jnp.concatenate([gs[r * 3 + br:r * 3 + br + 1, :] for r in range(G)], axis=1)
    o = gate(0) * o_c + gate(1) * o_s + gate(2) * o_w
    o_ref[0] = jnp.concatenate([o[:, r * TQ:(r + 1) * TQ].T for r in range(G)], axis=1).astype(BF16)


def _overlap_T(S):
    nbp = S // CMP_STRIDE
    n_slc = S // SLC_LEN
    cs = np.arange(nbp) * CMP_STRIDE
    ss = np.arange(n_slc) * SLC_LEN
    ov = np.clip(np.minimum(cs[:, None] + CMP_LEN, ss[None, :] + SLC_LEN)
                 - np.maximum(cs[:, None], ss[None, :]), 0, None) / CMP_LEN
    ov[nbp - 1:, :] = 0.0
    return jnp.asarray(ov.T, BF16)


def _nsa(qT, cmp_n, cmp_T, kn, vsT, vwT, gT):
    B, _, S = qT.shape
    G2, TQ, HD = NSA_KV_HEADS, ATT_TQ, NSA_HD
    nbp = S // CMP_STRIDE
    ns = S // SLC_LEN
    ovT = _overlap_T(S)
    return pl.pallas_call(
        _nsa_kernel,
        grid=(B, G2, S // TQ),
        in_specs=[
            pl.BlockSpec((1, NSA_GROUP * HD, TQ), lambda b, g, i: (b, g, i)),
            pl.BlockSpec((1, 1, 1, nbp, HD), lambda b, g, i: (b, 0, g, 0, 0)),
            pl.BlockSpec((1, 1, 1, HD, nbp), lambda b, g, i: (b, 1, g, 0, 0)),
            pl.BlockSpec((1, 1, 1, S, HD), lambda b, g, i: (b, 2, g, 0, 0)),
            pl.BlockSpec((1, 1, S // ATT_KT, HD, ATT_KT), lambda b, g, i: (b, g, 0, 0, 0)),
            pl.BlockSpec((1, 1, 1, S, HD), lambda b, g, i: (b, 3, g, 0, 0)),
            pl.BlockSpec((1, 1, S // LANES, HD, LANES), lambda b, g, i: (b, g, 0, 0, 0)),
            pl.BlockSpec((1, 1, 16, TQ), lambda b, g, i: (b, g, 0, i)),
            pl.BlockSpec((ns, nbp), lambda b, g, i: (0, 0)),
        ],
        out_specs=pl.BlockSpec((1, TQ, NSA_GROUP * HD), lambda b, g, i: (b, i, g)),
        out_shape=jax.ShapeDtypeStruct((B, S, NSA_WIDTH), BF16),
        scratch_shapes=[pltpu.VMEM((ns, TQ), F32), pltpu.VMEM((ns, TQ), F32)],
        compiler_params=_cparams(("parallel", "parallel", "arbitrary")),
        name="nsa_attn",
    )(qT, cmp_n, cmp_T, kn, vsT, kn, vwT, gT, ovT)


def _layernorm(t, g, b):
    mu = jnp.mean(t, axis=-1, keepdims=True)
    d = t - mu
    var = jnp.mean(d * d, axis=-1, keepdims=True)
    return d * lax.rsqrt(var + LN_EPS) * g + b


def _outproj_kernel(alpha, ohg_ref, onsa_ref, x_ref, w_ref, g_ref, b_ref, h_ref):
    mix = _nn(ohg_ref[...], w_ref[0:HG_WIDTH, :]) + _nn(onsa_ref[...], w_ref[HG_WIDTH:HG_WIDTH + NSA_WIDTH, :])
    h_ref[...] = _layernorm(alpha * x_ref[...] + mix, g_ref[...], b_ref[...])


def _outproj(alpha, o_hg, o_nsa, x2, w_out_l, g, b):
    n, D = x2.shape
    tm = OUT_TM
    row = lambda w: pl.BlockSpec((tm, w), lambda i: (i, 0))
    const = lambda shp: pl.BlockSpec(shp, lambda i: (0, 0))
    return pl.pallas_call(
        functools.partial(_outproj_kernel, alpha),
        grid=(n // tm,),
        in_specs=[row(HG_WIDTH), row(NSA_WIDTH), row(D), const(w_out_l.shape), const((1, D)), const((1, D))],
        out_specs=row(D),
        out_shape=jax.ShapeDtypeStruct((n, D), F32),
        compiler_params=_cparams(("parallel",)),
        name="out_proj_ln",
    )(o_hg, o_nsa, x2, w_out_l.astype(BF16), g.reshape(1, D).astype(F32), b.reshape(1, D).astype(F32))


def _topk_rows(s, k):
    R = s.shape[0]
    r_io = lax.broadcasted_iota(jnp.int32, s.shape, 0)
    vals, idxs = [], []
    for _ in range(k):
        m = jnp.max(s, axis=0, keepdims=True)
        idx = jnp.min(jnp.where(s == m, r_io, R), axis=0, keepdims=True)
        vals.append(m)
        idxs.append(idx)
        s = jnp.where(r_io == idx, -jnp.inf, s)
    return jnp.concatenate(vals, axis=0), jnp.concatenate(idxs, axis=0)


def _route_kernel(h_ref, wq_ref, k1_ref, k2_ref, e_ref, g_ref):
    K, half = PEER_TOPK, PEER_DQ // 2
    q = _nn(h_ref[...].astype(BF16), wq_ref[...]).astype(BF16)
    for h in range(PEER_HEADS):
        s1 = _nt(k1_ref[...], q[:, h * PEER_DQ:h * PEER_DQ + half])
        s2 = _nt(k2_ref[...], q[:, h * PEER_DQ + half:(h + 1) * PEER_DQ])
        v1, i1 = _topk_rows(s1, K)
        v2, i2 = _topk_rows(s2, K)
        cand = jnp.concatenate([v1[a:a + 1, :] + v2 for a in range(K)], axis=0)
        top_s, top_c = _topk_rows(cand, K)
        ca, cb = top_c >> int(math.log2(K)), top_c & (K - 1)
        e1 = jnp.zeros_like(top_c)
        e2 = jnp.zeros_like(top_c)
        for a in range(K):
            e1 = e1 + jnp.where(ca == a, i1[a:a + 1, :], 0)
            e2 = e2 + jnp.where(cb == a, i2[a:a + 1, :], 0)
        e_ref[h * K:(h + 1) * K, :] = e1 * PEER_NKEYS + e2
        ex = jnp.exp(top_s - top_s[0:1, :])
        g_ref[h * K:(h + 1) * K, :] = ex / jnp.sum(ex, axis=0, keepdims=True)


def _route(h1, wq, k1, k2):
    n, D = h1.shape
    tr = ROUTE_TR
    hk = PEER_HEADS * PEER_TOPK
    const = lambda shp: pl.BlockSpec(shp, lambda i: (0, 0))
    return pl.pallas_call(
        _route_kernel,
        grid=(n // tr,),
        in_specs=[pl.BlockSpec((tr, D), lambda i: (i, 0)), const(wq.shape), const(k1.shape), const(k2.shape)],
        out_specs=(pl.BlockSpec((hk, tr), lambda i: (0, i)), pl.BlockSpec((hk, tr), lambda i: (0, i))),
        out_shape=(jax.ShapeDtypeStruct((hk, n), jnp.int32), jax.ShapeDtypeStruct((hk, n), F32)),
        compiler_params=_cparams(("parallel",)),
        name="peer_route",
    )(h1, wq.astype(BF16), k1.astype(BF16), k2.astype(BF16))


def _pack_rows(t):
    half = t.shape[1] // 2
    tb = t.astype(BF16)
    lo = lax.bitcast_convert_type(tb[:, :half], jnp.uint16).astype(jnp.uint32)
    hi = lax.bitcast_convert_type(tb[:, half:], jnp.uint16).astype(jnp.uint32)
    return lo | (hi << 16)


def _sc_gather(table, idx):
    n = idx.shape[0]
    w = table.shape[1]
    per = n // SC_WORKERS
    steps = per // SC_WIN
    mesh = plsc.VectorSubcoreMesh(core_axis_name="c", subcore_axis_name="s")

    @pl.kernel(out_type=jax.ShapeDtypeStruct((n, w), table.dtype), mesh=mesh,
               scratch_types=[pltpu.VMEM((SC_WIN,), jnp.int32), pltpu.VMEM((SC_WIN, w), table.dtype)])
    def gather(t_hbm, i_hbm, o_hbm, idx_v, buf):
        wid = lax.axis_index("c") * (SC_WORKERS // 2) + lax.axis_index("s")
        base0 = wid * per

        @pl.loop(0, steps)
        def _(s):
            base = base0 + s * SC_WIN
            pltpu.sync_copy(i_hbm.at[pl.ds(base, SC_WIN)], idx_v)
            pltpu.sync_copy(t_hbm.at[idx_v], buf)
            pltpu.sync_copy(buf, o_hbm.at[pl.ds(base, SC_WIN)])

    return gather(table, idx)


def _unpack(w):
    lo = lax.bitcast_convert_type(w << 16, F32)
    hi = lax.bitcast_convert_type(w & jnp.uint32(0xFFFF0000), F32)
    return lo, hi


def _mix_kernel(alpha, h_ref, gate_ref, ug_ref, vg_ref, g_ref, b_ref, o_ref):
    TT, HK = MIX_TT, PEER_HEADS * PEER_TOPK
    x = h_ref[...]
    half = x.shape[1] // 2
    eye = lax.broadcasted_iota(jnp.int32, (HK, HK), 0) == lax.broadcasted_iota(jnp.int32, (HK, HK), 1)
    rows = []
    for t in range(TT):
        ulo, uhi = _unpack(ug_ref[t * HK:(t + 1) * HK, :])
        xt = x[t:t + 1, :]
        hcol = jnp.sum(ulo * xt[:, :half] + uhi * xt[:, half:], axis=1, keepdims=True)
        grow = jnp.broadcast_to(gate_ref[t:t + 1, :], (HK, HK))
        gcol = jnp.sum(jnp.where(eye, grow, 0.0), axis=1, keepdims=True)
        wcol = gcol * jax.nn.gelu(hcol)
        vlo, vhi = _unpack(vg_ref[t * HK:(t + 1) * HK, :])
        rows.append(jnp.concatenate([jnp.sum(wcol * vlo, axis=0, keepdims=True),
                                     jnp.sum(wcol * vhi, axis=0, keepdims=True)], axis=1))
    ffn = jnp.concatenate(rows, axis=0)
    o_ref[...] = _layernorm(alpha * x + ffn, g_ref[...], b_ref[...])


def _mix(alpha, h1, gates, ug, vg, g, b):
    n, D = h1.shape
    tt = MIX_TT
    hk = PEER_HEADS * PEER_TOPK
    const = lambda shp: pl.BlockSpec(shp, lambda i: (0, 0))
    return pl.pallas_call(
        functools.partial(_mix_kernel, alpha),
        grid=(n // tt,),
        in_specs=[pl.BlockSpec((tt, D), lambda i: (i, 0)), pl.BlockSpec((tt, hk), lambda i: (i, 0)),
                  pl.BlockSpec((tt * hk, D // 2), lambda i: (i, 0)), pl.BlockSpec((tt * hk, D // 2), lambda i: (i, 0)),
                  const((1, D)), const((1, D))],
        out_specs=pl.BlockSpec((tt, D), lambda i: (i, 0)),
        out_shape=jax.ShapeDtypeStruct((n, D), F32),
        compiler_params=_cparams(("parallel",)),
        name="peer_mix_ln",
    )(h1, gates, ug, vg, g.reshape(1, D).astype(F32), b.reshape(1, D).astype(F32))


PEER_CHUNK = 4096


def _peer_ffn_ln(alpha, h1, wq, k1, k2, u, v, g, b):
    n, D = h1.shape
    hk = PEER_HEADS * PEER_TOPK
    eT, gT = _route(h1, wq, k1, k2)
    experts = eT.T
    gates = gT.T
    up, vp = _pack_rows(u), _pack_rows(v)
    ch = min(PEER_CHUNK, n)
    outs = []
    for c in range(n // ch):
        idx = experts[c * ch:(c + 1) * ch].reshape(ch * hk)
        ug = _sc_gather(up, idx)
        vg = _sc_gather(vp, idx)
        outs.append(_mix(alpha, h1[c * ch:(c + 1) * ch], gates[c * ch:(c + 1) * ch], ug, vg, g, b))
    return jnp.concatenate(outs, axis=0)


def kernel(x, w_in, w_out, hg_lb, hg_norm_g, cmpk_pe, cmpk_w1, cmpk_b1, cmpk_w2, cmpk_b2,
           cmpv_pe, cmpv_w1, cmpv_b1, cmpv_w2, cmpv_b2, ln1_g, ln1_b,
           peer_wq, peer_k1, peer_k2, peer_u, peer_v, ln2_g, ln2_b):
    B, S, D = x.shape
    depth = w_in.shape[0]
    alpha = (2.0 * depth) ** 0.25
    h = x
    for l in range(depth):
        hg, qT, kn, vsT, vwT, gT = _project(h, w_in[l])
        o_hg = _hgrn2(hg, hg_lb, hg_norm_g[l], l)
        cmp_n, cmp_T = _compress(kn, (cmpk_pe[l], cmpk_w1[l], cmpk_b1[l], cmpk_w2[l], cmpk_b2[l]),
                                 (cmpv_pe[l], cmpv_w1[l], cmpv_b1[l], cmpv_w2[l], cmpv_b2[l]))
        o_nsa = _nsa(qT, cmp_n, cmp_T, kn, vsT, vwT, gT)
        h1 = _outproj(alpha, o_hg.reshape(B * S, HG_WIDTH), o_nsa.reshape(B * S, NSA_WIDTH),
                      h.reshape(B * S, D), w_out[l], ln1_g[l], ln1_b[l])
        h2 = _peer_ffn_ln(alpha, h1, peer_wq[l], peer_k1[l], peer_k2[l], peer_u[l], peer_v[l], ln2_g[l], ln2_b[l])
        h = h2.reshape(B, S, D)
    return h
```

```python
import functools
import math

import jax
import jax.numpy as jnp
import numpy as np
from jax import lax
from jax.experimental import pallas as pl
from jax.experimental.pallas import tpu as pltpu
from jax.experimental.pallas import tpu_sc as plsc

F32 = jnp.float32
BF16 = jnp.bfloat16

HG_HEADS = 4
HG_DK = 128
HG_DV = 128
HG_CHUNK = 64
HG_SUB = 16
HG_WIDTH = HG_HEADS * HG_DV
NSA_HEADS = 8
NSA_KV_HEADS = 2
NSA_GROUP = NSA_HEADS // NSA_KV_HEADS
NSA_HD = 64
NSA_WIDTH = NSA_HEADS * NSA_HD
CMP_LEN = 32
CMP_STRIDE = 16
CMP_HIDDEN = 256
SLC_LEN = 64
SLC_TOPK = 16
WIN = 512
FORCE_SCORE = 1.0e4
PEER_HEADS = 8
PEER_DQ = 256
PEER_NKEYS = 128
PEER_TOPK = 16
LN_EPS = 1e-5
NEG = -1e30

LANES = 128
PROJ_TM = 512
HG_TC = 256
ATT_TQ = 128
ATT_KT = 512
OUT_TM = 512
ROUTE_TR = 256
MIX_TT = 16
SC_WIN = 32
SC_NBUF = 4
SC_WORKERS = 32
VMEM_LIMIT = 56 * 1024 * 1024


def _cparams(sem):
    return pltpu.CompilerParams(dimension_semantics=sem, vmem_limit_bytes=VMEM_LIMIT)


def _nt(a, b):
    return lax.dot_general(a, b, (((1,), (1,)), ((), ())), preferred_element_type=F32)


def _tn(a, b):
    return lax.dot_general(a, b, (((0,), (0,)), ((), ())), preferred_element_type=F32)


def _nn(a, b):
    return jnp.dot(a, b, preferred_element_type=F32)


def _proj_kernel(x_ref, whg_ref, wqT_ref, wkn_ref, wvT_ref, wgT_ref,
                 hg_ref, qT_ref, kn_ref, vsT_ref, vwT_ref, gT_ref):
    xb = x_ref[0].astype(BF16)
    hg_ref[0] = _nn(xb, whg_ref[...])
    qT_ref[0] = (_nt(wqT_ref[...], xb) * (NSA_HD ** -0.5)).astype(BF16)
    kn = _nn(xb, wkn_ref[...])
    for j in range(4):
        for g in range(NSA_KV_HEADS):
            lo = j * 2 * NSA_HD + g * NSA_HD
            kn_ref[0, j, g] = kn[:, lo:lo + NSA_HD].astype(BF16)
    vT = _nt(wvT_ref[...], xb).astype(BF16)
    tm = vT.shape[1]
    for g in range(NSA_KV_HEADS):
        for c in range(tm // ATT_KT):
            vsT_ref[0, g, c] = vT[g * NSA_HD:(g + 1) * NSA_HD, c * ATT_KT:(c + 1) * ATT_KT]
        for c in range(tm // LANES):
            vwT_ref[0, g, c] = vT[(2 + g) * NSA_HD:(3 + g) * NSA_HD, c * LANES:(c + 1) * LANES]
    gT = _nt(wgT_ref[...], xb)
    gT_ref[0, 0] = gT[0:16]
    gT_ref[0, 1] = gT[16:32]


def _project(x, w_in_l):
    B, S, D = x.shape
    tm = PROJ_TM
    o = np.cumsum((0, 512, 512, 512, 512, 512, 128, 128, 128, 128, 128, 128, 24))
    wb = w_in_l.astype(BF16)
    whg = wb[:, o[0]:o[4]]
    wqT = wb[:, o[4]:o[5]].T
    wkn = jnp.concatenate([wb[:, o[5]:o[6]], wb[:, o[6]:o[7]], wb[:, o[7]:o[8]], wb[:, o[9]:o[10]]], axis=1)
    wvT = jnp.concatenate([wb[:, o[8]:o[9]], wb[:, o[10]:o[11]]], axis=1).T
    wg = wb[:, o[11]:o[12]].reshape(D, NSA_KV_HEADS, NSA_GROUP * 3)
    wgT = jnp.pad(wg, ((0, 0), (0, 0), (0, 16 - NSA_GROUP * 3))).reshape(D, 32).T
    const = lambda shp: pl.BlockSpec(shp, lambda b, i: (0,) * len(shp))
    out_shape = (
        jax.ShapeDtypeStruct((B, S, 4 * 512), F32),
        jax.ShapeDtypeStruct((B, NSA_WIDTH, S), BF16),
        jax.ShapeDtypeStruct((B, 4, NSA_KV_HEADS, S, NSA_HD), BF16),
        jax.ShapeDtypeStruct((B, NSA_KV_HEADS, S // ATT_KT, NSA_HD, ATT_KT), BF16),
        jax.ShapeDtypeStruct((B, NSA_KV_HEADS, S // LANES, NSA_HD, LANES), BF16),
        jax.ShapeDtypeStruct((B, NSA_KV_HEADS, 16, S), F32),
    )
    out_specs = (
        pl.BlockSpec((1, tm, 2048), lambda b, i: (b, i, 0)),
        pl.BlockSpec((1, NSA_WIDTH, tm), lambda b, i: (b, 0, i)),
        pl.BlockSpec((1, 4, NSA_KV_HEADS, tm, NSA_HD), lambda b, i: (b, 0, 0, i, 0)),
        pl.BlockSpec((1, NSA_KV_HEADS, tm // ATT_KT, NSA_HD, ATT_KT), lambda b, i: (b, 0, i, 0, 0)),
        pl.BlockSpec((1, NSA_KV_HEADS, tm // LANES, NSA_HD, LANES), lambda b, i: (b, 0, i, 0, 0)),
        pl.BlockSpec((1, NSA_KV_HEADS, 16, tm), lambda b, i: (b, 0, 0, i)),
    )
    return pl.pallas_call(
        _proj_kernel,
        grid=(B, S // tm),
        in_specs=[pl.BlockSpec((1, tm, D), lambda b, i: (b, i, 0)),
                  const(whg.shape), const(wqT.shape), const(wkn.shape), const(wvT.shape), const(wgT.shape)],
        out_specs=out_specs,
        out_shape=out_shape,
        compiler_params=_cparams(("parallel", "arbitrary")),
        name="in_proj",
    )(x, whg, wqT, wkn, wvT, wgT)


def _split3(a):
    hi = a.astype(BF16)
    r = a - hi.astype(F32)
    mid = r.astype(BF16)
    lo = (r - mid.astype(F32)).astype(BF16)
    return hi, mid, lo


def _hgrn_kernel(layer, hg_ref, lb_ref, ng_ref, o_ref, st_ref):
    C, SB, H, DK = HG_CHUNK, HG_SUB, HG_HEADS, HG_DK
    nsb = C // SB

    @pl.when(pl.program_id(1) == 0)
    def _():
        st_ref[...] = jnp.zeros_like(st_ref)

    lbp = lb_ref[...]
    e = jnp.exp(lbp - jnp.max(lbp, axis=0, keepdims=True))
    lb = jnp.sum(e[:layer + 1], axis=0, keepdims=True) / jnp.sum(e, axis=0, keepdims=True)

    ri = lax.broadcasted_iota(jnp.int32, (C, C), 0)
    ci = lax.broadcasted_iota(jnp.int32, (C, C), 1)
    tril = (ci <= ri).astype(BF16)
    t_sub = lax.broadcasted_iota(jnp.int32, (SB, H * DK), 0)
    lane16 = lax.broadcasted_iota(jnp.int32, (SB, SB), 1)
    ng = ng_ref[...]

    def chunk(c, carry):
        r0 = pl.multiple_of(c * C, C)
        blk = hg_ref[0, pl.ds(r0, C), :]
        q = blk[:, 0:512]
        f = lb + (1.0 - lb) * jax.nn.sigmoid(blk[:, 512:1024])
        lf = jnp.log(f)
        k = 1.0 - f
        v = blk[:, 1024:1536]
        gate = blk[:, 1536:2048]
        hi, mid, lo = _split3(lf)
        b = _nn(tril, hi) + _nn(tril, mid) + _nn(tril, lo)
        b_last = b[C - 1:C, :]
        qe = (q * jnp.exp(b)).astype(BF16)
        ks = (k * jnp.exp(b_last - b)).astype(BF16)
        dec = jnp.exp(b_last)
        vb = v.astype(BF16)
        rblk = jnp.concatenate([jnp.broadcast_to(b[i * SB:i * SB + 1, :], (SB, H * DK)) for i in range(nsb)], axis=0)
        qn = (q * jnp.exp(b - rblk)).astype(BF16)
        outs = []
        for h in range(H):
            sl = slice(h * DK, (h + 1) * DK)
            stT = st_ref[h]
            o_h = _nt(qe[:, sl], stT.astype(BF16))
            rows = []
            for i in range(nsb):
                rs = slice(i * SB, (i + 1) * SB)
                acc = jnp.zeros((SB, HG_DV), F32)
                if i > 0:
                    ke = (k[0:i * SB, sl] * jnp.exp(b[i * SB:i * SB + 1, sl] - b[0:i * SB, sl])).astype(BF16)
                    a_off = _nt(qn[rs, sl], ke)
                    acc = acc + _nn(a_off.astype(BF16), vb[0:i * SB, sl])
                qi, ki, bi = q[rs, sl], k[rs, sl], b[rs, sl]
                a_d = jnp.zeros((SB, SB), F32)
                for s in range(SB):
                    w = qi * ki[s:s + 1, :] * jnp.exp(jnp.minimum(bi - bi[s:s + 1, :], 0.0))
                    col = jnp.sum(w, axis=1, keepdims=True)
                    a_d = a_d + jnp.where((lane16 == s) & (t_sub[:, 0:SB] >= s), col, 0.0)
                acc = acc + _nn(a_d.astype(BF16), vb[rs, sl])
                rows.append(acc)
            o_h = o_h + jnp.concatenate(rows, axis=0)
            st_ref[h] = dec[:, sl] * stT + _tn(vb[:, sl], ks[:, sl])
            o_h = o_h * lax.rsqrt(jnp.mean(o_h * o_h, axis=1, keepdims=True) + LN_EPS) * ng
            outs.append(o_h)
        o = jnp.concatenate(outs, axis=1) * jax.nn.silu(gate)
        o_ref[0, pl.ds(r0, C), :] = o.astype(BF16)
        return carry

    lax.fori_loop(0, hg_ref.shape[1] // C, chunk, 0)


def _hgrn2(hg, hg_lb, norm_g_l, layer):
    B, S, _ = hg.shape
    tc = HG_TC
    return pl.pallas_call(
        functools.partial(_hgrn_kernel, layer),
        grid=(B, S // tc),
        in_specs=[pl.BlockSpec((1, tc, 2048), lambda b, i: (b, i, 0)),
                  pl.BlockSpec(hg_lb.shape, lambda b, i: (0, 0)),
                  pl.BlockSpec((1, HG_DV), lambda b, i: (0, 0))],
        out_specs=pl.BlockSpec((1, tc, HG_WIDTH), lambda b, i: (b, i, 0)),
        out_shape=jax.ShapeDtypeStruct((B, S, HG_WIDTH), BF16),
        scratch_shapes=[pltpu.VMEM((HG_HEADS, HG_DV, HG_DK), F32)],
        compiler_params=_cparams(("parallel", "arbitrary")),
        name="hgrn2",
    )(hg, hg_lb.astype(F32), norm_g_l.reshape(1, HG_DV).astype(F32))


def _compress_kernel(c_ref, pe_ref, w1_ref, b1_ref, w2_ref, w2T_ref, b2_ref, b2T_ref, o_ref, oT_ref):
    half = CMP_STRIDE * NSA_HD
    c = c_ref[0, 0, 0].astype(F32)
    pe = pe_ref[0]
    ca = (c + pe[:, 0:half]).astype(BF16)
    cb = (c + pe[:, half:2 * half]).astype(BF16)
    pa = _nn(ca, w1_ref[0, 0:half, :])
    pb = _nn(cb, w1_ref[0, half:2 * half, :])
    nbp = pa.shape[0]
    h = pa + pltpu.roll(pb, nbp - 1, 0) + b1_ref[0]
    h = jax.nn.gelu(h).astype(BF16)
    o_ref[0, 0, 0] = (_nn(h, w2_ref[0]) + b2_ref[0]).astype(BF16)
    oT_ref[0, 0, 0] = (_nt(w2T_ref[0], h) + b2T_ref[0]).astype(BF16)


def _compress(kn, cmpk, cmpv):
    B, _, G, S, HD = kn.shape
    nbp = S // CMP_STRIDE
    c = kn[:, 0:2].reshape(B, 2, G, nbp, CMP_STRIDE * HD)
    stack = lambda a, b, f: jnp.stack([f(a), f(b)], axis=0)
    pe = stack(cmpk[0], cmpv[0], lambda t: t.reshape(1, CMP_LEN * HD).astype(F32))
    w1 = stack(cmpk[1], cmpv[1], lambda t: t.astype(BF16))
    b1 = stack(cmpk[2], cmpv[2], lambda t: t.reshape(1, CMP_HIDDEN).astype(F32))
    w2 = stack(cmpk[3], cmpv[3], lambda t: t.astype(BF16))
    w2T = stack(cmpk[3], cmpv[3], lambda t: t.astype(BF16).T)
    b2 = stack(cmpk[4], cmpv[4], lambda t: t.reshape(1, HD).astype(F32))
    b2T = stack(cmpk[4], cmpv[4], lambda t: t.reshape(HD, 1).astype(F32))
    wspec = lambda a: pl.BlockSpec((1,) + a.shape[1:], lambda b, t, g: (t,) + (0,) * (a.ndim - 1))
    return pl.pallas_call(
        _compress_kernel,
        grid=(B, 2, G),
        in_specs=[pl.BlockSpec((1, 1, 1, nbp, CMP_STRIDE * HD), lambda b, t, g: (b, t, g, 0, 0)),
                  wspec(pe), wspec(w1), wspec(b1), wspec(w2), wspec(w2T), wspec(b2), wspec(b2T)],
        out_specs=(pl.BlockSpec((1, 1, 1, nbp, HD), lambda b, t, g: (b, t, g, 0, 0)),
                   pl.BlockSpec((1, 1, 1, HD, nbp), lambda b, t, g: (b, t, g, 0, 0))),
        out_shape=(jax.ShapeDtypeStruct((B, 2, G, nbp, HD), BF16),
                   jax.ShapeDtypeStruct((B, 2, G, HD, nbp), BF16)),
        compiler_params=_cparams(("parallel", "parallel", "parallel")),
        name="kv_compress",
    )(c, pe, w1, b1, w2, w2T, b2, b2T)


def _softmax_cols(s, mask):
    s = jnp.where(mask, s, NEG)
    m = jnp.max(s, axis=0, keepdims=True)
    p = jnp.where(mask, jnp.exp(s - m), 0.0)
    return p, jnp.sum(p, axis=0, keepdims=True)


def _nsa_kernel(qT_ref, kc_ref, vcT_ref, ks_ref, vsT_ref, kw_ref, vwT_ref, gT_ref, ovT_ref,
                o_ref, sc_ref, sel_ref):
    TQ, G, HD, KT = ATT_TQ, NSA_GROUP, NSA_HD, ATT_KT
    qi = pl.program_id(2)
    t0 = qi * TQ
    qT = qT_ref[0]
    q4 = jnp.concatenate([qT[r * HD:(r + 1) * HD, :] for r in range(G)], axis=1)
    pos1 = t0 + lax.broadcasted_iota(jnp.int32, (1, TQ), 1)
    pos4 = jnp.concatenate([pos1] * G, axis=1)

    nbp = kc_ref.shape[3]
    s_c = _nn(kc_ref[0, 0, 0], q4)
    n_io = lax.broadcasted_iota(jnp.int32, (nbp, 1), 0)
    p_c, l_c = _softmax_cols(s_c, (n_io * CMP_STRIDE + (CMP_LEN - 1)) <= pos4)
    p_c = p_c / jnp.maximum(l_c, 1e-30)
    o_c = _nn(vcT_ref[0, 0, 0], p_c.astype(BF16))
    p_sum = p_c[:, 0:TQ]
    for r in range(1, G):
        p_sum = p_sum + p_c[:, r * TQ:(r + 1) * TQ]
    imp = _nn(ovT_ref[...], p_sum.astype(BF16))

    ns = imp.shape[0]
    j_io = lax.broadcasted_iota(jnp.int32, (ns, 1), 0)
    cur = pos1 >> int(math.log2(SLC_LEN))
    forced = (j_io == 0) | (j_io == cur) | (j_io == cur - 1)
    score = jnp.where(forced, FORCE_SCORE, jnp.where(j_io * SLC_LEN <= pos1, imp, -1.0))
    sc_ref[...] = score

    def rank_body(i, cnt):
        row = sc_ref[pl.ds(i, 1), :]
        ahead = (row > score) | ((row == score) & (i < j_io))
        return cnt + ahead.astype(jnp.int32)

    cnt = lax.fori_loop(0, ns, rank_body, jnp.zeros((ns, TQ), jnp.int32))
    sel_ref[...] = (cnt < min(SLC_TOPK, ns)).astype(F32)

    bpt = KT // SLC_LEN
    k_io = lax.broadcasted_iota(jnp.int32, (KT, 1), 0)

    def sel_body(kt, carry):
        m, l, acc = carry
        k0 = pl.multiple_of(kt * KT, KT)
        s = _nn(ks_ref[0, 0, 0, pl.ds(k0, KT), :], q4)
        selt = sel_ref[pl.ds(pl.multiple_of(kt * bpt, bpt), bpt), :]
        m1 = jnp.concatenate([jnp.broadcast_to(selt[jb:jb + 1, :], (SLC_LEN, TQ)) for jb in range(bpt)], axis=0)
        mask = (jnp.concatenate([m1] * G, axis=1) > 0.5) & ((k0 + k_io) <= pos4)
        s = jnp.where(mask, s, NEG)
        m_new = jnp.maximum(m, jnp.max(s, axis=0, keepdims=True))
        alpha = jnp.exp(m - m_new)
        p = jnp.where(mask, jnp.exp(s - m_new), 0.0)
        l = alpha * l + jnp.sum(p, axis=0, keepdims=True)
        acc = alpha * acc + _nn(vsT_ref[0, 0, kt], p.astype(BF16))
        return m_new, l, acc

    n_kt = (t0 + TQ - 1) // KT + 1
    init = (jnp.full((1, G * TQ), NEG, F32), jnp.zeros((1, G * TQ), F32), jnp.zeros((HD, G * TQ), F32))
    _, l_s, acc_s = lax.fori_loop(0, n_kt, sel_body, init)
    o_s = acc_s / jnp.maximum(l_s, 1e-30)

    nwt = WIN // TQ + 1
    kw_t, vw_t, kp_t = [], [], []
    r_io = lax.broadcasted_iota(jnp.int32, (TQ, 1), 0)
    for i in range(nwt):
        idx = qi - (nwt - 1) + i
        idc = jnp.maximum(idx, 0)
        kw_t.append(kw_ref[0, 0, 0, pl.ds(pl.multiple_of(idc * TQ, TQ), TQ), :])
        vw_t.append(vwT_ref[0, 0, idc])
        kp_t.append(idx * TQ + r_io)
    kpos = jnp.concatenate(kp_t, axis=0)
    s_w = _nn(jnp.concatenate(kw_t, axis=0), q4)
    delta = pos4 - kpos
    p_w, l_w = _softmax_cols(s_w, (delta >= 0) & (delta < WIN) & (kpos >= 0))
    o_w = _nn(jnp.concatenate(vw_t, axis=1), p_w.astype(BF16)) / jnp.maximum(l_w, 1e-30)

    gs = jax.nn.sigmoid(gT_ref[0, 0])
    gate = lambda br: jnp.concatenate([gs[r * 3 + br:r * 3 + br + 1, :] for r in range(G)], axis=1)
    o = gate(0) * o_c + gate(1) * o_s + gate(2) * o_w
    o_ref[0] = jnp.concatenate([o[:, r * TQ:(r + 1) * TQ].T for r in range(G)], axis=1).astype(BF16)


def _overlap_T(S):
    nbp = S // CMP_STRIDE
    n_slc = S // SLC_LEN
    cs = np.arange(nbp) * CMP_STRIDE
    ss = np.arange(n_slc) * SLC_LEN
    ov = np.clip(np.minimum(cs[:, None] + CMP_LEN, ss[None, :] + SLC_LEN)
                 - np.maximum(cs[:, None], ss[None, :]), 0, None) / CMP_LEN
    ov[nbp - 1:, :] = 0.0
    return jnp.asarray(ov.T, BF16)


def _nsa(qT, cmp_n, cmp_T, kn, vsT, vwT, gT):
    B, _, S = qT.shape
    G2, TQ, HD = NSA_KV_HEADS, ATT_TQ, NSA_HD
    nbp = S // CMP_STRIDE
    ns = S // SLC_LEN
    ovT = _overlap_T(S)
    return pl.pallas_call(
        _nsa_kernel,
        grid=(B, G2, S // TQ),
        in_specs=[
            pl.BlockSpec((1, NSA_GROUP * HD, TQ), lambda b, g, i: (b, g, i)),
            pl.BlockSpec((1, 1, 1, nbp, HD), lambda b, g, i: (b, 0, g, 0, 0)),
            pl.BlockSpec((1, 1, 1, HD, nbp), lambda b, g, i: (b, 1, g, 0, 0)),
            pl.BlockSpec((1, 1, 1, S, HD), lambda b, g, i: (b, 2, g, 0, 0)),
            pl.BlockSpec((1, 1, S // ATT_KT, HD, ATT_KT), lambda b, g, i: (b, g, 0, 0, 0)),
            pl.BlockSpec((1, 1, 1, S, HD), lambda b, g, i: (b, 3, g, 0, 0)),
            pl.BlockSpec((1, 1, S // LANES, HD, LANES), lambda b, g, i: (b, g, 0, 0, 0)),
            pl.BlockSpec((1, 1, 16, TQ), lambda b, g, i: (b, g, 0, i)),
            pl.BlockSpec((ns, nbp), lambda b, g, i: (0, 0)),
        ],
        out_specs=pl.BlockSpec((1, TQ, NSA_GROUP * HD), lambda b, g, i: (b, i, g)),
        out_shape=jax.ShapeDtypeStruct((B, S, NSA_WIDTH), BF16),
        scratch_shapes=[pltpu.VMEM((ns, TQ), F32), pltpu.VMEM((ns, TQ), F32)],
        compiler_params=_cparams(("parallel", "parallel", "arbitrary")),
        name="nsa_attn",
    )(qT, cmp_n, cmp_T, kn, vsT, kn, vwT, gT, ovT)


def _layernorm(t, g, b):
    mu = jnp.mean(t, axis=-1, keepdims=True)
    d = t - mu
    var = jnp.mean(d * d, axis=-1, keepdims=True)
    return d * lax.rsqrt(var + LN_EPS) * g + b


def _outproj_kernel(alpha, ohg_ref, onsa_ref, x_ref, w_ref, g_ref, b_ref, h_ref):
    mix = _nn(ohg_ref[...], w_ref[0:HG_WIDTH, :]) + _nn(onsa_ref[...], w_ref[HG_WIDTH:HG_WIDTH + NSA_WIDTH, :])
    h_ref[...] = _layernorm(alpha * x_ref[...] + mix, g_ref[...], b_ref[...])


def _outproj(alpha, o_hg, o_nsa, x2, w_out_l, g, b):
    n, D = x2.shape
    tm = OUT_TM
    row = lambda w: pl.BlockSpec((tm, w), lambda i: (i, 0))
    const = lambda shp: pl.BlockSpec(shp, lambda i: (0, 0))
    return pl.pallas_call(
        functools.partial(_outproj_kernel, alpha),
        grid=(n // tm,),
        in_specs=[row(HG_WIDTH), row(NSA_WIDTH), row(D), const(w_out_l.shape), const((1, D)), const((1, D))],
        out_specs=row(D),
        out_shape=jax.ShapeDtypeStruct((n, D), F32),
        compiler_params=_cparams(("parallel",)),
        name="out_proj_ln",
    )(o_hg, o_nsa, x2, w_out_l.astype(BF16), g.reshape(1, D).astype(F32), b.reshape(1, D).astype(F32))


def _topk_rows(s, k):
    R = s.shape[0]
    r_io = lax.broadcasted_iota(jnp.int32, s.shape, 0)
    vals, idxs = [], []
    for _ in range(k):
        m = jnp.max(s, axis=0, keepdims=True)
        idx = jnp.min(jnp.where(s == m, r_io, R), axis=0, keepdims=True)
        vals.append(m)
        idxs.append(idx)
        s = jnp.where(r_io == idx, -jnp.inf, s)
    return jnp.concatenate(vals, axis=0), jnp.concatenate(idxs, axis=0)


def _route_kernel(h_ref, wq_ref, k1_ref, k2_ref, e_ref, g_ref):
    K, half = PEER_TOPK, PEER_DQ // 2
    q = _nn(h_ref[...].astype(BF16), wq_ref[...]).astype(BF16)
    for h in range(PEER_HEADS):
        s1 = _nt(k1_ref[...], q[:, h * PEER_DQ:h * PEER_DQ + half])
        s2 = _nt(k2_ref[...], q[:, h * PEER_DQ + half:(h + 1) * PEER_DQ])
        v1, i1 = _topk_rows(s1, K)
        v2, i2 = _topk_rows(s2, K)
        cand = jnp.concatenate([v1[a:a + 1, :] + v2 for a in range(K)], axis=0)
        top_s, top_c = _topk_rows(cand, K)
        ca, cb = top_c >> int(math.log2(K)), top_c & (K - 1)
        e1 = jnp.zeros_like(top_c)
        e2 = jnp.zeros_like(top_c)
        for a in range(K):
            e1 = e1 + jnp.where(ca == a, i1[a:a + 1, :], 0)
            e2 = e2 + jnp.where(cb == a, i2[a:a + 1, :], 0)
        e_ref[h * K:(h + 1) * K, :] = e1 * PEER_NKEYS + e2
        ex = jnp.exp(top_s - top_s[0:1, :])
        g_ref[h * K:(h + 1) * K, :] = ex / jnp.sum(ex, axis=0, keepdims=True)


def _route(h1, wq, k1, k2):
    n, D = h1.shape
    tr = ROUTE_TR
    hk = PEER_HEADS * PEER_TOPK
    const = lambda shp: pl.BlockSpec(shp, lambda i: (0, 0))
    return pl.pallas_call(
        _route_kernel,
        grid=(n // tr,),
        in_specs=[pl.BlockSpec((tr, D), lambda i: (i, 0)), const(wq.shape), const(k1.shape), const(k2.shape)],
        out_specs=(pl.BlockSpec((hk, tr), lambda i: (0, i)), pl.BlockSpec((hk, tr), lambda i: (0, i))),
        out_shape=(jax.ShapeDtypeStruct((hk, n), jnp.int32), jax.ShapeDtypeStruct((hk, n), F32)),
        compiler_params=_cparams(("parallel",)),
        name="peer_route",
    )(h1, wq.astype(BF16), k1.astype(BF16), k2.astype(BF16))


def _pack_rows(t):
    half = t.shape[1] // 2
    tb = t.astype(BF16)
    lo = lax.bitcast_convert_type(tb[:, :half], jnp.uint16).astype(jnp.uint32)
    hi = lax.bitcast_convert_type(tb[:, half:], jnp.uint16).astype(jnp.uint32)
    return lo | (hi << 16)


def _sc_gather(table, idx):
    n = idx.shape[0]
    w = table.shape[1]
    per = n // SC_WORKERS
    steps, rem = divmod(per, SC_WIN)
    assert rem == 0 and steps % SC_NBUF == 0 and n % SC_WORKERS == 0
    mesh = plsc.VectorSubcoreMesh(core_axis_name="c", subcore_axis_name="s")

    @pl.kernel(out_type=jax.ShapeDtypeStruct((n, w), table.dtype), mesh=mesh,
               scratch_types=[pltpu.VMEM((per,), jnp.int32), pltpu.VMEM((SC_NBUF, SC_WIN, w), table.dtype),
                              pltpu.SemaphoreType.DMA((SC_NBUF,)), pltpu.SemaphoreType.DMA((SC_NBUF,))])
    def gather(t_hbm, i_hbm, o_hbm, idx_v, buf, gsem, wsem):
        wid = lax.axis_index("c") * (SC_WORKERS // 2) + lax.axis_index("s")
        base0 = wid * per
        pltpu.sync_copy(i_hbm.at[pl.ds(base0, per)], idx_v)

        def fetch(s, slot):
            return pltpu.make_async_copy(t_hbm.at[idx_v.at[pl.ds(s * SC_WIN, SC_WIN)]], buf.at[slot], gsem.at[slot])

        def flush(s, slot):
            return pltpu.make_async_copy(buf.at[slot], o_hbm.at[pl.ds(base0 + s * SC_WIN, SC_WIN)], wsem.at[slot])

        for j in range(SC_NBUF - 1):
            fetch(j, j).start()

        @pl.loop(0, steps, step=SC_NBUF)
        def _(s0):
            for j in range(SC_NBUF):
                s = s0 + j
                prev = (j - 1) % SC_NBUF
                fetch(s, j).wait()
                flush(s, j).start()

                @pl.when(s > 0)
                def _():
                    flush(s - 1, prev).wait()

                @pl.when(s + SC_NBUF - 1 < steps)
                def _():
                    fetch(s + SC_NBUF - 1, prev).start()

        flush(steps - 1, (steps - 1) % SC_NBUF).wait()

    return gather(table, idx)


def _unpack(w):
    lo = lax.bitcast_convert_type(w << 16, F32)
    hi = lax.bitcast_convert_type(w & jnp.uint32(0xFFFF0000), F32)
    return lo, hi


def _mix_kernel(alpha, h_ref, gate_ref, ug_ref, vg_ref, g_ref, b_ref, o_ref):
    TT, HK = MIX_TT, PEER_HEADS * PEER_TOPK
    x = h_ref[...]
    half = x.shape[1] // 2
    eye = lax.broadcasted_iota(jnp.int32, (HK, HK), 0) == lax.broadcasted_iota(jnp.int32, (HK, HK), 1)
    rows = []
    for t in range(TT):
        ulo, uhi = _unpack(ug_ref[t * HK:(t + 1) * HK, :])
        xt = x[t:t + 1, :]
        hcol = jnp.sum(ulo * xt[:, :half] + uhi * xt[:, half:], axis=1, keepdims=True)
        grow = jnp.broadcast_to(gate_ref[t:t + 1, :], (HK, HK))
        gcol = jnp.sum(jnp.where(eye, grow, 0.0), axis=1, keepdims=True)
        wcol = gcol * jax.nn.gelu(hcol)
        vlo, vhi = _unpack(vg_ref[t * HK:(t + 1) * HK, :])
        rows.append(jnp.concatenate([jnp.sum(wcol * vlo, axis=0, keepdims=True),
                                     jnp.sum(wcol * vhi, axis=0, keepdims=True)], axis=1))
    ffn = jnp.concatenate(rows, axis=0)
    o_ref[...] = _layernorm(alpha * x + ffn, g_ref[...], b_ref[...])


def _mix(alpha, h1, gates, ug, vg, g, b):
    n, D = h1.shape
    tt = MIX_TT
    hk = PEER_HEADS * PEER_TOPK
    const = lambda shp: pl.BlockSpec(shp, lambda i: (0, 0))
    return pl.pallas_call(
        functools.partial(_mix_kernel, alpha),
        grid=(n // tt,),
        in_specs=[pl.BlockSpec((tt, D), lambda i: (i, 0)), pl.BlockSpec((tt, hk), lambda i: (i, 0)),
                  pl.BlockSpec((tt * hk, D // 2), lambda i: (i, 0)), pl.BlockSpec((tt * hk, D // 2), lambda i: (i, 0)),
                  const((1, D)), const((1, D))],
        out_specs=pl.BlockSpec((tt, D), lambda i: (i, 0)),
        out_shape=jax.ShapeDtypeStruct((n, D), F32),
        compiler_params=_cparams(("parallel",)),
        name="peer_mix_ln",
    )(h1, gates, ug, vg, g.reshape(1, D).astype(F32), b.reshape(1, D).astype(F32))


PEER_CHUNK = 4096


def _peer_ffn_ln(alpha, h1, wq, k1, k2, up, vp, g, b):
    n, D = h1.shape
    hk = PEER_HEADS * PEER_TOPK
    eT, gT = _route(h1, wq, k1, k2)
    experts = eT.T
    gates = gT.T
    ch = min(PEER_CHUNK, n)
    outs = []
    for c in range(n // ch):
        idx = experts[c * ch:(c + 1) * ch].reshape(ch * hk)
        ug = _sc_gather(up, idx)
        vg = _sc_gather(vp, idx)
        outs.append(_mix(alpha, h1[c * ch:(c + 1) * ch], gates[c * ch:(c + 1) * ch], ug, vg, g, b))
    return jnp.concatenate(outs, axis=0)


def kernel(x, w_in, w_out, hg_lb, hg_norm_g, cmpk_pe, cmpk_w1, cmpk_b1, cmpk_w2, cmpk_b2,
           cmpv_pe, cmpv_w1, cmpv_b1, cmpv_w2, cmpv_b2, ln1_g, ln1_b,
           peer_wq, peer_k1, peer_k2, peer_u, peer_v, ln2_g, ln2_b):
    B, S, D = x.shape
    depth = w_in.shape[0]
    alpha = (2.0 * depth) ** 0.25
    h = x
    for l in range(depth):
        up, vp = _pack_rows(peer_u[l]), _pack_rows(peer_v[l])
        cmpk = (cmpk_pe[l], cmpk_w1[l], cmpk_b1[l], cmpk_w2[l], cmpk_b2[l])
        cmpv = (cmpv_pe[l], cmpv_w1[l], cmpv_b1[l], cmpv_w2[l], cmpv_b2[l])
        outs = []
        for bi in range(B):
            hb = h[bi:bi + 1]
            hg, qT, kn, vsT, vwT, gT = _project(hb, w_in[l])
            o_hg = _hgrn2(hg, hg_lb, hg_norm_g[l], l)
            cmp_n, cmp_T = _compress(kn, cmpk, cmpv)
            o_nsa = _nsa(qT, cmp_n, cmp_T, kn, vsT, vwT, gT)
            h1 = _outproj(alpha, o_hg.reshape(S, HG_WIDTH), o_nsa.reshape(S, NSA_WIDTH),
                          hb.reshape(S, D), w_out[l], ln1_g[l], ln1_b[l])
            outs.append(_peer_ffn_ln(alpha, h1, peer_wq[l], peer_k1[l], peer_k2[l], up, vp, ln2_g[l], ln2_b[l]))
        h = jnp.stack(outs, axis=0)
    return h
```

```python
import functools
import math

import jax
import jax.numpy as jnp
import numpy as np
from jax import lax
from jax.experimental import pallas as pl
from jax.experimental.pallas import tpu as pltpu
from jax.experimental.pallas import tpu_sc as plsc

F32 = jnp.float32
BF16 = jnp.bfloat16

HG_HEADS = 4
HG_DK = 128
HG_DV = 128
HG_CHUNK = 64
HG_SUB = 16
HG_WIDTH = HG_HEADS * HG_DV
NSA_HEADS = 8
NSA_KV_HEADS = 2
NSA_GROUP = NSA_HEADS // NSA_KV_HEADS
NSA_HD = 64
NSA_WIDTH = NSA_HEADS * NSA_HD
CMP_LEN = 32
CMP_STRIDE = 16
CMP_HIDDEN = 256
SLC_LEN = 64
SLC_TOPK = 16
WIN = 512
FORCE_SCORE = 1.0e4
PEER_HEADS = 8
PEER_DQ = 256
PEER_NKEYS = 128
PEER_TOPK = 16
LN_EPS = 1e-5
NEG = -1e30
LOG2E = math.log2(math.e)

LANES = 128
PROJ_TM = 512
HG_TC = 256
ATT_TQ = 128
ATT_KT = 512
ATT_KS = 1024
OUT_TM = 512
ROUTE_TR = 256
MIX_TT = 16
SC_WIN = 32
SC_NBUF = 4
SC_WORKERS = 32
VMEM_LIMIT = 56 * 1024 * 1024


def _cparams(sem):
    return pltpu.CompilerParams(dimension_semantics=sem, vmem_limit_bytes=VMEM_LIMIT)


def _nt(a, b):
    return lax.dot_general(a, b, (((1,), (1,)), ((), ())), preferred_element_type=F32)


def _tn(a, b):
    return lax.dot_general(a, b, (((0,), (0,)), ((), ())), preferred_element_type=F32)


def _nn(a, b):
    return jnp.dot(a, b, preferred_element_type=F32)


def _proj_kernel(x_ref, whg_ref, wqT_ref, wkn_ref, wvT_ref, wgT_ref,
                 hg_ref, qT_ref, kn_ref, ksE_ref, vsT_ref, vwT_ref, gT_ref):
    xb = x_ref[0].astype(BF16)
    hg_ref[0] = _nn(xb, whg_ref[...])
    qT_ref[0] = (_nt(wqT_ref[...], xb) * (NSA_HD ** -0.5 * LOG2E)).astype(BF16)
    kn = _nn(xb, wkn_ref[...])
    for j in range(3):
        for g in range(NSA_KV_HEADS):
            lo = j * 2 * NSA_HD + g * NSA_HD
            kn_ref[0, j, g] = kn[:, lo:lo + NSA_HD].astype(BF16)
    row = pl.program_id(1) * kn.shape[0] + lax.broadcasted_iota(jnp.int32, (kn.shape[0], NSA_HD), 0)
    blk = (row >> int(math.log2(SLC_LEN))) & (ATT_KS // SLC_LEN - 1)
    onehot = (lax.broadcasted_iota(jnp.int32, (kn.shape[0], NSA_HD), 1) == blk).astype(F32)
    for g in range(NSA_KV_HEADS):
        lo = 3 * 2 * NSA_HD + g * NSA_HD
        ksE_ref[0, g] = jnp.concatenate([kn[:, lo:lo + NSA_HD], onehot], axis=1).astype(BF16)
    vT = _nt(wvT_ref[...], xb).astype(BF16)
    tm = vT.shape[1]
    for g in range(NSA_KV_HEADS):
        for c in range(tm // ATT_KT):
            vsT_ref[0, g, c] = vT[g * NSA_HD:(g + 1) * NSA_HD, c * ATT_KT:(c + 1) * ATT_KT]
        for c in range(tm // LANES):
            vwT_ref[0, g, c] = vT[(2 + g) * NSA_HD:(3 + g) * NSA_HD, c * LANES:(c + 1) * LANES]
    gT = _nt(wgT_ref[...], xb)
    gT_ref[0, 0] = gT[0:16]
    gT_ref[0, 1] = gT[16:32]


def _project(x, w_in_l):
    B, S, D = x.shape
    tm = PROJ_TM
    o = np.cumsum((0, 512, 512, 512, 512, 512, 128, 128, 128, 128, 128, 128, 24))
    wb = w_in_l.astype(BF16)
    whg = wb[:, o[0]:o[4]]
    wqT = wb[:, o[4]:o[5]].T
    wkn = jnp.concatenate([wb[:, o[5]:o[6]], wb[:, o[6]:o[7]], wb[:, o[9]:o[10]], wb[:, o[7]:o[8]]], axis=1)
    wvT = jnp.concatenate([wb[:, o[8]:o[9]], wb[:, o[10]:o[11]]], axis=1).T
    wg = wb[:, o[11]:o[12]].reshape(D, NSA_KV_HEADS, NSA_GROUP * 3)
    wgT = jnp.pad(wg, ((0, 0), (0, 0), (0, 16 - NSA_GROUP * 3))).reshape(D, 32).T
    const = lambda shp: pl.BlockSpec(shp, lambda b, i: (0,) * len(shp))
    out_shape = (
        jax.ShapeDtypeStruct((B, S, 4 * 512), F32),
        jax.ShapeDtypeStruct((B, NSA_WIDTH, S), BF16),
        jax.ShapeDtypeStruct((B, 3, NSA_KV_HEADS, S, NSA_HD), BF16),
        jax.ShapeDtypeStruct((B, NSA_KV_HEADS, S, 2 * NSA_HD), BF16),
        jax.ShapeDtypeStruct((B, NSA_KV_HEADS, S // ATT_KT, NSA_HD, ATT_KT), BF16),
        jax.ShapeDtypeStruct((B, NSA_KV_HEADS, S // LANES, NSA_HD, LANES), BF16),
        jax.ShapeDtypeStruct((B, NSA_KV_HEADS, 16, S), F32),
    )
    out_specs = (
        pl.BlockSpec((1, tm, 2048), lambda b, i: (b, i, 0)),
        pl.BlockSpec((1, NSA_WIDTH, tm), lambda b, i: (b, 0, i)),
        pl.BlockSpec((1, 3, NSA_KV_HEADS, tm, NSA_HD), lambda b, i: (b, 0, 0, i, 0)),
        pl.BlockSpec((1, NSA_KV_HEADS, tm, 2 * NSA_HD), lambda b, i: (b, 0, i, 0)),
        pl.BlockSpec((1, NSA_KV_HEADS, tm // ATT_KT, NSA_HD, ATT_KT), lambda b, i: (b, 0, i, 0, 0)),
        pl.BlockSpec((1, NSA_KV_HEADS, tm // LANES, NSA_HD, LANES), lambda b, i: (b, 0, i, 0, 0)),
        pl.BlockSpec((1, NSA_KV_HEADS, 16, tm), lambda b, i: (b, 0, 0, i)),
    )
    return pl.pallas_call(
        _proj_kernel,
        grid=(B, S // tm),
        in_specs=[pl.BlockSpec((1, tm, D), lambda b, i: (b, i, 0)),
                  const(whg.shape), const(wqT.shape), const(wkn.shape), const(wvT.shape), const(wgT.shape)],
        out_specs=out_specs,
        out_shape=out_shape,
        compiler_params=_cparams(("parallel", "arbitrary")),
        name="in_proj",
    )(x, whg, wqT, wkn, wvT, wgT)


def _split3(a):
    hi = a.astype(BF16)
    r = a - hi.astype(F32)
    mid = r.astype(BF16)
    lo = (r - mid.astype(F32)).astype(BF16)
    return hi, mid, lo


def _hgrn_kernel(layer, hg_ref, lb_ref, ng_ref, o_ref, st_ref):
    C, SB, H, DK = HG_CHUNK, HG_SUB, HG_HEADS, HG_DK
    nsb = C // SB

    @pl.when(pl.program_id(1) == 0)
    def _():
        st_ref[...] = jnp.zeros_like(st_ref)

    lbp = lb_ref[...]
    e = jnp.exp(lbp - jnp.max(lbp, axis=0, keepdims=True))
    lb = jnp.sum(e[:layer + 1], axis=0, keepdims=True) / jnp.sum(e, axis=0, keepdims=True)

    ri = lax.broadcasted_iota(jnp.int32, (C, C), 0)
    ci = lax.broadcasted_iota(jnp.int32, (C, C), 1)
    tril = (ci <= ri).astype(BF16)
    t_sub = lax.broadcasted_iota(jnp.int32, (SB, H * DK), 0)
    lane16 = lax.broadcasted_iota(jnp.int32, (SB, SB), 1)
    ng = ng_ref[...]

    def chunk(c, carry):
        r0 = pl.multiple_of(c * C, C)
        blk = hg_ref[0, pl.ds(r0, C), :]
        q = blk[:, 0:512]
        f = lb + (1.0 - lb) * jax.nn.sigmoid(blk[:, 512:1024])
        lf = jnp.log(f)
        k = 1.0 - f
        v = blk[:, 1024:1536]
        gate = blk[:, 1536:2048]
        hi, mid, lo = _split3(lf)
        b = _nn(tril, hi) + _nn(tril, mid) + _nn(tril, lo)
        b_last = b[C - 1:C, :]
        qe = (q * jnp.exp(b)).astype(BF16)
        ks = (k * jnp.exp(b_last - b)).astype(BF16)
        dec = jnp.exp(b_last)
        vb = v.astype(BF16)
        rblk = jnp.concatenate([jnp.broadcast_to(b[i * SB:i * SB + 1, :], (SB, H * DK)) for i in range(nsb)], axis=0)
        qn = (q * jnp.exp(b - rblk)).astype(BF16)
        outs = []
        for h in range(H):
            sl = slice(h * DK, (h + 1) * DK)
            stT = st_ref[h]
            o_h = _nt(qe[:, sl], stT.astype(BF16))
            rows = []
            for i in range(nsb):
                rs = slice(i * SB, (i + 1) * SB)
                acc = jnp.zeros((SB, HG_DV), F32)
                if i > 0:
                    ke = (k[0:i * SB, sl] * jnp.exp(b[i * SB:i * SB + 1, sl] - b[0:i * SB, sl])).astype(BF16)
                    a_off = _nt(qn[rs, sl], ke)
                    acc = acc + _nn(a_off.astype(BF16), vb[0:i * SB, sl])
                qi, ki, bi = q[rs, sl], k[rs, sl], b[rs, sl]
                a_d = jnp.zeros((SB, SB), F32)
                for s in range(SB):
                    w = qi * ki[s:s + 1, :] * jnp.exp(jnp.minimum(bi - bi[s:s + 1, :], 0.0))
                    col = jnp.sum(w, axis=1, keepdims=True)
                    a_d = a_d + jnp.where((lane16 == s) & (t_sub[:, 0:SB] >= s), col, 0.0)
                acc = acc + _nn(a_d.astype(BF16), vb[rs, sl])
                rows.append(acc)
            o_h = o_h + jnp.concatenate(rows, axis=0)
            st_ref[h] = dec[:, sl] * stT + _tn(vb[:, sl], ks[:, sl])
            o_h = o_h * lax.rsqrt(jnp.mean(o_h * o_h, axis=1, keepdims=True) + LN_EPS) * ng
            outs.append(o_h)
        o = jnp.concatenate(outs, axis=1) * jax.nn.silu(gate)
        o_ref[0, pl.ds(r0, C), :] = o.astype(BF16)
        return carry

    lax.fori_loop(0, hg_ref.shape[1] // C, chunk, 0)


def _hgrn2(hg, hg_lb, norm_g_l, layer):
    B, S, _ = hg.shape
    tc = HG_TC
    return pl.pallas_call(
        functools.partial(_hgrn_kernel, layer),
        grid=(B, S // tc),
        in_specs=[pl.BlockSpec((1, tc, 2048), lambda b, i: (b, i, 0)),
                  pl.BlockSpec(hg_lb.shape, lambda b, i: (0, 0)),
                  pl.BlockSpec((1, HG_DV), lambda b, i: (0, 0))],
        out_specs=pl.BlockSpec((1, tc, HG_WIDTH), lambda b, i: (b, i, 0)),
        out_shape=jax.ShapeDtypeStruct((B, S, HG_WIDTH), BF16),
        scratch_shapes=[pltpu.VMEM((HG_HEADS, HG_DV, HG_DK), F32)],
        compiler_params=_cparams(("parallel", "arbitrary")),
        name="hgrn2",
    )(hg, hg_lb.astype(F32), norm_g_l.reshape(1, HG_DV).astype(F32))


def _compress_kernel(c_ref, pe_ref, w1_ref, b1_ref, w2_ref, w2T_ref, b2_ref, b2T_ref, o_ref, oT_ref):
    half = CMP_STRIDE * NSA_HD
    c = c_ref[0, 0, 0].astype(F32)
    pe = pe_ref[0]
    ca = (c + pe[:, 0:half]).astype(BF16)
    cb = (c + pe[:, half:2 * half]).astype(BF16)
    pa = _nn(ca, w1_ref[0, 0:half, :])
    pb = _nn(cb, w1_ref[0, half:2 * half, :])
    nbp = pa.shape[0]
    h = pa + pltpu.roll(pb, nbp - 1, 0) + b1_ref[0]
    h = jax.nn.gelu(h).astype(BF16)
    o_ref[0, 0, 0] = (_nn(h, w2_ref[0]) + b2_ref[0]).astype(BF16)
    oT_ref[0, 0, 0] = (_nt(w2T_ref[0], h) + b2T_ref[0]).astype(BF16)


def _compress(kn, cmpk, cmpv):
    B, _, G, S, HD = kn.shape
    nbp = S // CMP_STRIDE
    c = kn[:, 0:2].reshape(B, 2, G, nbp, CMP_STRIDE * HD)
    stack = lambda a, b, f: jnp.stack([f(a), f(b)], axis=0)
    pe = stack(cmpk[0], cmpv[0], lambda t: t.reshape(1, CMP_LEN * HD).astype(F32))
    w1 = stack(cmpk[1], cmpv[1], lambda t: t.astype(BF16))
    b1 = stack(cmpk[2], cmpv[2], lambda t: t.reshape(1, CMP_HIDDEN).astype(F32))
    w2 = stack(cmpk[3], cmpv[3], lambda t: t.astype(BF16))
    w2T = stack(cmpk[3], cmpv[3], lambda t: t.astype(BF16).T)
    b2 = stack(cmpk[4], cmpv[4], lambda t: t.reshape(1, HD).astype(F32))
    b2T = stack(cmpk[4], cmpv[4], lambda t: t.reshape(HD, 1).astype(F32))
    wspec = lambda a: pl.BlockSpec((1,) + a.shape[1:], lambda b, t, g: (t,) + (0,) * (a.ndim - 1))
    return pl.pallas_call(
        _compress_kernel,
        grid=(B, 2, G),
        in_specs=[pl.BlockSpec((1, 1, 1, nbp, CMP_STRIDE * HD), lambda b, t, g: (b, t, g, 0, 0)),
                  wspec(pe), wspec(w1), wspec(b1), wspec(w2), wspec(w2T), wspec(b2), wspec(b2T)],
        out_specs=(pl.BlockSpec((1, 1, 1, nbp, HD), lambda b, t, g: (b, t, g, 0, 0)),
                   pl.BlockSpec((1, 1, 1, HD, nbp), lambda b, t, g: (b, t, g, 0, 0))),
        out_shape=(jax.ShapeDtypeStruct((B, 2, G, nbp, HD), BF16),
                   jax.ShapeDtypeStruct((B, 2, G, HD, nbp), BF16)),
        compiler_params=_cparams(("parallel", "parallel", "parallel")),
        name="kv_compress",
    )(c, pe, w1, b1, w2, w2T, b2, b2T)


def _topk_mask_rows(score, k):
    R = score.shape[0]
    r_io = lax.broadcasted_iota(jnp.int32, score.shape, 0)
    s = score
    for _ in range(k):
        m = jnp.max(s, axis=0, keepdims=True)
        idx = jnp.min(jnp.where(s == m, r_io, R), axis=0, keepdims=True)
        s = jnp.where(r_io == idx, -jnp.inf, s)
    return (s == -jnp.inf).astype(F32)


def _with_ones_row(vT):
    pad = (lax.broadcasted_iota(jnp.int32, (16, vT.shape[1]), 0) == 0).astype(BF16)
    return jnp.concatenate([vT, pad], axis=0)


def _nsa_kernel(qT_ref, kc_ref, vcT_ref, ks_ref, vsT_ref, kw_ref, vwT_ref, gT_ref, ovT_ref,
                o_ref, sel_ref):
    TQ, G, HD, KT = ATT_TQ, NSA_GROUP, NSA_HD, ATT_KT
    qi = pl.program_id(2)
    t0 = qi * TQ
    qT = qT_ref[0]
    q4 = jnp.concatenate([qT[r * HD:(r + 1) * HD, :] for r in range(G)], axis=1)
    pos1 = t0 + lax.broadcasted_iota(jnp.int32, (1, TQ), 1)
    pos4 = jnp.concatenate([pos1] * G, axis=1)

    nbp = kc_ref.shape[3]
    s_c = _nn(kc_ref[0, 0, 0], q4)
    n_io = lax.broadcasted_iota(jnp.int32, (nbp, 1), 0)
    mask_c = (n_io * CMP_STRIDE + (CMP_LEN - 1)) <= pos4
    s_c = jnp.where(mask_c, s_c, NEG)
    p_c = jnp.where(mask_c, jnp.exp2(s_c - jnp.max(s_c, axis=0, keepdims=True)), 0.0)
    p_c = p_c / jnp.maximum(jnp.sum(p_c, axis=0, keepdims=True), 1e-30)
    o_c = _nn(vcT_ref[0, 0, 0], p_c.astype(BF16))
    p_sum = p_c[:, 0:TQ]
    for r in range(1, G):
        p_sum = p_sum + p_c[:, r * TQ:(r + 1) * TQ]
    imp = _nn(ovT_ref[...], p_sum.astype(BF16))

    ns = imp.shape[0]
    j_io = lax.broadcasted_iota(jnp.int32, (ns, 1), 0)
    cur = pos1 >> int(math.log2(SLC_LEN))
    forced = (j_io == 0) | (j_io == cur) | (j_io == cur - 1)
    score = jnp.where(forced, FORCE_SCORE, jnp.where(j_io * SLC_LEN <= pos1, imp, -1.0))
    sel_ref[...] = (_topk_mask_rows(score, min(SLC_TOPK, ns)) - 1.0) * (-NEG)

    KS = ATT_KS
    bps = KS // SLC_LEN
    k_io = lax.broadcasted_iota(jnp.int32, (KS, 1), 0)
    zrows = jnp.zeros((HD - bps, G * TQ), BF16)

    def sel_step(ks_i, carry, diagonal):
        m, acc = carry
        k0 = pl.multiple_of(ks_i * KS, KS)
        bias = sel_ref[pl.ds(pl.multiple_of(ks_i * bps, bps), bps), :].astype(BF16)
        rhs = jnp.concatenate([q4, jnp.concatenate([bias] * G, axis=1), zrows], axis=0)
        s = _nn(ks_ref[0, 0, pl.ds(k0, KS), :], rhs)
        if diagonal:
            s = jnp.where((k0 + k_io) <= pos4, s, NEG)
        m_new = jnp.maximum(m, jnp.max(s, axis=0, keepdims=True))
        p = jnp.exp2(s - m_new).astype(BF16)
        acc = jnp.exp2(m - m_new) * acc
        for c in range(KS // KT):
            acc = acc + _nn(_with_ones_row(vsT_ref[0, 0, ks_i * (KS // KT) + c]), p[c * KT:(c + 1) * KT])
        return m_new, acc

    n_ks = (t0 + TQ - 1) // KS + 1
    init = (jnp.full((1, G * TQ), NEG, F32), jnp.zeros((HD + 16, G * TQ), F32))
    carry = lax.fori_loop(0, n_ks - 1, lambda i, c: sel_step(i, c, False), init)
    _, acc_s = sel_step(n_ks - 1, carry, True)
    o_s = acc_s[0:HD] / jnp.maximum(acc_s[HD:HD + 1], 1e-30)

    nwt = WIN // TQ + 1
    kw_t, vw_t, kp_t = [], [], []
    r_io = lax.broadcasted_iota(jnp.int32, (TQ, 1), 0)
    for i in range(nwt):
        idx = qi - (nwt - 1) + i
        idc = jnp.maximum(idx, 0)
        kw_t.append(kw_ref[0, 0, 0, pl.ds(pl.multiple_of(idc * TQ, TQ), TQ), :])
        vw_t.append(vwT_ref[0, 0, idc])
        kp_t.append(idx * TQ + r_io)
    kpos = jnp.concatenate(kp_t, axis=0)
    s_w = _nn(jnp.concatenate(kw_t, axis=0), q4)
    delta = pos4 - kpos
    s_w = jnp.where((delta >= 0) & (delta < WIN) & (kpos >= 0), s_w, NEG)
    p_w = jnp.exp2(s_w - jnp.max(s_w, axis=0, keepdims=True)).astype(BF16)
    acc_w = _nn(_with_ones_row(jnp.concatenate(vw_t, axis=1)), p_w)
    o_w = acc_w[0:HD] / jnp.maximum(acc_w[HD:HD + 1], 1e-30)

    gs = jax.nn.sigmoid(gT_ref[0, 0])
    gate = lambda br: jnp.concatenate([gs[r * 3 + br:r * 3 + br + 1, :] for r in range(G)], axis=1)
    o = gate(0) * o_c + gate(1) * o_s + gate(2) * o_w
    o_ref[0] = jnp.concatenate([o[:, r * TQ:(r + 1) * TQ].T for r in range(G)], axis=1).astype(BF16)


def _overlap_T(S):
    nbp = S // CMP_STRIDE
    n_slc = S // SLC_LEN
    cs = np.arange(nbp) * CMP_STRIDE
    ss = np.arange(n_slc) * SLC_LEN
    ov = np.clip(np.minimum(cs[:, None] + CMP_LEN, ss[None, :] + SLC_LEN)
                 - np.maximum(cs[:, None], ss[None, :]), 0, None) / CMP_LEN
    ov[nbp - 1:, :] = 0.0
    return jnp.asarray(ov.T, BF16)


def _nsa(qT, cmp_n, cmp_T, kn, ksE, vsT, vwT, gT):
    B, _, S = qT.shape
    G2, TQ, HD = NSA_KV_HEADS, ATT_TQ, NSA_HD
    nbp = S // CMP_STRIDE
    ns = S // SLC_LEN
    ovT = _overlap_T(S)
    return pl.pallas_call(
        _nsa_kernel,
        grid=(B, G2, S // TQ),
        in_specs=[
            pl.BlockSpec((1, NSA_GROUP * HD, TQ), lambda b, g, i: (b, g, i)),
            pl.BlockSpec((1, 1, 1, nbp, HD), lambda b, g, i: (b, 0, g, 0, 0)),
            pl.BlockSpec((1, 1, 1, HD, nbp), lambda b, g, i: (b, 1, g, 0, 0)),
            pl.BlockSpec((1, 1, S, 2 * HD), lambda b, g, i: (b, g, 0, 0)),
            pl.BlockSpec((1, 1, S // ATT_KT, HD, ATT_KT), lambda b, g, i: (b, g, 0, 0, 0)),
            pl.BlockSpec((1, 1, 1, S, HD), lambda b, g, i: (b, 2, g, 0, 0)),
            pl.BlockSpec((1, 1, S // LANES, HD, LANES), lambda b, g, i: (b, g, 0, 0, 0)),
            pl.BlockSpec((1, 1, 16, TQ), lambda b, g, i: (b, g, 0, i)),
            pl.BlockSpec((ns, nbp), lambda b, g, i: (0, 0)),
        ],
        out_specs=pl.BlockSpec((1, TQ, NSA_GROUP * HD), lambda b, g, i: (b, i, g)),
        out_shape=jax.ShapeDtypeStruct((B, S, NSA_WIDTH), BF16),
        scratch_shapes=[pltpu.VMEM((ns, TQ), F32)],
        compiler_params=_cparams(("parallel", "parallel", "arbitrary")),
        name="nsa_attn",
    )(qT, cmp_n, cmp_T, ksE, vsT, kn, vwT, gT, ovT)


def _layernorm(t, g, b):
    mu = jnp.mean(t, axis=-1, keepdims=True)
    d = t - mu
    var = jnp.mean(d * d, axis=-1, keepdims=True)
    return d * lax.rsqrt(var + LN_EPS) * g + b


def _outproj_kernel(alpha, ohg_ref, onsa_ref, x_ref, w_ref, g_ref, b_ref, h_ref):
    mix = _nn(ohg_ref[...], w_ref[0:HG_WIDTH, :]) + _nn(onsa_ref[...], w_ref[HG_WIDTH:HG_WIDTH + NSA_WIDTH, :])
    h_ref[...] = _layernorm(alpha * x_ref[...] + mix, g_ref[...], b_ref[...])


def _outproj(alpha, o_hg, o_nsa, x2, w_out_l, g, b):
    n, D = x2.shape
    tm = OUT_TM
    row = lambda w: pl.BlockSpec((tm, w), lambda i: (i, 0))
    const = lambda shp: pl.BlockSpec(shp, lambda i: (0, 0))
    return pl.pallas_call(
        functools.partial(_outproj_kernel, alpha),
        grid=(n // tm,),
        in_specs=[row(HG_WIDTH), row(NSA_WIDTH), row(D), const(w_out_l.shape), const((1, D)), const((1, D))],
        out_specs=row(D),
        out_shape=jax.ShapeDtypeStruct((n, D), F32),
        compiler_params=_cparams(("parallel",)),
        name="out_proj_ln",
    )(o_hg, o_nsa, x2, w_out_l.astype(BF16), g.reshape(1, D).astype(F32), b.reshape(1, D).astype(F32))


def _topk_rows(s, k):
    R = s.shape[0]
    r_io = lax.broadcasted_iota(jnp.int32, s.shape, 0)
    vals, idxs = [], []
    for _ in range(k):
        m = jnp.max(s, axis=0, keepdims=True)
        idx = jnp.min(jnp.where(s == m, r_io, R), axis=0, keepdims=True)
        vals.append(m)
        idxs.append(idx)
        s = jnp.where(r_io == idx, -jnp.inf, s)
    return jnp.concatenate(vals, axis=0), jnp.concatenate(idxs, axis=0)


def _route_kernel(h_ref, wq_ref, k1_ref, k2_ref, e_ref, g_ref):
    K, half = PEER_TOPK, PEER_DQ // 2
    q = _nn(h_ref[...].astype(BF16), wq_ref[...]).astype(BF16)
    for h in range(PEER_HEADS):
        s1 = _nt(k1_ref[...], q[:, h * PEER_DQ:h * PEER_DQ + half])
        s2 = _nt(k2_ref[...], q[:, h * PEER_DQ + half:(h + 1) * PEER_DQ])
        v1, i1 = _topk_rows(s1, K)
        v2, i2 = _topk_rows(s2, K)
        cand = jnp.concatenate([v1[a:a + 1, :] + v2 for a in range(K)], axis=0)
        top_s, top_c = _topk_rows(cand, K)
        ca, cb = top_c >> int(math.log2(K)), top_c & (K - 1)
        e1 = jnp.zeros_like(top_c)
        e2 = jnp.zeros_like(top_c)
        for a in range(K):
            e1 = e1 + jnp.where(ca == a, i1[a:a + 1, :], 0)
            e2 = e2 + jnp.where(cb == a, i2[a:a + 1, :], 0)
        e_ref[h * K:(h + 1) * K, :] = e1 * PEER_NKEYS + e2
        ex = jnp.exp(top_s - top_s[0:1, :])
        g_ref[h * K:(h + 1) * K, :] = ex / jnp.sum(ex, axis=0, keepdims=True)


def _route(h1, wq, k1, k2):
    n, D = h1.shape
    tr = ROUTE_TR
    hk = PEER_HEADS * PEER_TOPK
    const = lambda shp: pl.BlockSpec(shp, lambda i: (0, 0))
    return pl.pallas_call(
        _route_kernel,
        grid=(n // tr,),
        in_specs=[pl.BlockSpec((tr, D), lambda i: (i, 0)), const(wq.shape), const(k1.shape), const(k2.shape)],
        out_specs=(pl.BlockSpec((hk, tr), lambda i: (0, i)), pl.BlockSpec((hk, tr), lambda i: (0, i))),
        out_shape=(jax.ShapeDtypeStruct((hk, n), jnp.int32), jax.ShapeDtypeStruct((hk, n), F32)),
        compiler_params=_cparams(("parallel",)),
        name="peer_route",
    )(h1, wq.astype(BF16), k1.astype(BF16), k2.astype(BF16))


def _pack_rows(t):
    half = t.shape[1] // 2
    tb = t.astype(BF16)
    lo = lax.bitcast_convert_type(tb[:, :half], jnp.uint16).astype(jnp.uint32)
    hi = lax.bitcast_convert_type(tb[:, half:], jnp.uint16).astype(jnp.uint32)
    return lo | (hi << 16)


def _sc_gather(table, idx):
    n = idx.shape[0]
    w = table.shape[1]
    per = n // SC_WORKERS
    steps, rem = divmod(per, SC_WIN)
    assert rem == 0 and steps % SC_NBUF == 0 and n % SC_WORKERS == 0
    mesh = plsc.VectorSubcoreMesh(core_axis_name="c", subcore_axis_name="s")

    @pl.kernel(out_type=jax.ShapeDtypeStruct((n, w), table.dtype), mesh=mesh,
               scratch_types=[pltpu.VMEM((per,), jnp.int32), pltpu.VMEM((SC_NBUF, SC_WIN, w), table.dtype),
                              pltpu.SemaphoreType.DMA((SC_NBUF,)), pltpu.SemaphoreType.DMA((SC_NBUF,))])
    def gather(t_hbm, i_hbm, o_hbm, idx_v, buf, gsem, wsem):
        wid = lax.axis_index("c") * (SC_WORKERS // 2) + lax.axis_index("s")
        base0 = wid * per
        pltpu.sync_copy(i_hbm.at[pl.ds(base0, per)], idx_v)

        def fetch(s, slot):
            return pltpu.make_async_copy(t_hbm.at[idx_v.at[pl.ds(s * SC_WIN, SC_WIN)]], buf.at[slot], gsem.at[slot])

        def flush(s, slot):
            return pltpu.make_async_copy(buf.at[slot], o_hbm.at[pl.ds(base0 + s * SC_WIN, SC_WIN)], wsem.at[slot])

        for j in range(SC_NBUF - 1):
            fetch(j, j).start()

        @pl.loop(0, steps, step=SC_NBUF)
        def _(s0):
            for j in range(SC_NBUF):
                s = s0 + j
                prev = (j - 1) % SC_NBUF
                fetch(s, j).wait()
                flush(s, j).start()

                @pl.when(s > 0)
                def _():
                    flush(s - 1, prev).wait()

                @pl.when(s + SC_NBUF - 1 < steps)
                def _():
                    fetch(s + SC_NBUF - 1, prev).start()

        flush(steps - 1, (steps - 1) % SC_NBUF).wait()

    return gather(table, idx)


def _unpack(w):
    lo = lax.bitcast_convert_type(w << 16, F32)
    hi = lax.bitcast_convert_type(w & jnp.uint32(0xFFFF0000), F32)
    return lo, hi


def _mix_kernel(alpha, h_ref, gate_ref, ug_ref, vg_ref, g_ref, b_ref, o_ref):
    TT, HK = MIX_TT, PEER_HEADS * PEER_TOPK
    x = h_ref[...]
    half = x.shape[1] // 2
    eye = lax.broadcasted_iota(jnp.int32, (HK, HK), 0) == lax.broadcasted_iota(jnp.int32, (HK, HK), 1)
    rows = []
    for t in range(TT):
        ulo, uhi = _unpack(ug_ref[t * HK:(t + 1) * HK, :])
        xt = x[t:t + 1, :]
        hcol = jnp.sum(ulo * xt[:, :half] + uhi * xt[:, half:], axis=1, keepdims=True)
        grow = jnp.broadcast_to(gate_ref[t:t + 1, :], (HK, HK))
        gcol = jnp.sum(jnp.where(eye, grow, 0.0), axis=1, keepdims=True)
        wcol = gcol * jax.nn.gelu(hcol)
        vlo, vhi = _unpack(vg_ref[t * HK:(t + 1) * HK, :])
        rows.append(jnp.concatenate([jnp.sum(wcol * vlo, axis=0, keepdims=True),
                                     jnp.sum(wcol * vhi, axis=0, keepdims=True)], axis=1))
    ffn = jnp.concatenate(rows, axis=0)
    o_ref[...] = _layernorm(alpha * x + ffn, g_ref[...], b_ref[...])


def _mix(alpha, h1, gates, ug, vg, g, b):
    n, D = h1.shape
    tt = MIX_TT
    hk = PEER_HEADS * PEER_TOPK
    const = lambda shp: pl.BlockSpec(shp, lambda i: (0, 0))
    return pl.pallas_call(
        functools.partial(_mix_kernel, alpha),
        grid=(n // tt,),
        in_specs=[pl.BlockSpec((tt, D), lambda i: (i, 0)), pl.BlockSpec((tt, hk), lambda i: (i, 0)),
                  pl.BlockSpec((tt * hk, D // 2), lambda i: (i, 0)), pl.BlockSpec((tt * hk, D // 2), lambda i: (i, 0)),
                  const((1, D)), const((1, D))],
        out_specs=pl.BlockSpec((tt, D), lambda i: (i, 0)),
        out_shape=jax.ShapeDtypeStruct((n, D), F32),
        compiler_params=_cparams(("parallel",)),
        name="peer_mix_ln",
    )(h1, gates, ug, vg, g.reshape(1, D).astype(F32), b.reshape(1, D).astype(F32))


PEER_CHUNK = 4096


def _peer_ffn_ln(alpha, h1, wq, k1, k2, up, vp, g, b):
    n, D = h1.shape
    hk = PEER_HEADS * PEER_TOPK
    eT, gT = _route(h1, wq, k1, k2)
    experts = eT.T
    gates = gT.T
    ch = min(PEER_CHUNK, n)
    outs = []
    for c in range(n // ch):
        idx = experts[c * ch:(c + 1) * ch].reshape(ch * hk)
        ug = _sc_gather(up, idx)
        vg = _sc_gather(vp, idx)
        outs.append(_mix(alpha, h1[c * ch:(c + 1) * ch], gates[c * ch:(c + 1) * ch], ug, vg, g, b))
    return jnp.concatenate(outs, axis=0)


def kernel(x, w_in, w_out, hg_lb, hg_norm_g, cmpk_pe, cmpk_w1, cmpk_b1, cmpk_w2, cmpk_b2,
           cmpv_pe, cmpv_w1, cmpv_b1, cmpv_w2, cmpv_b2, ln1_g, ln1_b,
           peer_wq, peer_k1, peer_k2, peer_u, peer_v, ln2_g, ln2_b):
    B, S, D = x.shape
    depth = w_in.shape[0]
    alpha = (2.0 * depth) ** 0.25
    h = x
    for l in range(depth):
        up, vp = _pack_rows(peer_u[l]), _pack_rows(peer_v[l])
        cmpk = (cmpk_pe[l], cmpk_w1[l], cmpk_b1[l], cmpk_w2[l], cmpk_b2[l])
        cmpv = (cmpv_pe[l], cmpv_w1[l], cmpv_b1[l], cmpv_w2[l], cmpv_b2[l])
        outs = []
        for bi in range(B):
            hb = h[bi:bi + 1]
            hg, qT, kn, ksE, vsT, vwT, gT = _project(hb, w_in[l])
            o_hg = _hgrn2(hg, hg_lb, hg_norm_g[l], l)
            cmp_n, cmp_T = _compress(kn, cmpk, cmpv)
            o_nsa = _nsa(qT, cmp_n, cmp_T, kn, ksE, vsT, vwT, gT)
            h1 = _outproj(alpha, o_hg.reshape(S, HG_WIDTH), o_nsa.reshape(S, NSA_WIDTH),
                          hb.reshape(S, D), w_out[l], ln1_g[l], ln1_b[l])
            outs.append(_peer_ffn_ln(alpha, h1, peer_wq[l], peer_k1[l], peer_k2[l], up, vp, ln2_g[l], ln2_b[l]))
        h = jnp.stack(outs, axis=0)
    return h
```

```python
import dataclasses
import functools
import math

import jax
import jax.numpy as jnp
import numpy as np
from jax import lax
from jax.experimental import pallas as pl
from jax.experimental.pallas import tpu as pltpu
from jax.experimental.pallas import tpu_sc as plsc

F32 = jnp.float32
BF16 = jnp.bfloat16

HG_HEADS = 4
HG_DK = 128
HG_DV = 128
HG_CHUNK = 64
HG_SUB = 16
HG_WIDTH = HG_HEADS * HG_DV
NSA_HEADS = 8
NSA_KV_HEADS = 2
NSA_GROUP = NSA_HEADS // NSA_KV_HEADS
NSA_HD = 64
NSA_WIDTH = NSA_HEADS * NSA_HD
CMP_LEN = 32
CMP_STRIDE = 16
CMP_HIDDEN = 256
SLC_LEN = 64
SLC_TOPK = 16
WIN = 512
FORCE_SCORE = 1.0e4
PEER_HEADS = 8
PEER_DQ = 256
PEER_NKEYS = 128
PEER_TOPK = 16
LN_EPS = 1e-5
NEG = -1e30
LOG2E = math.log2(math.e)

LANES = 128
PROJ_TM = 512
HG_TC = 256
ATT_TQ = 128
ATT_KT = 512
ATT_KS = 1024
OUT_TM = 512
ROUTE_TR = 256
MIX_TT = 16
SC_WORKERS = 32
SC_LANES = 16
SC_USLOT = 4
SC_VWIN = 16
SC_VSLOT = 4
SC_XB = 8
VMEM_LIMIT = 56 * 1024 * 1024


def _cparams(sem):
    return pltpu.CompilerParams(dimension_semantics=sem, vmem_limit_bytes=VMEM_LIMIT)


def _nt(a, b):
    return lax.dot_general(a, b, (((1,), (1,)), ((), ())), preferred_element_type=F32)


def _tn(a, b):
    return lax.dot_general(a, b, (((0,), (0,)), ((), ())), preferred_element_type=F32)


def _nn(a, b):
    return jnp.dot(a, b, preferred_element_type=F32)


def _proj_kernel(x_ref, whg_ref, wqT_ref, wkn_ref, wvT_ref, wgT_ref,
                 hg_ref, qT_ref, kn_ref, ksE_ref, vsT_ref, vwT_ref, gT_ref):
    xb = x_ref[0].astype(BF16)
    hg_ref[0] = _nn(xb, whg_ref[...])
    qT_ref[0] = (_nt(wqT_ref[...], xb) * (NSA_HD ** -0.5 * LOG2E)).astype(BF16)
    kn = _nn(xb, wkn_ref[...])
    for j in range(3):
        for g in range(NSA_KV_HEADS):
            lo = j * 2 * NSA_HD + g * NSA_HD
            kn_ref[0, j, g] = kn[:, lo:lo + NSA_HD].astype(BF16)
    row = pl.program_id(1) * kn.shape[0] + lax.broadcasted_iota(jnp.int32, (kn.shape[0], NSA_HD), 0)
    blk = (row >> int(math.log2(SLC_LEN))) & (ATT_KS // SLC_LEN - 1)
    onehot = (lax.broadcasted_iota(jnp.int32, (kn.shape[0], NSA_HD), 1) == blk).astype(F32)
    for g in range(NSA_KV_HEADS):
        lo = 3 * 2 * NSA_HD + g * NSA_HD
        ksE_ref[0, g] = jnp.concatenate([kn[:, lo:lo + NSA_HD], onehot], axis=1).astype(BF16)
    vT = _nt(wvT_ref[...], xb).astype(BF16)
    tm = vT.shape[1]
    for g in range(NSA_KV_HEADS):
        for c in range(tm // ATT_KT):
            vsT_ref[0, g, c] = vT[g * NSA_HD:(g + 1) * NSA_HD, c * ATT_KT:(c + 1) * ATT_KT]
        for c in range(tm // LANES):
            vwT_ref[0, g, c] = vT[(2 + g) * NSA_HD:(3 + g) * NSA_HD, c * LANES:(c + 1) * LANES]
    gT = _nt(wgT_ref[...], xb)
    gT_ref[0, 0] = gT[0:16]
    gT_ref[0, 1] = gT[16:32]


def _project(x, w_in_l):
    B, S, D = x.shape
    tm = PROJ_TM
    o = np.cumsum((0, 512, 512, 512, 512, 512, 128, 128, 128, 128, 128, 128, 24))
    wb = w_in_l.astype(BF16)
    whg = wb[:, o[0]:o[4]]
    wqT = wb[:, o[4]:o[5]].T
    wkn = jnp.concatenate([wb[:, o[5]:o[6]], wb[:, o[6]:o[7]], wb[:, o[9]:o[10]], wb[:, o[7]:o[8]]], axis=1)
    wvT = jnp.concatenate([wb[:, o[8]:o[9]], wb[:, o[10]:o[11]]], axis=1).T
    wg = wb[:, o[11]:o[12]].reshape(D, NSA_KV_HEADS, NSA_GROUP * 3)
    wgT = jnp.pad(wg, ((0, 0), (0, 0), (0, 16 - NSA_GROUP * 3))).reshape(D, 32).T
    const = lambda shp: pl.BlockSpec(shp, lambda b, i: (0,) * len(shp))
    out_shape = (
        jax.ShapeDtypeStruct((B, S, 4 * 512), F32),
        jax.ShapeDtypeStruct((B, NSA_WIDTH, S), BF16),
        jax.ShapeDtypeStruct((B, 3, NSA_KV_HEADS, S, NSA_HD), BF16),
        jax.ShapeDtypeStruct((B, NSA_KV_HEADS, S, 2 * NSA_HD), BF16),
        jax.ShapeDtypeStruct((B, NSA_KV_HEADS, S // ATT_KT, NSA_HD, ATT_KT), BF16),
        jax.ShapeDtypeStruct((B, NSA_KV_HEADS, S // LANES, NSA_HD, LANES), BF16),
        jax.ShapeDtypeStruct((B, NSA_KV_HEADS, 16, S), F32),
    )
    out_specs = (
        pl.BlockSpec((1, tm, 2048), lambda b, i: (b, i, 0)),
        pl.BlockSpec((1, NSA_WIDTH, tm), lambda b, i: (b, 0, i)),
        pl.BlockSpec((1, 3, NSA_KV_HEADS, tm, NSA_HD), lambda b, i: (b, 0, 0, i, 0)),
        pl.BlockSpec((1, NSA_KV_HEADS, tm, 2 * NSA_HD), lambda b, i: (b, 0, i, 0)),
        pl.BlockSpec((1, NSA_KV_HEADS, tm // ATT_KT, NSA_HD, ATT_KT), lambda b, i: (b, 0, i, 0, 0)),
        pl.BlockSpec((1, NSA_KV_HEADS, tm // LANES, NSA_HD, LANES), lambda b, i: (b, 0, i, 0, 0)),
        pl.BlockSpec((1, NSA_KV_HEADS, 16, tm), lambda b, i: (b, 0, 0, i)),
    )
    return pl.pallas_call(
        _proj_kernel,
        grid=(B, S // tm),
        in_specs=[pl.BlockSpec((1, tm, D), lambda b, i: (b, i, 0)),
                  const(whg.shape), const(wqT.shape), const(wkn.shape), const(wvT.shape), const(wgT.shape)],
        out_specs=out_specs,
        out_shape=out_shape,
        compiler_params=_cparams(("parallel", "arbitrary")),
        name="in_proj",
    )(x, whg, wqT, wkn, wvT, wgT)


def _split3(a):
    hi = a.astype(BF16)
    r = a - hi.astype(F32)
    mid = r.astype(BF16)
    lo = (r - mid.astype(F32)).astype(BF16)
    return hi, mid, lo


def _hgrn_kernel(layer, hg_ref, lb_ref, ng_ref, o_ref, st_ref):
    C, SB, H, DK = HG_CHUNK, HG_SUB, HG_HEADS, HG_DK
    nsb = C // SB

    @pl.when(pl.program_id(1) == 0)
    def _():
        st_ref[...] = jnp.zeros_like(st_ref)

    lbp = lb_ref[...]
    e = jnp.exp(lbp - jnp.max(lbp, axis=0, keepdims=True))
    lb = jnp.sum(e[:layer + 1], axis=0, keepdims=True) / jnp.sum(e, axis=0, keepdims=True)

    ri = lax.broadcasted_iota(jnp.int32, (C, C), 0)
    ci = lax.broadcasted_iota(jnp.int32, (C, C), 1)
    tril = (ci <= ri).astype(BF16)
    t_sub = lax.broadcasted_iota(jnp.int32, (SB, H * DK), 0)
    lane16 = lax.broadcasted_iota(jnp.int32, (SB, SB), 1)
    ng = ng_ref[...]

    def chunk(c, carry):
        r0 = pl.multiple_of(c * C, C)
        blk = hg_ref[0, pl.ds(r0, C), :]
        q = blk[:, 0:512]
        f = lb + (1.0 - lb) * jax.nn.sigmoid(blk[:, 512:1024])
        lf = jnp.log(f)
        k = 1.0 - f
        v = blk[:, 1024:1536]
        gate = blk[:, 1536:2048]
        hi, mid, lo = _split3(lf)
        b = _nn(tril, hi) + _nn(tril, mid) + _nn(tril, lo)
        b_last = b[C - 1:C, :]
        qe = (q * jnp.exp(b)).astype(BF16)
        ks = (k * jnp.exp(b_last - b)).astype(BF16)
        dec = jnp.exp(b_last)
        vb = v.astype(BF16)
        rblk = jnp.concatenate([jnp.broadcast_to(b[i * SB:i * SB + 1, :], (SB, H * DK)) for i in range(nsb)], axis=0)
        qn = (q * jnp.exp(b - rblk)).astype(BF16)
        outs = []
        for h in range(H):
            sl = slice(h * DK, (h + 1) * DK)
            stT = st_ref[h]
            o_h = _nt(qe[:, sl], stT.astype(BF16))
            rows = []
            for i in range(nsb):
                rs = slice(i * SB, (i + 1) * SB)
                acc = jnp.zeros((SB, HG_DV), F32)
                if i > 0:
                    ke = (k[0:i * SB, sl] * jnp.exp(b[i * SB:i * SB + 1, sl] - b[0:i * SB, sl])).astype(BF16)
                    a_off = _nt(qn[rs, sl], ke)
                    acc = acc + _nn(a_off.astype(BF16), vb[0:i * SB, sl])
                qi, ki, bi = q[rs, sl], k[rs, sl], b[rs, sl]
                a_d = jnp.zeros((SB, SB), F32)
                for s in range(SB):
                    w = qi * ki[s:s + 1, :] * jnp.exp(jnp.minimum(bi - bi[s:s + 1, :], 0.0))
                    col = jnp.sum(w, axis=1, keepdims=True)
                    a_d = a_d + jnp.where((lane16 == s) & (t_sub[:, 0:SB] >= s), col, 0.0)
                acc = acc + _nn(a_d.astype(BF16), vb[rs, sl])
                rows.append(acc)
            o_h = o_h + jnp.concatenate(rows, axis=0)
            st_ref[h] = dec[:, sl] * stT + _tn(vb[:, sl], ks[:, sl])
            o_h = o_h * lax.rsqrt(jnp.mean(o_h * o_h, axis=1, keepdims=True) + LN_EPS) * ng
            outs.append(o_h)
        o = jnp.concatenate(outs, axis=1) * jax.nn.silu(gate)
        o_ref[0, pl.ds(r0, C), :] = o.astype(BF16)
        return carry

    lax.fori_loop(0, hg_ref.shape[1] // C, chunk, 0)


def _hgrn2(hg, hg_lb, norm_g_l, layer):
    B, S, _ = hg.shape
    tc = HG_TC
    return pl.pallas_call(
        functools.partial(_hgrn_kernel, layer),
        grid=(B, S // tc),
        in_specs=[pl.BlockSpec((1, tc, 2048), lambda b, i: (b, i, 0)),
                  pl.BlockSpec(hg_lb.shape, lambda b, i: (0, 0)),
                  pl.BlockSpec((1, HG_DV), lambda b, i: (0, 0))],
        out_specs=pl.BlockSpec((1, tc, HG_WIDTH), lambda b, i: (b, i, 0)),
        out_shape=jax.ShapeDtypeStruct((B, S, HG_WIDTH), BF16),
        scratch_shapes=[pltpu.VMEM((HG_HEADS, HG_DV, HG_DK), F32)],
        compiler_params=_cparams(("parallel", "arbitrary")),
        name="hgrn2",
    )(hg, hg_lb.astype(F32), norm_g_l.reshape(1, HG_DV).astype(F32))


def _compress_kernel(c_ref, pe_ref, w1_ref, b1_ref, w2_ref, w2T_ref, b2_ref, b2T_ref, o_ref, oT_ref):
    half = CMP_STRIDE * NSA_HD
    c = c_ref[0, 0, 0].astype(F32)
    pe = pe_ref[0]
    ca = (c + pe[:, 0:half]).astype(BF16)
    cb = (c + pe[:, half:2 * half]).astype(BF16)
    pa = _nn(ca, w1_ref[0, 0:half, :])
    pb = _nn(cb, w1_ref[0, half:2 * half, :])
    nbp = pa.shape[0]
    h = pa + pltpu.roll(pb, nbp - 1, 0) + b1_ref[0]
    h = jax.nn.gelu(h).astype(BF16)
    o_ref[0, 0, 0] = (_nn(h, w2_ref[0]) + b2_ref[0]).astype(BF16)
    oT_ref[0, 0, 0] = (_nt(w2T_ref[0], h) + b2T_ref[0]).astype(BF16)


def _compress(kn, cmpk, cmpv):
    B, _, G, S, HD = kn.shape
    nbp = S // CMP_STRIDE
    c = kn[:, 0:2].reshape(B, 2, G, nbp, CMP_STRIDE * HD)
    stack = lambda a, b, f: jnp.stack([f(a), f(b)], axis=0)
    pe = stack(cmpk[0], cmpv[0], lambda t: t.reshape(1, CMP_LEN * HD).astype(F32))
    w1 = stack(cmpk[1], cmpv[1], lambda t: t.astype(BF16))
    b1 = stack(cmpk[2], cmpv[2], lambda t: t.reshape(1, CMP_HIDDEN).astype(F32))
    w2 = stack(cmpk[3], cmpv[3], lambda t: t.astype(BF16))
    w2T = stack(cmpk[3], cmpv[3], lambda t: t.astype(BF16).T)
    b2 = stack(cmpk[4], cmpv[4], lambda t: t.reshape(1, HD).astype(F32))
    b2T = stack(cmpk[4], cmpv[4], lambda t: t.reshape(HD, 1).astype(F32))
    wspec = lambda a: pl.BlockSpec((1,) + a.shape[1:], lambda b, t, g: (t,) + (0,) * (a.ndim - 1))
    return pl.pallas_call(
        _compress_kernel,
        grid=(B, 2, G),
        in_specs=[pl.BlockSpec((1, 1, 1, nbp, CMP_STRIDE * HD), lambda b, t, g: (b, t, g, 0, 0)),
                  wspec(pe), wspec(w1), wspec(b1), wspec(w2), wspec(w2T), wspec(b2), wspec(b2T)],
        out_specs=(pl.BlockSpec((1, 1, 1, nbp, HD), lambda b, t, g: (b, t, g, 0, 0)),
                   pl.BlockSpec((1, 1, 1, HD, nbp), lambda b, t, g: (b, t, g, 0, 0))),
        out_shape=(jax.ShapeDtypeStruct((B, 2, G, nbp, HD), BF16),
                   jax.ShapeDtypeStruct((B, 2, G, HD, nbp), BF16)),
        compiler_params=_cparams(("parallel", "parallel", "parallel")),
        name="kv_compress",
    )(c, pe, w1, b1, w2, w2T, b2, b2T)


def _topk_mask_rows(score, k):
    R = score.shape[0]
    r_io = lax.broadcasted_iota(jnp.int32, score.shape, 0)
    s = score
    for _ in range(k):
        m = jnp.max(s, axis=0, keepdims=True)
        idx = jnp.min(jnp.where(s == m, r_io, R), axis=0, keepdims=True)
        s = jnp.where(r_io == idx, -jnp.inf, s)
    return (s == -jnp.inf).astype(F32)


def _with_ones_row(vT):
    pad = (lax.broadcasted_iota(jnp.int32, (16, vT.shape[1]), 0) == 0).astype(BF16)
    return jnp.concatenate([vT, pad], axis=0)


def _nsa_kernel(qT_ref, kc_ref, vcT_ref, ks_ref, vsT_ref, kw_ref, vwT_ref, gT_ref, ovT_ref,
                o_ref, sel_ref):
    TQ, G, HD, KT = ATT_TQ, NSA_GROUP, NSA_HD, ATT_KT
    qi = pl.program_id(2)
    t0 = qi * TQ
    qT = qT_ref[0]
    q4 = jnp.concatenate([qT[r * HD:(r + 1) * HD, :] for r in range(G)], axis=1)
    pos1 = t0 + lax.broadcasted_iota(jnp.int32, (1, TQ), 1)
    pos4 = jnp.concatenate([pos1] * G, axis=1)

    nbp = kc_ref.shape[3]
    s_c = _nn(kc_ref[0, 0, 0], q4)
    n_io = lax.broadcasted_iota(jnp.int32, (nbp, 1), 0)
    mask_c = (n_io * CMP_STRIDE + (CMP_LEN - 1)) <= pos4
    s_c = jnp.where(mask_c, s_c, NEG)
    p_c = jnp.where(mask_c, jnp.exp2(s_c - jnp.max(s_c, axis=0, keepdims=True)), 0.0)
    p_c = p_c / jnp.maximum(jnp.sum(p_c, axis=0, keepdims=True), 1e-30)
    o_c = _nn(vcT_ref[0, 0, 0], p_c.astype(BF16))
    p_sum = p_c[:, 0:TQ]
    for r in range(1, G):
        p_sum = p_sum + p_c[:, r * TQ:(r + 1) * TQ]
    imp = _nn(ovT_ref[...], p_sum.astype(BF16))

    ns = imp.shape[0]
    j_io = lax.broadcasted_iota(jnp.int32, (ns, 1), 0)
    cur = pos1 >> int(math.log2(SLC_LEN))
    forced = (j_io == 0) | (j_io == cur) | (j_io == cur - 1)
    score = jnp.where(forced, FORCE_SCORE, jnp.where(j_io * SLC_LEN <= pos1, imp, -1.0))
    sel_ref[...] = (_topk_mask_rows(score, min(SLC_TOPK, ns)) - 1.0) * (-NEG)

    KS = ATT_KS
    bps = KS // SLC_LEN
    k_io = lax.broadcasted_iota(jnp.int32, (KS, 1), 0)
    zrows = jnp.zeros((HD - bps, G * TQ), BF16)

    def sel_step(ks_i, carry, diagonal):
        m, acc = carry
        k0 = pl.multiple_of(ks_i * KS, KS)
        bias = sel_ref[pl.ds(pl.multiple_of(ks_i * bps, bps), bps), :].astype(BF16)
        rhs = jnp.concatenate([q4, jnp.concatenate([bias] * G, axis=1), zrows], axis=0)
        s = _nn(ks_ref[0, 0, pl.ds(k0, KS), :], rhs)
        if diagonal:
            s = jnp.where((k0 + k_io) <= pos4, s, NEG)
        m_new = jnp.maximum(m, jnp.max(s, axis=0, keepdims=True))
        p = jnp.exp2(s - m_new).astype(BF16)
        acc = jnp.exp2(m - m_new) * acc
        for c in range(KS // KT):
            acc = acc + _nn(_with_ones_row(vsT_ref[0, 0, ks_i * (KS // KT) + c]), p[c * KT:(c + 1) * KT])
        return m_new, acc

    n_ks = (t0 + TQ - 1) // KS + 1
    init = (jnp.full((1, G * TQ), NEG, F32), jnp.zeros((HD + 16, G * TQ), F32))
    carry = lax.fori_loop(0, n_ks - 1, lambda i, c: sel_step(i, c, False), init)
    _, acc_s = sel_step(n_ks - 1, carry, True)
    o_s = acc_s[0:HD] / jnp.maximum(acc_s[HD:HD + 1], 1e-30)

    nwt = WIN // TQ + 1
    kw_t, vw_t, kp_t = [], [], []
    r_io = lax.broadcasted_iota(jnp.int32, (TQ, 1), 0)
    for i in range(nwt):
        idx = qi - (nwt - 1) + i
        idc = jnp.maximum(idx, 0)
        kw_t.append(kw_ref[0, 0, 0, pl.ds(pl.multiple_of(idc * TQ, TQ), TQ), :])
        vw_t.append(vwT_ref[0, 0, idc])
        kp_t.append(idx * TQ + r_io)
    kpos = jnp.concatenate(kp_t, axis=0)
    s_w = _nn(jnp.concatenate(kw_t, axis=0), q4)
    delta = pos4 - kpos
    s_w = jnp.where((delta >= 0) & (delta < WIN) & (kpos >= 0), s_w, NEG)
    p_w = jnp.exp2(s_w - jnp.max(s_w, axis=0, keepdims=True)).astype(BF16)
    acc_w = _nn(_with_ones_row(jnp.concatenate(vw_t, axis=1)), p_w)
    o_w = acc_w[0:HD] / jnp.maximum(acc_w[HD:HD + 1], 1e-30)

    gs = jax.nn.sigmoid(gT_ref[0, 0])
    gate = lambda br: jnp.concatenate([gs[r * 3 + br:r * 3 + br + 1, :] for r in range(G)], axis=1)
    o = gate(0) * o_c + gate(1) * o_s + gate(2) * o_w
    o_ref[0] = jnp.concatenate([o[:, r * TQ:(r + 1) * TQ].T for r in range(G)], axis=1).astype(BF16)


def _overlap_T(S):
    nbp = S // CMP_STRIDE
    n_slc = S // SLC_LEN
    cs = np.arange(nbp) * CMP_STRIDE
    ss = np.arange(n_slc) * SLC_LEN
    ov = np.clip(np.minimum(cs[:, None] + CMP_LEN, ss[None, :] + SLC_LEN)
                 - np.maximum(cs[:, None], ss[None, :]), 0, None) / CMP_LEN
    ov[nbp - 1:, :] = 0.0
    return jnp.asarray(ov.T, BF16)


def _nsa(qT, cmp_n, cmp_T, kn, ksE, vsT, vwT, gT):
    B, _, S = qT.shape
    G2, TQ, HD = NSA_KV_HEADS, ATT_TQ, NSA_HD
    nbp = S // CMP_STRIDE
    ns = S // SLC_LEN
    ovT = _overlap_T(S)
    return pl.pallas_call(
        _nsa_kernel,
        grid=(B, G2, S // TQ),
        in_specs=[
            pl.BlockSpec((1, NSA_GROUP * HD, TQ), lambda b, g, i: (b, g, i)),
            pl.BlockSpec((1, 1, 1, nbp, HD), lambda b, g, i: (b, 0, g, 0, 0)),
            pl.BlockSpec((1, 1, 1, HD, nbp), lambda b, g, i: (b, 1, g, 0, 0)),
            pl.BlockSpec((1, 1, S, 2 * HD), lambda b, g, i: (b, g, 0, 0)),
            pl.BlockSpec((1, 1, S // ATT_KT, HD, ATT_KT), lambda b, g, i: (b, g, 0, 0, 0)),
            pl.BlockSpec((1, 1, 1, S, HD), lambda b, g, i: (b, 2, g, 0, 0)),
            pl.BlockSpec((1, 1, S // LANES, HD, LANES), lambda b, g, i: (b, g, 0, 0, 0)),
            pl.BlockSpec((1, 1, 16, TQ), lambda b, g, i: (b, g, 0, i)),
            pl.BlockSpec((ns, nbp), lambda b, g, i: (0, 0)),
        ],
        out_specs=pl.BlockSpec((1, TQ, NSA_GROUP * HD), lambda b, g, i: (b, i, g)),
        out_shape=jax.ShapeDtypeStruct((B, S, NSA_WIDTH), BF16),
        scratch_shapes=[pltpu.VMEM((ns, TQ), F32)],
        compiler_params=_cparams(("parallel", "parallel", "arbitrary")),
        name="nsa_attn",
    )(qT, cmp_n, cmp_T, ksE, vsT, kn, vwT, gT, ovT)


def _layernorm(t, g, b):
    mu = jnp.mean(t, axis=-1, keepdims=True)
    d = t - mu
    var = jnp.mean(d * d, axis=-1, keepdims=True)
    return d * lax.rsqrt(var + LN_EPS) * g + b


def _outproj_kernel(alpha, ohg_ref, onsa_ref, x_ref, w_ref, g_ref, b_ref, h_ref):
    mix = _nn(ohg_ref[...], w_ref[0:HG_WIDTH, :]) + _nn(onsa_ref[...], w_ref[HG_WIDTH:HG_WIDTH + NSA_WIDTH, :])
    h_ref[...] = _layernorm(alpha * x_ref[...] + mix, g_ref[...], b_ref[...])


def _outproj(alpha, o_hg, o_nsa, x2, w_out_l, g, b):
    n, D = x2.shape
    tm = OUT_TM
    row = lambda w: pl.BlockSpec((tm, w), lambda i: (i, 0))
    const = lambda shp: pl.BlockSpec(shp, lambda i: (0, 0))
    return pl.pallas_call(
        functools.partial(_outproj_kernel, alpha),
        grid=(n // tm,),
        in_specs=[row(HG_WIDTH), row(NSA_WIDTH), row(D), const(w_out_l.shape), const((1, D)), const((1, D))],
        out_specs=row(D),
        out_shape=jax.ShapeDtypeStruct((n, D), F32),
        compiler_params=_cparams(("parallel",)),
        name="out_proj_ln",
    )(o_hg, o_nsa, x2, w_out_l.astype(BF16), g.reshape(1, D).astype(F32), b.reshape(1, D).astype(F32))


def _topk_rows(s, k):
    R = s.shape[0]
    r_io = lax.broadcasted_iota(jnp.int32, s.shape, 0)
    vals, idxs = [], []
    for _ in range(k):
        m = jnp.max(s, axis=0, keepdims=True)
        idx = jnp.min(jnp.where(s == m, r_io, R), axis=0, keepdims=True)
        vals.append(m)
        idxs.append(idx)
        s = jnp.where(r_io == idx, -jnp.inf, s)
    return jnp.concatenate(vals, axis=0), jnp.concatenate(idxs, axis=0)


def _route_kernel(h_ref, wq_ref, k1_ref, k2_ref, e_ref, g_ref):
    K, half = PEER_TOPK, PEER_DQ // 2
    q = _nn(h_ref[...].astype(BF16), wq_ref[...]).astype(BF16)
    for h in range(PEER_HEADS):
        s1 = _nt(k1_ref[...], q[:, h * PEER_DQ:h * PEER_DQ + half])
        s2 = _nt(k2_ref[...], q[:, h * PEER_DQ + half:(h + 1) * PEER_DQ])
        v1, i1 = _topk_rows(s1, K)
        v2, i2 = _topk_rows(s2, K)
        cand = jnp.concatenate([v1[a:a + 1, :] + v2 for a in range(K)], axis=0)
        top_s, top_c = _topk_rows(cand, K)
        ca, cb = top_c >> int(math.log2(K)), top_c & (K - 1)
        e1 = jnp.zeros_like(top_c)
        e2 = jnp.zeros_like(top_c)
        for a in range(K):
            e1 = e1 + jnp.where(ca == a, i1[a:a + 1, :], 0)
            e2 = e2 + jnp.where(cb == a, i2[a:a + 1, :], 0)
        e_ref[h * K:(h + 1) * K, :] = e1 * PEER_NKEYS + e2
        ex = jnp.exp(top_s - top_s[0:1, :])
        g_ref[h * K:(h + 1) * K, :] = ex / jnp.sum(ex, axis=0, keepdims=True)


def _route(h1, wq, k1, k2):
    n, D = h1.shape
    tr = ROUTE_TR
    hk = PEER_HEADS * PEER_TOPK
    const = lambda shp: pl.BlockSpec(shp, lambda i: (0, 0))
    return pl.pallas_call(
        _route_kernel,
        grid=(n // tr,),
        in_specs=[pl.BlockSpec((tr, D), lambda i: (i, 0)), const(wq.shape), const(k1.shape), const(k2.shape)],
        out_specs=(pl.BlockSpec((hk, tr), lambda i: (0, i)), pl.BlockSpec((hk, tr), lambda i: (0, i))),
        out_shape=(jax.ShapeDtypeStruct((hk, n), jnp.int32), jax.ShapeDtypeStruct((hk, n), F32)),
        compiler_params=_cparams(("parallel",)),
        name="peer_route",
    )(h1, wq.astype(BF16), k1.astype(BF16), k2.astype(BF16))


def _pack_rows(t):
    half = t.shape[1] // 2
    tb = t.astype(BF16)
    lo = lax.bitcast_convert_type(tb[:, :half], jnp.uint16).astype(jnp.uint32)
    hi = lax.bitcast_convert_type(tb[:, half:], jnp.uint16).astype(jnp.uint32)
    return lo | (hi << 16)


def _sc_peer(utab, vtab, idx, x):
    n = x.shape[0]
    w = utab.shape[1]
    hk = PEER_HEADS * PEER_TOPK
    L = SC_LANES
    T = n // SC_WORKERS
    uwin = hk // SC_USLOT
    vpt = hk // SC_VWIN
    per_u = vpt // SC_USLOT
    assert n % SC_WORKERS == 0 and T % SC_XB == 0 and uwin % L == 0 and vpt % SC_VSLOT == 0 and vpt % SC_USLOT == 0
    vsteps = T * vpt
    mesh = plsc.VectorSubcoreMesh(core_axis_name="c", subcore_axis_name="s")
    cp = dataclasses.replace(pltpu.CompilerParams(), needs_layout_passes=False)

    @pl.kernel(out_type=(jax.ShapeDtypeStruct((n * hk,), F32), jax.ShapeDtypeStruct((n * hk, w), utab.dtype)),
               mesh=mesh, compiler_params=cp,
               scratch_types=[pltpu.VMEM((T * hk,), jnp.int32),
                              pltpu.VMEM((SC_XB, 2 * w), F32),
                              pltpu.VMEM((SC_USLOT, uwin, w), utab.dtype),
                              pltpu.VMEM((SC_VSLOT, SC_VWIN, w), vtab.dtype),
                              pltpu.VMEM((L, L), F32),
                              pltpu.VMEM((SC_XB * hk,), F32),
                              pltpu.SemaphoreType.DMA((SC_USLOT,)),
                              pltpu.SemaphoreType.DMA((SC_VSLOT,)),
                              pltpu.SemaphoreType.DMA((SC_VSLOT,))])
    def peer(u_hbm, v_hbm, i_hbm, x_hbm, h_hbm, vg_hbm, idx_v, x_v, ubuf, vbuf, acc_s, h_v, usem, vsem, wsem):
        wid = lax.axis_index("c") * (SC_WORKERS // 2) + lax.axis_index("s")
        tok0 = wid * T
        row0 = tok0 * hk
        pltpu.sync_copy(i_hbm.at[pl.ds(row0, T * hk)], idx_v)

        def ufetch(t, j):
            return pltpu.make_async_copy(u_hbm.at[idx_v.at[pl.ds(t * hk + j * uwin, uwin)]], ubuf.at[j], usem.at[j])

        def vfetch(s, slot):
            return pltpu.make_async_copy(v_hbm.at[idx_v.at[pl.ds(s * SC_VWIN, SC_VWIN)]], vbuf.at[slot], vsem.at[slot])

        def vflush(s, slot):
            return pltpu.make_async_copy(vbuf.at[slot], vg_hbm.at[pl.ds(row0 + s * SC_VWIN, SC_VWIN)], wsem.at[slot])

        for j in range(SC_USLOT):
            ufetch(0, j).start()
        for j in range(SC_VSLOT - 1):
            vfetch(j, j).start()

        def vstep(s, slot):
            prev = (slot - 1) % SC_VSLOT
            vfetch(s, slot).wait()
            vflush(s, slot).start()

            @pl.when(s > 0)
            def _():
                vflush(s - 1, prev).wait()

            @pl.when(s + SC_VSLOT - 1 < vsteps)
            def _():
                vfetch(s + SC_VSLOT - 1, prev).start()

        lane = lax.iota(jnp.int32, L)

        @pl.loop(0, T // SC_XB)
        def _(tb):
            pltpu.sync_copy(x_hbm.at[pl.ds(pl.multiple_of(tok0 + tb * SC_XB, SC_XB), SC_XB)], x_v)

            @pl.loop(0, SC_XB)
            def _(tt):
                t = tb * SC_XB + tt
                for j in range(SC_USLOT):
                    for i in range(per_u):
                        vstep(t * vpt + j * per_u + i, (j * per_u + i) % SC_VSLOT)
                    ufetch(t, j).wait()

                    def dot_body(jj, accs):
                        off = pl.multiple_of(jj * L, L)
                        xlo = x_v[tt, pl.ds(off, L)]
                        xhi = x_v[tt, pl.ds(w + off, L)]
                        out = []
                        for r in range(uwin):
                            wv = ubuf[j, r, pl.ds(off, L)]
                            lo = plsc.bitcast(wv << 16, F32)
                            hi = plsc.bitcast(wv & jnp.uint32(0xFFFF0000), F32)
                            out.append(accs[r] + lo * xlo + hi * xhi)
                        return tuple(out)

                    accs = lax.fori_loop(0, w // L, dot_body, tuple(jnp.zeros((L,), F32) for _ in range(uwin)))

                    @pl.when(t + 1 < T)
                    def _():
                        ufetch(t + 1, j).start()

                    for g in range(uwin // L):
                        for r in range(L):
                            acc_s[r, :] = accs[g * L + r]
                        tot = jnp.zeros((L,), F32)
                        for c in range(L):
                            tot = tot + plsc.load_gather(acc_s, [lane, jnp.full((L,), c, jnp.int32)])
                        h_v[pl.ds(tt * hk + j * uwin + g * L, L)] = tot

            pltpu.sync_copy(h_v, h_hbm.at[pl.ds(pl.multiple_of(row0 + tb * SC_XB * hk, SC_XB * hk), SC_XB * hk)])

        vflush(vsteps - 1, (vsteps - 1) % SC_VSLOT).wait()

    return peer(utab, vtab, idx, x)


def _unpack(w):
    lo = lax.bitcast_convert_type(w << 16, F32)
    hi = lax.bitcast_convert_type(w & jnp.uint32(0xFFFF0000), F32)
    return lo, hi


def _mix_kernel(alpha, h_ref, s_ref, gate_ref, vg_ref, g_ref, b_ref, o_ref):
    TT, HK = MIX_TT, PEER_HEADS * PEER_TOPK
    x = h_ref[...]
    wrow = gate_ref[...] * jax.nn.gelu(s_ref[...])
    eye = lax.broadcasted_iota(jnp.int32, (HK, HK), 0) == lax.broadcasted_iota(jnp.int32, (HK, HK), 1)
    rows = []
    for t in range(TT):
        wcol = jnp.sum(jnp.where(eye, jnp.broadcast_to(wrow[t:t + 1, :], (HK, HK)), 0.0),
                       axis=1, keepdims=True)
        vlo, vhi = _unpack(vg_ref[t * HK:(t + 1) * HK, :])
        rows.append(jnp.concatenate([jnp.sum(wcol * vlo, axis=0, keepdims=True),
                                     jnp.sum(wcol * vhi, axis=0, keepdims=True)], axis=1))
    ffn = jnp.concatenate(rows, axis=0)
    o_ref[...] = _layernorm(alpha * x + ffn, g_ref[...], b_ref[...])


def _mix(alpha, h1, scores, gates, vg, g, b):
    n, D = h1.shape
    tt = MIX_TT
    hk = PEER_HEADS * PEER_TOPK
    const = lambda shp: pl.BlockSpec(shp, lambda i: (0, 0))
    return pl.pallas_call(
        functools.partial(_mix_kernel, alpha),
        grid=(n // tt,),
        in_specs=[pl.BlockSpec((tt, D), lambda i: (i, 0)), pl.BlockSpec((tt, hk), lambda i: (i, 0)),
                  pl.BlockSpec((tt, hk), lambda i: (i, 0)), pl.BlockSpec((tt * hk, D // 2), lambda i: (i, 0)),
                  const((1, D)), const((1, D))],
        out_specs=pl.BlockSpec((tt, D), lambda i: (i, 0)),
        out_shape=jax.ShapeDtypeStruct((n, D), F32),
        compiler_params=_cparams(("parallel",)),
        name="peer_mix_ln",
    )(h1, scores, gates, vg, g.reshape(1, D).astype(F32), b.reshape(1, D).astype(F32))


PEER_CHUNK = 4096


def _peer_ffn_ln(alpha, h1, wq, k1, k2, up, vp, g, b):
    n, D = h1.shape
    hk = PEER_HEADS * PEER_TOPK
    eT, gT = _route(h1, wq, k1, k2)
    experts = eT.T
    gates = gT.T
    ch = min(PEER_CHUNK, n)
    outs = []
    for c in range(n // ch):
        hc = h1[c * ch:(c + 1) * ch]
        scores, vg = _sc_peer(up, vp, experts[c * ch:(c + 1) * ch].reshape(ch * hk), hc)
        outs.append(_mix(alpha, hc, scores.reshape(ch, hk), gates[c * ch:(c + 1) * ch], vg, g, b))
    return jnp.concatenate(outs, axis=0)


def kernel(x, w_in, w_out, hg_lb, hg_norm_g, cmpk_pe, cmpk_w1, cmpk_b1, cmpk_w2, cmpk_b2,
           cmpv_pe, cmpv_w1, cmpv_b1, cmpv_w2, cmpv_b2, ln1_g, ln1_b,
           peer_wq, peer_k1, peer_k2, peer_u, peer_v, ln2_g, ln2_b):
    B, S, D = x.shape
    depth = w_in.shape[0]
    alpha = (2.0 * depth) ** 0.25
    h = x
    for l in range(depth):
        up, vp = _pack_rows(peer_u[l]), _pack_rows(peer_v[l])
        cmpk = (cmpk_pe[l], cmpk_w1[l], cmpk_b1[l], cmpk_w2[l], cmpk_b2[l])
        cmpv = (cmpv_pe[l], cmpv_w1[l], cmpv_b1[l], cmpv_w2[l], cmpv_b2[l])
        outs = []
        for bi in range(B):
            hb = h[bi:bi + 1]
            hg, qT, kn, ksE, vsT, vwT, gT = _project(hb, w_in[l])
            o_hg = _hgrn2(hg, hg_lb, hg_norm_g[l], l)
            cmp_n, cmp_T = _compress(kn, cmpk, cmpv)
            o_nsa = _nsa(qT, cmp_n, cmp_T, kn, ksE, vsT, vwT, gT)
            h1 = _outproj(alpha, o_hg.reshape(S, HG_WIDTH), o_nsa.reshape(S, NSA_WIDTH),
                          hb.reshape(S, D), w_out[l], ln1_g[l], ln1_b[l])
            outs.append(_peer_ffn_ln(alpha, h1, peer_wq[l], peer_k1[l], peer_k2[l], up, vp, ln2_g[l], ln2_b[l]))
        h = jnp.stack(outs, axis=0)
    return h
```

```python
import dataclasses
import functools
import math

import jax
import jax.numpy as jnp
import numpy as np
from jax import lax
from jax.experimental import pallas as pl
from jax.experimental.pallas import tpu as pltpu
from jax.experimental.pallas import tpu_sc as plsc

F32 = jnp.float32
BF16 = jnp.bfloat16

HG_HEADS = 4
HG_DK = 128
HG_DV = 128
HG_CHUNK = 64
HG_SUB = 16
HG_WIDTH = HG_HEADS * HG_DV
NSA_HEADS = 8
NSA_KV_HEADS = 2
NSA_GROUP = NSA_HEADS // NSA_KV_HEADS
NSA_HD = 64
NSA_WIDTH = NSA_HEADS * NSA_HD
CMP_LEN = 32
CMP_STRIDE = 16
CMP_HIDDEN = 256
SLC_LEN = 64
SLC_TOPK = 16
WIN = 512
FORCE_SCORE = 1.0e4
PEER_HEADS = 8
PEER_DQ = 256
PEER_NKEYS = 128
PEER_TOPK = 16
LN_EPS = 1e-5
NEG = -1e30
LOG2E = math.log2(math.e)

LANES = 128
PROJ_TM = 512
HG_TC = 256
ATT_TQ = 128
ATT_KT = 512
ATT_KS = 1024
OUT_TM = 512
ROUTE_TR = 256
MIX_TT = 16
SC_WORKERS = 32
SC_LANES = 16
SC_UWIN = 16
SC_USLOT = 4
SC_VWIN = 16
SC_VSLOT = 8
SC_XB = 8
VMEM_LIMIT = 56 * 1024 * 1024


def _cparams(sem):
    return pltpu.CompilerParams(dimension_semantics=sem, vmem_limit_bytes=VMEM_LIMIT)


def _nt(a, b):
    return lax.dot_general(a, b, (((1,), (1,)), ((), ())), preferred_element_type=F32)


def _tn(a, b):
    return lax.dot_general(a, b, (((0,), (0,)), ((), ())), preferred_element_type=F32)


def _nn(a, b):
    return jnp.dot(a, b, preferred_element_type=F32)


def _proj_kernel(x_ref, whg_ref, wqT_ref, wkn_ref, wvT_ref, wgT_ref,
                 hg_ref, qT_ref, kn_ref, ksE_ref, vsT_ref, vwT_ref, gT_ref):
    xb = x_ref[0].astype(BF16)
    hg_ref[0] = _nn(xb, whg_ref[...])
    qT_ref[0] = (_nt(wqT_ref[...], xb) * (NSA_HD ** -0.5 * LOG2E)).astype(BF16)
    kn = _nn(xb, wkn_ref[...])
    for j in range(3):
        for g in range(NSA_KV_HEADS):
            lo = j * 2 * NSA_HD + g * NSA_HD
            kn_ref[0, j, g] = kn[:, lo:lo + NSA_HD].astype(BF16)
    row = pl.program_id(1) * kn.shape[0] + lax.broadcasted_iota(jnp.int32, (kn.shape[0], NSA_HD), 0)
    blk = (row >> int(math.log2(SLC_LEN))) & (ATT_KS // SLC_LEN - 1)
    onehot = (lax.broadcasted_iota(jnp.int32, (kn.shape[0], NSA_HD), 1) == blk).astype(F32)
    for g in range(NSA_KV_HEADS):
        lo = 3 * 2 * NSA_HD + g * NSA_HD
        ksE_ref[0, g] = jnp.concatenate([kn[:, lo:lo + NSA_HD], onehot], axis=1).astype(BF16)
    vT = _nt(wvT_ref[...], xb).astype(BF16)
    tm = vT.shape[1]
    for g in range(NSA_KV_HEADS):
        for c in range(tm // ATT_KT):
            vsT_ref[0, g, c] = vT[g * NSA_HD:(g + 1) * NSA_HD, c * ATT_KT:(c + 1) * ATT_KT]
        for c in range(tm // LANES):
            vwT_ref[0, g, c] = vT[(2 + g) * NSA_HD:(3 + g) * NSA_HD, c * LANES:(c + 1) * LANES]
    gT = _nt(wgT_ref[...], xb)
    gT_ref[0, 0] = gT[0:16]
    gT_ref[0, 1] = gT[16:32]


def _project(x, w_in_l):
    B, S, D = x.shape
    tm = PROJ_TM
    o = np.cumsum((0, 512, 512, 512, 512, 512, 128, 128, 128, 128, 128, 128, 24))
    wb = w_in_l.astype(BF16)
    whg = wb[:, o[0]:o[4]]
    wqT = wb[:, o[4]:o[5]].T
    wkn = jnp.concatenate([wb[:, o[5]:o[6]], wb[:, o[6]:o[7]], wb[:, o[9]:o[10]], wb[:, o[7]:o[8]]], axis=1)
    wvT = jnp.concatenate([wb[:, o[8]:o[9]], wb[:, o[10]:o[11]]], axis=1).T
    wg = wb[:, o[11]:o[12]].reshape(D, NSA_KV_HEADS, NSA_GROUP * 3)
    wgT = jnp.pad(wg, ((0, 0), (0, 0), (0, 16 - NSA_GROUP * 3))).reshape(D, 32).T
    const = lambda shp: pl.BlockSpec(shp, lambda b, i: (0,) * len(shp))
    out_shape = (
        jax.ShapeDtypeStruct((B, S, 4 * 512), F32),
        jax.ShapeDtypeStruct((B, NSA_WIDTH, S), BF16),
        jax.ShapeDtypeStruct((B, 3, NSA_KV_HEADS, S, NSA_HD), BF16),
        jax.ShapeDtypeStruct((B, NSA_KV_HEADS, S, 2 * NSA_HD), BF16),
        jax.ShapeDtypeStruct((B, NSA_KV_HEADS, S // ATT_KT, NSA_HD, ATT_KT), BF16),
        jax.ShapeDtypeStruct((B, NSA_KV_HEADS, S // LANES, NSA_HD, LANES), BF16),
        jax.ShapeDtypeStruct((B, NSA_KV_HEADS, 16, S), F32),
    )
    out_specs = (
        pl.BlockSpec((1, tm, 2048), lambda b, i: (b, i, 0)),
        pl.BlockSpec((1, NSA_WIDTH, tm), lambda b, i: (b, 0, i)),
        pl.BlockSpec((1, 3, NSA_KV_HEADS, tm, NSA_HD), lambda b, i: (b, 0, 0, i, 0)),
        pl.BlockSpec((1, NSA_KV_HEADS, tm, 2 * NSA_HD), lambda b, i: (b, 0, i, 0)),
        pl.BlockSpec((1, NSA_KV_HEADS, tm // ATT_KT, NSA_HD, ATT_KT), lambda b, i: (b, 0, i, 0, 0)),
        pl.BlockSpec((1, NSA_KV_HEADS, tm // LANES, NSA_HD, LANES), lambda b, i: (b, 0, i, 0, 0)),
        pl.BlockSpec((1, NSA_KV_HEADS, 16, tm), lambda b, i: (b, 0, 0, i)),
    )
    return pl.pallas_call(
        _proj_kernel,
        grid=(B, S // tm),
        in_specs=[pl.BlockSpec((1, tm, D), lambda b, i: (b, i, 0)),
                  const(whg.shape), const(wqT.shape), const(wkn.shape), const(wvT.shape), const(wgT.shape)],
        out_specs=out_specs,
        out_shape=out_shape,
        compiler_params=_cparams(("parallel", "arbitrary")),
        name="in_proj",
    )(x, whg, wqT, wkn, wvT, wgT)


def _split3(a):
    hi = a.astype(BF16)
    r = a - hi.astype(F32)
    mid = r.astype(BF16)
    lo = (r - mid.astype(F32)).astype(BF16)
    return hi, mid, lo


def _hgrn_kernel(layer, hg_ref, lb_ref, ng_ref, o_ref, st_ref):
    C, SB, H, DK = HG_CHUNK, HG_SUB, HG_HEADS, HG_DK
    nsb = C // SB

    @pl.when(pl.program_id(1) == 0)
    def _():
        st_ref[...] = jnp.zeros_like(st_ref)

    lbp = lb_ref[...]
    e = jnp.exp(lbp - jnp.max(lbp, axis=0, keepdims=True))
    lb = jnp.sum(e[:layer + 1], axis=0, keepdims=True) / jnp.sum(e, axis=0, keepdims=True)

    ri = lax.broadcasted_iota(jnp.int32, (C, C), 0)
    ci = lax.broadcasted_iota(jnp.int32, (C, C), 1)
    tril = (ci <= ri).astype(BF16)
    t_sub = lax.broadcasted_iota(jnp.int32, (SB, H * DK), 0)
    lane16 = lax.broadcasted_iota(jnp.int32, (SB, SB), 1)
    ng = ng_ref[...]

    def chunk(c, carry):
        r0 = pl.multiple_of(c * C, C)
        blk = hg_ref[0, pl.ds(r0, C), :]
        q = blk[:, 0:512]
        f = lb + (1.0 - lb) * jax.nn.sigmoid(blk[:, 512:1024])
        lf = jnp.log(f)
        k = 1.0 - f
        v = blk[:, 1024:1536]
        gate = blk[:, 1536:2048]
        hi, mid, lo = _split3(lf)
        b = _nn(tril, hi) + _nn(tril, mid) + _nn(tril, lo)
        b_last = b[C - 1:C, :]
        qe = (q * jnp.exp(b)).astype(BF16)
        ks = (k * jnp.exp(b_last - b)).astype(BF16)
        dec = jnp.exp(b_last)
        vb = v.astype(BF16)
        rblk = jnp.concatenate([jnp.broadcast_to(b[i * SB:i * SB + 1, :], (SB, H * DK)) for i in range(nsb)], axis=0)
        qn = (q * jnp.exp(b - rblk)).astype(BF16)
        outs = []
        for h in range(H):
            sl = slice(h * DK, (h + 1) * DK)
            stT = st_ref[h]
            o_h = _nt(qe[:, sl], stT.astype(BF16))
            rows = []
            for i in range(nsb):
                rs = slice(i * SB, (i + 1) * SB)
                acc = jnp.zeros((SB, HG_DV), F32)
                if i > 0:
                    ke = (k[0:i * SB, sl] * jnp.exp(b[i * SB:i * SB + 1, sl] - b[0:i * SB, sl])).astype(BF16)
                    a_off = _nt(qn[rs, sl], ke)
                    acc = acc + _nn(a_off.astype(BF16), vb[0:i * SB, sl])
                qi, ki, bi = q[rs, sl], k[rs, sl], b[rs, sl]
                a_d = jnp.zeros((SB, SB), F32)
                for s in range(SB):
                    w = qi * ki[s:s + 1, :] * jnp.exp(jnp.minimum(bi - bi[s:s + 1, :], 0.0))
                    col = jnp.sum(w, axis=1, keepdims=True)
                    a_d = a_d + jnp.where((lane16 == s) & (t_sub[:, 0:SB] >= s), col, 0.0)
                acc = acc + _nn(a_d.astype(BF16), vb[rs, sl])
                rows.append(acc)
            o_h = o_h + jnp.concatenate(rows, axis=0)
            st_ref[h] = dec[:, sl] * stT + _tn(vb[:, sl], ks[:, sl])
            o_h = o_h * lax.rsqrt(jnp.mean(o_h * o_h, axis=1, keepdims=True) + LN_EPS) * ng
            outs.append(o_h)
        o = jnp.concatenate(outs, axis=1) * jax.nn.silu(gate)
        o_ref[0, pl.ds(r0, C), :] = o.astype(BF16)
        return carry

    lax.fori_loop(0, hg_ref.shape[1] // C, chunk, 0)


def _hgrn2(hg, hg_lb, norm_g_l, layer):
    B, S, _ = hg.shape
    tc = HG_TC
    return pl.pallas_call(
        functools.partial(_hgrn_kernel, layer),
        grid=(B, S // tc),
        in_specs=[pl.BlockSpec((1, tc, 2048), lambda b, i: (b, i, 0)),
                  pl.BlockSpec(hg_lb.shape, lambda b, i: (0, 0)),
                  pl.BlockSpec((1, HG_DV), lambda b, i: (0, 0))],
        out_specs=pl.BlockSpec((1, tc, HG_WIDTH), lambda b, i: (b, i, 0)),
        out_shape=jax.ShapeDtypeStruct((B, S, HG_WIDTH), BF16),
        scratch_shapes=[pltpu.VMEM((HG_HEADS, HG_DV, HG_DK), F32)],
        compiler_params=_cparams(("parallel", "arbitrary")),
        name="hgrn2",
    )(hg, hg_lb.astype(F32), norm_g_l.reshape(1, HG_DV).astype(F32))


def _compress_kernel(c_ref, pe_ref, w1_ref, b1_ref, w2_ref, w2T_ref, b2_ref, b2T_ref, o_ref, oT_ref):
    half = CMP_STRIDE * NSA_HD
    c = c_ref[0, 0, 0].astype(F32)
    pe = pe_ref[0]
    ca = (c + pe[:, 0:half]).astype(BF16)
    cb = (c + pe[:, half:2 * half]).astype(BF16)
    pa = _nn(ca, w1_ref[0, 0:half, :])
    pb = _nn(cb, w1_ref[0, half:2 * half, :])
    nbp = pa.shape[0]
    h = pa + pltpu.roll(pb, nbp - 1, 0) + b1_ref[0]
    h = jax.nn.gelu(h).astype(BF16)
    o_ref[0, 0, 0] = (_nn(h, w2_ref[0]) + b2_ref[0]).astype(BF16)
    oT_ref[0, 0, 0] = (_nt(w2T_ref[0], h) + b2T_ref[0]).astype(BF16)


def _compress(kn, cmpk, cmpv):
    B, _, G, S, HD = kn.shape
    nbp = S // CMP_STRIDE
    c = kn[:, 0:2].reshape(B, 2, G, nbp, CMP_STRIDE * HD)
    stack = lambda a, b, f: jnp.stack([f(a), f(b)], axis=0)
    pe = stack(cmpk[0], cmpv[0], lambda t: t.reshape(1, CMP_LEN * HD).astype(F32))
    w1 = stack(cmpk[1], cmpv[1], lambda t: t.astype(BF16))
    b1 = stack(cmpk[2], cmpv[2], lambda t: t.reshape(1, CMP_HIDDEN).astype(F32))
    w2 = stack(cmpk[3], cmpv[3], lambda t: t.astype(BF16))
    w2T = stack(cmpk[3], cmpv[3], lambda t: t.astype(BF16).T)
    b2 = stack(cmpk[4], cmpv[4], lambda t: t.reshape(1, HD).astype(F32))
    b2T = stack(cmpk[4], cmpv[4], lambda t: t.reshape(HD, 1).astype(F32))
    wspec = lambda a: pl.BlockSpec((1,) + a.shape[1:], lambda b, t, g: (t,) + (0,) * (a.ndim - 1))
    return pl.pallas_call(
        _compress_kernel,
        grid=(B, 2, G),
        in_specs=[pl.BlockSpec((1, 1, 1, nbp, CMP_STRIDE * HD), lambda b, t, g: (b, t, g, 0, 0)),
                  wspec(pe), wspec(w1), wspec(b1), wspec(w2), wspec(w2T), wspec(b2), wspec(b2T)],
        out_specs=(pl.BlockSpec((1, 1, 1, nbp, HD), lambda b, t, g: (b, t, g, 0, 0)),
                   pl.BlockSpec((1, 1, 1, HD, nbp), lambda b, t, g: (b, t, g, 0, 0))),
        out_shape=(jax.ShapeDtypeStruct((B, 2, G, nbp, HD), BF16),
                   jax.ShapeDtypeStruct((B, 2, G, HD, nbp), BF16)),
        compiler_params=_cparams(("parallel", "parallel", "parallel")),
        name="kv_compress",
    )(c, pe, w1, b1, w2, w2T, b2, b2T)


def _topk_mask_rows(score, k):
    R = score.shape[0]
    r_io = lax.broadcasted_iota(jnp.int32, score.shape, 0)
    s = score
    for _ in range(k):
        m = jnp.max(s, axis=0, keepdims=True)
        idx = jnp.min(jnp.where(s == m, r_io, R), axis=0, keepdims=True)
        s = jnp.where(r_io == idx, -jnp.inf, s)
    return (s == -jnp.inf).astype(F32)


def _with_ones_row(vT):
    pad = (lax.broadcasted_iota(jnp.int32, (16, vT.shape[1]), 0) == 0).astype(BF16)
    return jnp.concatenate([vT, pad], axis=0)


def _nsa_kernel(qT_ref, kc_ref, vcT_ref, ks_ref, vsT_ref, kw_ref, vwT_ref, gT_ref, ovT_ref,
                o_ref, sel_ref):
    TQ, G, HD, KT = ATT_TQ, NSA_GROUP, NSA_HD, ATT_KT
    qi = pl.program_id(2)
    t0 = qi * TQ
    qT = qT_ref[0]
    q4 = jnp.concatenate([qT[r * HD:(r + 1) * HD, :] for r in range(G)], axis=1)
    pos1 = t0 + lax.broadcasted_iota(jnp.int32, (1, TQ), 1)
    pos4 = jnp.concatenate([pos1] * G, axis=1)

    nbp = kc_ref.shape[3]
    s_c = _nn(kc_ref[0, 0, 0], q4)
    n_io = lax.broadcasted_iota(jnp.int32, (nbp, 1), 0)
    mask_c = (n_io * CMP_STRIDE + (CMP_LEN - 1)) <= pos4
    s_c = jnp.where(mask_c, s_c, NEG)
    p_c = jnp.where(mask_c, jnp.exp2(s_c - jnp.max(s_c, axis=0, keepdims=True)), 0.0)
    p_c = p_c / jnp.maximum(jnp.sum(p_c, axis=0, keepdims=True), 1e-30)
    o_c = _nn(vcT_ref[0, 0, 0], p_c.astype(BF16))
    p_sum = p_c[:, 0:TQ]
    for r in range(1, G):
        p_sum = p_sum + p_c[:, r * TQ:(r + 1) * TQ]
    imp = _nn(ovT_ref[...], p_sum.astype(BF16))

    ns = imp.shape[0]
    j_io = lax.broadcasted_iota(jnp.int32, (ns, 1), 0)
    cur = pos1 >> int(math.log2(SLC_LEN))
    forced = (j_io == 0) | (j_io == cur) | (j_io == cur - 1)
    score = jnp.where(forced, FORCE_SCORE, jnp.where(j_io * SLC_LEN <= pos1, imp, -1.0))
    sel_ref[...] = (_topk_mask_rows(score, min(SLC_TOPK, ns)) - 1.0) * (-NEG)

    KS = ATT_KS
    bps = KS // SLC_LEN
    k_io = lax.broadcasted_iota(jnp.int32, (KS, 1), 0)
    zrows = jnp.zeros((HD - bps, G * TQ), BF16)

    def sel_step(ks_i, carry, diagonal):
        m, acc = carry
        k0 = pl.multiple_of(ks_i * KS, KS)
        bias = sel_ref[pl.ds(pl.multiple_of(ks_i * bps, bps), bps), :].astype(BF16)
        rhs = jnp.concatenate([q4, jnp.concatenate([bias] * G, axis=1), zrows], axis=0)
        s = _nn(ks_ref[0, 0, pl.ds(k0, KS), :], rhs)
        if diagonal:
            s = jnp.where((k0 + k_io) <= pos4, s, NEG)
        m_new = jnp.maximum(m, jnp.max(s, axis=0, keepdims=True))
        p = jnp.exp2(s - m_new).astype(BF16)
        acc = jnp.exp2(m - m_new) * acc
        for c in range(KS // KT):
            acc = acc + _nn(_with_ones_row(vsT_ref[0, 0, ks_i * (KS // KT) + c]), p[c * KT:(c + 1) * KT])
        return m_new, acc

    n_ks = (t0 + TQ - 1) // KS + 1
    init = (jnp.full((1, G * TQ), NEG, F32), jnp.zeros((HD + 16, G * TQ), F32))
    carry = lax.fori_loop(0, n_ks - 1, lambda i, c: sel_step(i, c, False), init)
    _, acc_s = sel_step(n_ks - 1, carry, True)
    o_s = acc_s[0:HD] / jnp.maximum(acc_s[HD:HD + 1], 1e-30)

    nwt = WIN // TQ + 1
    kw_t, vw_t, kp_t = [], [], []
    r_io = lax.broadcasted_iota(jnp.int32, (TQ, 1), 0)
    for i in range(nwt):
        idx = qi - (nwt - 1) + i
        idc = jnp.maximum(idx, 0)
        kw_t.append(kw_ref[0, 0, 0, pl.ds(pl.multiple_of(idc * TQ, TQ), TQ), :])
        vw_t.append(vwT_ref[0, 0, idc])
        kp_t.append(idx * TQ + r_io)
    kpos = jnp.concatenate(kp_t, axis=0)
    s_w = _nn(jnp.concatenate(kw_t, axis=0), q4)
    delta = pos4 - kpos
    s_w = jnp.where((delta >= 0) & (delta < WIN) & (kpos >= 0), s_w, NEG)
    p_w = jnp.exp2(s_w - jnp.max(s_w, axis=0, keepdims=True)).astype(BF16)
    acc_w = _nn(_with_ones_row(jnp.concatenate(vw_t, axis=1)), p_w)
    o_w = acc_w[0:HD] / jnp.maximum(acc_w[HD:HD + 1], 1e-30)

    gs = jax.nn.sigmoid(gT_ref[0, 0])
    gate = lambda br: jnp.concatenate([gs[r * 3 + br:r * 3 + br + 1, :] for r in range(G)], axis=1)
    o = gate(0) * o_c + gate(1) * o_s + gate(2) * o_w
    o_ref[0] = jnp.concatenate([o[:, r * TQ:(r + 1) * TQ].T for r in range(G)], axis=1).astype(BF16)


def _overlap_T(S):
    nbp = S // CMP_STRIDE
    n_slc = S // SLC_LEN
    cs = np.arange(nbp) * CMP_STRIDE
    ss = np.arange(n_slc) * SLC_LEN
    ov = np.clip(np.minimum(cs[:, None] + CMP_LEN, ss[None, :] + SLC_LEN)
                 - np.maximum(cs[:, None], ss[None, :]), 0, None) / CMP_LEN
    ov[nbp - 1:, :] = 0.0
    return jnp.asarray(ov.T, BF16)


def _nsa(qT, cmp_n, cmp_T, kn, ksE, vsT, vwT, gT):
    B, _, S = qT.shape
    G2, TQ, HD = NSA_KV_HEADS, ATT_TQ, NSA_HD
    nbp = S // CMP_STRIDE
    ns = S // SLC_LEN
    ovT = _overlap_T(S)
    return pl.pallas_call(
        _nsa_kernel,
        grid=(B, G2, S // TQ),
        in_specs=[
            pl.BlockSpec((1, NSA_GROUP * HD, TQ), lambda b, g, i: (b, g, i)),
            pl.BlockSpec((1, 1, 1, nbp, HD), lambda b, g, i: (b, 0, g, 0, 0)),
            pl.BlockSpec((1, 1, 1, HD, nbp), lambda b, g, i: (b, 1, g, 0, 0)),
            pl.BlockSpec((1, 1, S, 2 * HD), lambda b, g, i: (b, g, 0, 0)),
            pl.BlockSpec((1, 1, S // ATT_KT, HD, ATT_KT), lambda b, g, i: (b, g, 0, 0, 0)),
            pl.BlockSpec((1, 1, 1, S, HD), lambda b, g, i: (b, 2, g, 0, 0)),
            pl.BlockSpec((1, 1, S // LANES, HD, LANES), lambda b, g, i: (b, g, 0, 0, 0)),
            pl.BlockSpec((1, 1, 16, TQ), lambda b, g, i: (b, g, 0, i)),
            pl.BlockSpec((ns, nbp), lambda b, g, i: (0, 0)),
        ],
        out_specs=pl.BlockSpec((1, TQ, NSA_GROUP * HD), lambda b, g, i: (b, i, g)),
        out_shape=jax.ShapeDtypeStruct((B, S, NSA_WIDTH), BF16),
        scratch_shapes=[pltpu.VMEM((ns, TQ), F32)],
        compiler_params=_cparams(("parallel", "parallel", "arbitrary")),
        name="nsa_attn",
    )(qT, cmp_n, cmp_T, ksE, vsT, kn, vwT, gT, ovT)


def _layernorm(t, g, b):
    mu = jnp.mean(t, axis=-1, keepdims=True)
    d = t - mu
    var = jnp.mean(d * d, axis=-1, keepdims=True)
    return d * lax.rsqrt(var + LN_EPS) * g + b


def _outproj_kernel(alpha, ohg_ref, onsa_ref, x_ref, w_ref, g_ref, b_ref, h_ref):
    mix = _nn(ohg_ref[...], w_ref[0:HG_WIDTH, :]) + _nn(onsa_ref[...], w_ref[HG_WIDTH:HG_WIDTH + NSA_WIDTH, :])
    h_ref[...] = _layernorm(alpha * x_ref[...] + mix, g_ref[...], b_ref[...])


def _outproj(alpha, o_hg, o_nsa, x2, w_out_l, g, b):
    n, D = x2.shape
    tm = OUT_TM
    row = lambda w: pl.BlockSpec((tm, w), lambda i: (i, 0))
    const = lambda shp: pl.BlockSpec(shp, lambda i: (0, 0))
    return pl.pallas_call(
        functools.partial(_outproj_kernel, alpha),
        grid=(n // tm,),
        in_specs=[row(HG_WIDTH), row(NSA_WIDTH), row(D), const(w_out_l.shape), const((1, D)), const((1, D))],
        out_specs=row(D),
        out_shape=jax.ShapeDtypeStruct((n, D), F32),
        compiler_params=_cparams(("parallel",)),
        name="out_proj_ln",
    )(o_hg, o_nsa, x2, w_out_l.astype(BF16), g.reshape(1, D).astype(F32), b.reshape(1, D).astype(F32))


def _topk_rows(s, k):
    R = s.shape[0]
    r_io = lax.broadcasted_iota(jnp.int32, s.shape, 0)
    vals, idxs = [], []
    for _ in range(k):
        m = jnp.max(s, axis=0, keepdims=True)
        idx = jnp.min(jnp.where(s == m, r_io, R), axis=0, keepdims=True)
        vals.append(m)
        idxs.append(idx)
        s = jnp.where(r_io == idx, -jnp.inf, s)
    return jnp.concatenate(vals, axis=0), jnp.concatenate(idxs, axis=0)


def _route_kernel(h_ref, wq_ref, k1_ref, k2_ref, e_ref, g_ref):
    K, half = PEER_TOPK, PEER_DQ // 2
    q = _nn(h_ref[...].astype(BF16), wq_ref[...]).astype(BF16)
    for h in range(PEER_HEADS):
        s1 = _nt(k1_ref[...], q[:, h * PEER_DQ:h * PEER_DQ + half])
        s2 = _nt(k2_ref[...], q[:, h * PEER_DQ + half:(h + 1) * PEER_DQ])
        v1, i1 = _topk_rows(s1, K)
        v2, i2 = _topk_rows(s2, K)
        cand = jnp.concatenate([v1[a:a + 1, :] + v2 for a in range(K)], axis=0)
        top_s, top_c = _topk_rows(cand, K)
        ca, cb = top_c >> int(math.log2(K)), top_c & (K - 1)
        e1 = jnp.zeros_like(top_c)
        e2 = jnp.zeros_like(top_c)
        for a in range(K):
            e1 = e1 + jnp.where(ca == a, i1[a:a + 1, :], 0)
            e2 = e2 + jnp.where(cb == a, i2[a:a + 1, :], 0)
        e_ref[h * K:(h + 1) * K, :] = e1 * PEER_NKEYS + e2
        ex = jnp.exp(top_s - top_s[0:1, :])
        g_ref[h * K:(h + 1) * K, :] = ex / jnp.sum(ex, axis=0, keepdims=True)


def _route(h1, wq, k1, k2):
    n, D = h1.shape
    tr = ROUTE_TR
    hk = PEER_HEADS * PEER_TOPK
    const = lambda shp: pl.BlockSpec(shp, lambda i: (0, 0))
    return pl.pallas_call(
        _route_kernel,
        grid=(n // tr,),
        in_specs=[pl.BlockSpec((tr, D), lambda i: (i, 0)), const(wq.shape), const(k1.shape), const(k2.shape)],
        out_specs=(pl.BlockSpec((hk, tr), lambda i: (0, i)), pl.BlockSpec((hk, tr), lambda i: (0, i))),
        out_shape=(jax.ShapeDtypeStruct((hk, n), jnp.int32), jax.ShapeDtypeStruct((hk, n), F32)),
        compiler_params=_cparams(("parallel",)),
        name="peer_route",
    )(h1, wq.astype(BF16), k1.astype(BF16), k2.astype(BF16))


def _pack_rows(t):
    half = t.shape[1] // 2
    tb = t.astype(BF16)
    lo = lax.bitcast_convert_type(tb[:, :half], jnp.uint16).astype(jnp.uint32)
    hi = lax.bitcast_convert_type(tb[:, half:], jnp.uint16).astype(jnp.uint32)
    return lo | (hi << 16)


def _sc_peer(utab, vtab, idx, x):
    n = x.shape[0]
    w = utab.shape[1]
    hk = PEER_HEADS * PEER_TOPK
    L = SC_LANES
    T = n // SC_WORKERS
    uwin = SC_UWIN
    upt = hk // uwin
    vpt = hk // SC_VWIN
    per_u = vpt // upt
    vlead = SC_VSLOT // 2
    assert n % SC_WORKERS == 0 and T % SC_XB == 0 and uwin % L == 0
    assert upt % SC_USLOT == 0 and vpt % SC_VSLOT == 0 and vpt % upt == 0
    vsteps = T * vpt
    mesh = plsc.VectorSubcoreMesh(core_axis_name="c", subcore_axis_name="s")
    cp = dataclasses.replace(pltpu.CompilerParams(), needs_layout_passes=False)

    @pl.kernel(out_type=(jax.ShapeDtypeStruct((n * hk,), F32), jax.ShapeDtypeStruct((n * hk, w), utab.dtype)),
               mesh=mesh, compiler_params=cp,
               scratch_types=[pltpu.VMEM((T * hk,), jnp.int32),
                              pltpu.VMEM((SC_XB, 2 * w), F32),
                              pltpu.VMEM((SC_USLOT, uwin, w), utab.dtype),
                              pltpu.VMEM((SC_VSLOT, SC_VWIN, w), vtab.dtype),
                              pltpu.VMEM((L, L), F32),
                              pltpu.VMEM((SC_XB * hk,), F32),
                              pltpu.SemaphoreType.DMA((SC_USLOT,)),
                              pltpu.SemaphoreType.DMA((SC_VSLOT,)),
                              pltpu.SemaphoreType.DMA((SC_VSLOT,))])
    def peer(u_hbm, v_hbm, i_hbm, x_hbm, h_hbm, vg_hbm, idx_v, x_v, ubuf, vbuf, acc_s, h_v, usem, vsem, wsem):
        wid = lax.axis_index("c") * (SC_WORKERS // 2) + lax.axis_index("s")
        tok0 = wid * T
        row0 = tok0 * hk
        pltpu.sync_copy(i_hbm.at[pl.ds(row0, T * hk)], idx_v)

        def ufetch(t, j):
            slot = j % SC_USLOT
            return pltpu.make_async_copy(u_hbm.at[idx_v.at[pl.ds(t * hk + j * uwin, uwin)]], ubuf.at[slot], usem.at[slot])

        def vfetch(s, slot):
            return pltpu.make_async_copy(v_hbm.at[idx_v.at[pl.ds(s * SC_VWIN, SC_VWIN)]], vbuf.at[slot], vsem.at[slot])

        def vflush(s, slot):
            return pltpu.make_async_copy(vbuf.at[slot], vg_hbm.at[pl.ds(row0 + s * SC_VWIN, SC_VWIN)], wsem.at[slot])

        for j in range(SC_USLOT):
            ufetch(0, j).start()
        for j in range(vlead):
            vfetch(j, j).start()

        def vstep(s, slot):
            other = (slot + vlead) % SC_VSLOT
            vfetch(s, slot).wait()
            vflush(s, slot).start()

            @pl.when(s >= vlead)
            def _():
                vflush(s - vlead, other).wait()

            @pl.when(s + vlead < vsteps)
            def _():
                vfetch(s + vlead, other).start()

        lane = lax.iota(jnp.int32, L)

        @pl.loop(0, T // SC_XB)
        def _(tb):
            pltpu.sync_copy(x_hbm.at[pl.ds(pl.multiple_of(tok0 + tb * SC_XB, SC_XB), SC_XB)], x_v)

            @pl.loop(0, SC_XB)
            def _(tt):
                t = tb * SC_XB + tt
                for j in range(upt):
                    for i in range(per_u):
                        vstep(t * vpt + j * per_u + i, (j * per_u + i) % SC_VSLOT)
                    ufetch(t, j).wait()

                    def dot_body(jj, accs):
                        off = pl.multiple_of(jj * L, L)
                        xlo = x_v[tt, pl.ds(off, L)]
                        xhi = x_v[tt, pl.ds(w + off, L)]
                        out = []
                        for r in range(uwin):
                            wv = ubuf[j % SC_USLOT, r, pl.ds(off, L)]
                            lo = plsc.bitcast(wv << 16, F32)
                            hi = plsc.bitcast(wv & jnp.uint32(0xFFFF0000), F32)
                            out.append(accs[r] + lo * xlo + hi * xhi)
                        return tuple(out)

                    accs = lax.fori_loop(0, w // L, dot_body, tuple(jnp.zeros((L,), F32) for _ in range(uwin)))

                    if j + SC_USLOT < upt:
                        ufetch(t, j + SC_USLOT).start()
                    else:
                        @pl.when(t + 1 < T)
                        def _():
                            ufetch(t + 1, j + SC_USLOT - upt).start()

                    for g in range(uwin // L):
                        for r in range(L):
                            acc_s[r, :] = accs[g * L + r]
                        tot = jnp.zeros((L,), F32)
                        for c in range(L):
                            tot = tot + plsc.load_gather(acc_s, [lane, jnp.full((L,), c, jnp.int32)])
                        h_v[pl.ds(tt * hk + j * uwin + g * L, L)] = tot

            pltpu.sync_copy(h_v, h_hbm.at[pl.ds(pl.multiple_of(row0 + tb * SC_XB * hk, SC_XB * hk), SC_XB * hk)])

        for s in range(vsteps - vlead, vsteps):
            vflush(s, s % SC_VSLOT).wait()

    return peer(utab, vtab, idx, x)


def _unpack(w):
    lo = lax.bitcast_convert_type(w << 16, F32)
    hi = lax.bitcast_convert_type(w & jnp.uint32(0xFFFF0000), F32)
    return lo, hi


def _mix_kernel(alpha, h_ref, s_ref, gate_ref, vg_ref, g_ref, b_ref, o_ref):
    TT, HK = MIX_TT, PEER_HEADS * PEER_TOPK
    x = h_ref[...]
    wrow = gate_ref[...] * jax.nn.gelu(s_ref[...])
    eye = lax.broadcasted_iota(jnp.int32, (HK, HK), 0) == lax.broadcasted_iota(jnp.int32, (HK, HK), 1)
    rows = []
    for t in range(TT):
        wcol = jnp.sum(jnp.where(eye, jnp.broadcast_to(wrow[t:t + 1, :], (HK, HK)), 0.0),
                       axis=1, keepdims=True)
        vlo, vhi = _unpack(vg_ref[t * HK:(t + 1) * HK, :])
        rows.append(jnp.concatenate([jnp.sum(wcol * vlo, axis=0, keepdims=True),
                                     jnp.sum(wcol * vhi, axis=0, keepdims=True)], axis=1))
    ffn = jnp.concatenate(rows, axis=0)
    o_ref[...] = _layernorm(alpha * x + ffn, g_ref[...], b_ref[...])


def _mix(alpha, h1, scores, gates, vg, g, b):
    n, D = h1.shape
    tt = MIX_TT
    hk = PEER_HEADS * PEER_TOPK
    const = lambda shp: pl.BlockSpec(shp, lambda i: (0, 0))
    return pl.pallas_call(
        functools.partial(_mix_kernel, alpha),
        grid=(n // tt,),
        in_specs=[pl.BlockSpec((tt, D), lambda i: (i, 0)), pl.BlockSpec((tt, hk), lambda i: (i, 0)),
                  pl.BlockSpec((tt, hk), lambda i: (i, 0)), pl.BlockSpec((tt * hk, D // 2), lambda i: (i, 0)),
                  const((1, D)), const((1, D))],
        out_specs=pl.BlockSpec((tt, D), lambda i: (i, 0)),
        out_shape=jax.ShapeDtypeStruct((n, D), F32),
        compiler_params=_cparams(("parallel",)),
        name="peer_mix_ln",
    )(h1, scores, gates, vg, g.reshape(1, D).astype(F32), b.reshape(1, D).astype(F32))


PEER_CHUNK = 4096


def _peer_ffn_ln(alpha, h1, wq, k1, k2, up, vp, g, b):
    n, D = h1.shape
    hk = PEER_HEADS * PEER_TOPK
    eT, gT = _route(h1, wq, k1, k2)
    experts = eT.T
    gates = gT.T
    ch = min(PEER_CHUNK, n)
    outs = []
    for c in range(n // ch):
        hc = h1[c * ch:(c + 1) * ch]
        scores, vg = _sc_peer(up, vp, experts[c * ch:(c + 1) * ch].reshape(ch * hk), hc)
        outs.append(_mix(alpha, hc, scores.reshape(ch, hk), gates[c * ch:(c + 1) * ch], vg, g, b))
    return jnp.concatenate(outs, axis=0)


def kernel(x, w_in, w_out, hg_lb, hg_norm_g, cmpk_pe, cmpk_w1, cmpk_b1, cmpk_w2, cmpk_b2,
           cmpv_pe, cmpv_w1, cmpv_b1, cmpv_w2, cmpv_b2, ln1_g, ln1_b,
           peer_wq, peer_k1, peer_k2, peer_u, peer_v, ln2_g, ln2_b):
    B, S, D = x.shape
    depth = w_in.shape[0]
    alpha = (2.0 * depth) ** 0.25
    h = x
    for l in range(depth):
        up, vp = _pack_rows(peer_u[l]), _pack_rows(peer_v[l])
        cmpk = (cmpk_pe[l], cmpk_w1[l], cmpk_b1[l], cmpk_w2[l], cmpk_b2[l])
        cmpv = (cmpv_pe[l], cmpv_w1[l], cmpv_b1[l], cmpv_w2[l], cmpv_b2[l])
        outs = []
        for bi in range(B):
            hb = h[bi:bi + 1]
            hg, qT, kn, ksE, vsT, vwT, gT = _project(hb, w_in[l])
            o_hg = _hgrn2(hg, hg_lb, hg_norm_g[l], l)
            cmp_n, cmp_T = _compress(kn, cmpk, cmpv)
            o_nsa = _nsa(qT, cmp_n, cmp_T, kn, ksE, vsT, vwT, gT)
            h1 = _outproj(alpha, o_hg.reshape(S, HG_WIDTH), o_nsa.reshape(S, NSA_WIDTH),
                          hb.reshape(S, D), w_out[l], ln1_g[l], ln1_b[l])
            outs.append(_peer_ffn_ln(alpha, h1, peer_wq[l], peer_k1[l], peer_k2[l], up, vp, ln2_g[l], ln2_b[l]))
        h = jnp.stack(outs, axis=0)
    return h
```

```python
import dataclasses
import functools
import math

import jax
import jax.numpy as jnp
import numpy as np
from jax import lax
from jax.experimental import pallas as pl
from jax.experimental.pallas import tpu as pltpu
from jax.experimental.pallas import tpu_sc as plsc

F32 = jnp.float32
BF16 = jnp.bfloat16

HG_HEADS = 4
HG_DK = 128
HG_DV = 128
HG_CHUNK = 64
HG_SUB = 16
HG_WIDTH = HG_HEADS * HG_DV
NSA_HEADS = 8
NSA_KV_HEADS = 2
NSA_GROUP = NSA_HEADS // NSA_KV_HEADS
NSA_HD = 64
NSA_WIDTH = NSA_HEADS * NSA_HD
CMP_LEN = 32
CMP_STRIDE = 16
CMP_HIDDEN = 256
SLC_LEN = 64
SLC_TOPK = 16
WIN = 512
FORCE_SCORE = 1.0e4
PEER_HEADS = 8
PEER_DQ = 256
PEER_NKEYS = 128
PEER_TOPK = 16
LN_EPS = 1e-5
NEG = -1e30
LOG2E = math.log2(math.e)

LANES = 128
PROJ_TM = 512
HG_TC = 256
ATT_TQ = 128
ATT_KT = 512
ATT_KS = 1024
OUT_TM = 512
ROUTE_TR = 256
MIX_TT = 16
SC_WORKERS = 32
SC_LANES = 16
SC_UWIN = 16
SC_USLOT = 4
SC_VWIN = 16
SC_VSLOT = 8
SC_XB = 8
VMEM_LIMIT = 56 * 1024 * 1024


def _cparams(sem):
    return pltpu.CompilerParams(dimension_semantics=sem, vmem_limit_bytes=VMEM_LIMIT)


def _nt(a, b):
    return lax.dot_general(a, b, (((1,), (1,)), ((), ())), preferred_element_type=F32)


def _tn(a, b):
    return lax.dot_general(a, b, (((0,), (0,)), ((), ())), preferred_element_type=F32)


def _nn(a, b):
    return jnp.dot(a, b, preferred_element_type=F32)


def _proj_kernel(x_ref, whg_ref, wqT_ref, wkn_ref, wvT_ref, wgT_ref,
                 hg_ref, qT_ref, kn_ref, ksE_ref, vsT_ref, vwT_ref, gT_ref):
    xb = x_ref[0].astype(BF16)
    hg_ref[0] = _nn(xb, whg_ref[...])
    qT_ref[0] = (_nt(wqT_ref[...], xb) * (NSA_HD ** -0.5 * LOG2E)).astype(BF16)
    kn = _nn(xb, wkn_ref[...])
    for j in range(3):
        for g in range(NSA_KV_HEADS):
            lo = j * 2 * NSA_HD + g * NSA_HD
            kn_ref[0, j, g] = kn[:, lo:lo + NSA_HD].astype(BF16)
    row = pl.program_id(1) * kn.shape[0] + lax.broadcasted_iota(jnp.int32, (kn.shape[0], NSA_HD), 0)
    blk = (row >> int(math.log2(SLC_LEN))) & (ATT_KS // SLC_LEN - 1)
    onehot = (lax.broadcasted_iota(jnp.int32, (kn.shape[0], NSA_HD), 1) == blk).astype(F32)
    for g in range(NSA_KV_HEADS):
        lo = 3 * 2 * NSA_HD + g * NSA_HD
        ksE_ref[0, g] = jnp.concatenate([kn[:, lo:lo + NSA_HD], onehot], axis=1).astype(BF16)
    vT = _nt(wvT_ref[...], xb).astype(BF16)
    tm = vT.shape[1]
    for g in range(NSA_KV_HEADS):
        for c in range(tm // ATT_KT):
            vsT_ref[0, g, c] = vT[g * NSA_HD:(g + 1) * NSA_HD, c * ATT_KT:(c + 1) * ATT_KT]
        for c in range(tm // LANES):
            vwT_ref[0, g, c] = vT[(2 + g) * NSA_HD:(3 + g) * NSA_HD, c * LANES:(c + 1) * LANES]
    gT = _nt(wgT_ref[...], xb)
    gT_ref[0, 0] = gT[0:16]
    gT_ref[0, 1] = gT[16:32]


def _project(x, w_in_l):
    B, S, D = x.shape
    tm = PROJ_TM
    o = np.cumsum((0, 512, 512, 512, 512, 512, 128, 128, 128, 128, 128, 128, 24))
    wb = w_in_l.astype(BF16)
    whg = wb[:, o[0]:o[4]]
    wqT = wb[:, o[4]:o[5]].T
    wkn = jnp.concatenate([wb[:, o[5]:o[6]], wb[:, o[6]:o[7]], wb[:, o[9]:o[10]], wb[:, o[7]:o[8]]], axis=1)
    wvT = jnp.concatenate([wb[:, o[8]:o[9]], wb[:, o[10]:o[11]]], axis=1).T
    wg = wb[:, o[11]:o[12]].reshape(D, NSA_KV_HEADS, NSA_GROUP * 3)
    wgT = jnp.pad(wg, ((0, 0), (0, 0), (0, 16 - NSA_GROUP * 3))).reshape(D, 32).T
    const = lambda shp: pl.BlockSpec(shp, lambda b, i: (0,) * len(shp))
    out_shape = (
        jax.ShapeDtypeStruct((B, S, 4 * 512), F32),
        jax.ShapeDtypeStruct((B, NSA_WIDTH, S), BF16),
        jax.ShapeDtypeStruct((B, 3, NSA_KV_HEADS, S, NSA_HD), BF16),
        jax.ShapeDtypeStruct((B, NSA_KV_HEADS, S, 2 * NSA_HD), BF16),
        jax.ShapeDtypeStruct((B, NSA_KV_HEADS, S // ATT_KT, NSA_HD, ATT_KT), BF16),
        jax.ShapeDtypeStruct((B, NSA_KV_HEADS, S // LANES, NSA_HD, LANES), BF16),
        jax.ShapeDtypeStruct((B, NSA_KV_HEADS, 16, S), F32),
    )
    out_specs = (
        pl.BlockSpec((1, tm, 2048), lambda b, i: (b, i, 0)),
        pl.BlockSpec((1, NSA_WIDTH, tm), lambda b, i: (b, 0, i)),
        pl.BlockSpec((1, 3, NSA_KV_HEADS, tm, NSA_HD), lambda b, i: (b, 0, 0, i, 0)),
        pl.BlockSpec((1, NSA_KV_HEADS, tm, 2 * NSA_HD), lambda b, i: (b, 0, i, 0)),
        pl.BlockSpec((1, NSA_KV_HEADS, tm // ATT_KT, NSA_HD, ATT_KT), lambda b, i: (b, 0, i, 0, 0)),
        pl.BlockSpec((1, NSA_KV_HEADS, tm // LANES, NSA_HD, LANES), lambda b, i: (b, 0, i, 0, 0)),
        pl.BlockSpec((1, NSA_KV_HEADS, 16, tm), lambda b, i: (b, 0, 0, i)),
    )
    return pl.pallas_call(
        _proj_kernel,
        grid=(B, S // tm),
        in_specs=[pl.BlockSpec((1, tm, D), lambda b, i: (b, i, 0)),
                  const(whg.shape), const(wqT.shape), const(wkn.shape), const(wvT.shape), const(wgT.shape)],
        out_specs=out_specs,
        out_shape=out_shape,
        compiler_params=_cparams(("parallel", "arbitrary")),
        name="in_proj",
    )(x, whg, wqT, wkn, wvT, wgT)


def _split3(a):
    hi = a.astype(BF16)
    r = a - hi.astype(F32)
    mid = r.astype(BF16)
    lo = (r - mid.astype(F32)).astype(BF16)
    return hi, mid, lo


def _hgrn_kernel(layer, hg_ref, lb_ref, ng_ref, o_ref, st_ref):
    C, SB, H, DK = HG_CHUNK, HG_SUB, HG_HEADS, HG_DK
    nsb = C // SB

    @pl.when(pl.program_id(1) == 0)
    def _():
        st_ref[...] = jnp.zeros_like(st_ref)

    lbp = lb_ref[...]
    e = jnp.exp(lbp - jnp.max(lbp, axis=0, keepdims=True))
    lb = jnp.sum(e[:layer + 1], axis=0, keepdims=True) / jnp.sum(e, axis=0, keepdims=True)

    ri = lax.broadcasted_iota(jnp.int32, (C, C), 0)
    ci = lax.broadcasted_iota(jnp.int32, (C, C), 1)
    tril = (ci <= ri).astype(BF16)
    t_sub = lax.broadcasted_iota(jnp.int32, (SB, H * DK), 0)
    lane16 = lax.broadcasted_iota(jnp.int32, (SB, SB), 1)
    ng = ng_ref[...]

    def chunk(c, carry):
        r0 = pl.multiple_of(c * C, C)
        blk = hg_ref[0, pl.ds(r0, C), :]
        q = blk[:, 0:512]
        f = lb + (1.0 - lb) * jax.nn.sigmoid(blk[:, 512:1024])
        lf = jnp.log(f)
        k = 1.0 - f
        v = blk[:, 1024:1536]
        gate = blk[:, 1536:2048]
        hi, mid, lo = _split3(lf)
        b = _nn(tril, hi) + _nn(tril, mid) + _nn(tril, lo)
        b_last = b[C - 1:C, :]
        qe = (q * jnp.exp(b)).astype(BF16)
        ks = (k * jnp.exp(b_last - b)).astype(BF16)
        dec = jnp.exp(b_last)
        vb = v.astype(BF16)
        rblk = jnp.concatenate([jnp.broadcast_to(b[i * SB:i * SB + 1, :], (SB, H * DK)) for i in range(nsb)], axis=0)
        qn = (q * jnp.exp(b - rblk)).astype(BF16)
        outs = []
        for h in range(H):
            sl = slice(h * DK, (h + 1) * DK)
            stT = st_ref[h]
            o_h = _nt(qe[:, sl], stT.astype(BF16))
            rows = []
            for i in range(nsb):
                rs = slice(i * SB, (i + 1) * SB)
                acc = jnp.zeros((SB, HG_DV), F32)
                if i > 0:
                    ke = (k[0:i * SB, sl] * jnp.exp(b[i * SB:i * SB + 1, sl] - b[0:i * SB, sl])).astype(BF16)
                    a_off = _nt(qn[rs, sl], ke)
                    acc = acc + _nn(a_off.astype(BF16), vb[0:i * SB, sl])
                qi, ki, bi = q[rs, sl], k[rs, sl], b[rs, sl]
                a_d = jnp.zeros((SB, SB), F32)
                for s in range(SB):
                    w = qi * ki[s:s + 1, :] * jnp.exp(jnp.minimum(bi - bi[s:s + 1, :], 0.0))
                    col = jnp.sum(w, axis=1, keepdims=True)
                    a_d = a_d + jnp.where((lane16 == s) & (t_sub[:, 0:SB] >= s), col, 0.0)
                acc = acc + _nn(a_d.astype(BF16), vb[rs, sl])
                rows.append(acc)
            o_h = o_h + jnp.concatenate(rows, axis=0)
            st_ref[h] = dec[:, sl] * stT + _tn(vb[:, sl], ks[:, sl])
            o_h = o_h * lax.rsqrt(jnp.mean(o_h * o_h, axis=1, keepdims=True) + LN_EPS) * ng
            outs.append(o_h)
        o = jnp.concatenate(outs, axis=1) * jax.nn.silu(gate)
        o_ref[0, pl.ds(r0, C), :] = o.astype(BF16)
        return carry

    lax.fori_loop(0, hg_ref.shape[1] // C, chunk, 0)


def _hgrn2(hg, hg_lb, norm_g_l, layer):
    B, S, _ = hg.shape
    tc = HG_TC
    return pl.pallas_call(
        functools.partial(_hgrn_kernel, layer),
        grid=(B, S // tc),
        in_specs=[pl.BlockSpec((1, tc, 2048), lambda b, i: (b, i, 0)),
                  pl.BlockSpec(hg_lb.shape, lambda b, i: (0, 0)),
                  pl.BlockSpec((1, HG_DV), lambda b, i: (0, 0))],
        out_specs=pl.BlockSpec((1, tc, HG_WIDTH), lambda b, i: (b, i, 0)),
        out_shape=jax.ShapeDtypeStruct((B, S, HG_WIDTH), BF16),
        scratch_shapes=[pltpu.VMEM((HG_HEADS, HG_DV, HG_DK), F32)],
        compiler_params=_cparams(("parallel", "arbitrary")),
        name="hgrn2",
    )(hg, hg_lb.astype(F32), norm_g_l.reshape(1, HG_DV).astype(F32))


def _compress_kernel(c_ref, pe_ref, w1_ref, b1_ref, w2_ref, w2T_ref, b2_ref, b2T_ref, o_ref, oT_ref):
    half = CMP_STRIDE * NSA_HD
    c = c_ref[0, 0, 0].astype(F32)
    pe = pe_ref[0]
    ca = (c + pe[:, 0:half]).astype(BF16)
    cb = (c + pe[:, half:2 * half]).astype(BF16)
    pa = _nn(ca, w1_ref[0, 0:half, :])
    pb = _nn(cb, w1_ref[0, half:2 * half, :])
    nbp = pa.shape[0]
    h = pa + pltpu.roll(pb, nbp - 1, 0) + b1_ref[0]
    h = jax.nn.gelu(h).astype(BF16)
    o_ref[0, 0, 0] = (_nn(h, w2_ref[0]) + b2_ref[0]).astype(BF16)
    oT_ref[0, 0, 0] = (_nt(w2T_ref[0], h) + b2T_ref[0]).astype(BF16)


def _compress(kn, cmpk, cmpv):
    B, _, G, S, HD = kn.shape
    nbp = S // CMP_STRIDE
    c = kn[:, 0:2].reshape(B, 2, G, nbp, CMP_STRIDE * HD)
    stack = lambda a, b, f: jnp.stack([f(a), f(b)], axis=0)
    pe = stack(cmpk[0], cmpv[0], lambda t: t.reshape(1, CMP_LEN * HD).astype(F32))
    w1 = stack(cmpk[1], cmpv[1], lambda t: t.astype(BF16))
    b1 = stack(cmpk[2], cmpv[2], lambda t: t.reshape(1, CMP_HIDDEN).astype(F32))
    w2 = stack(cmpk[3], cmpv[3], lambda t: t.astype(BF16))
    w2T = stack(cmpk[3], cmpv[3], lambda t: t.astype(BF16).T)
    b2 = stack(cmpk[4], cmpv[4], lambda t: t.reshape(1, HD).astype(F32))
    b2T = stack(cmpk[4], cmpv[4], lambda t: t.reshape(HD, 1).astype(F32))
    wspec = lambda a: pl.BlockSpec((1,) + a.shape[1:], lambda b, t, g: (t,) + (0,) * (a.ndim - 1))
    return pl.pallas_call(
        _compress_kernel,
        grid=(B, 2, G),
        in_specs=[pl.BlockSpec((1, 1, 1, nbp, CMP_STRIDE * HD), lambda b, t, g: (b, t, g, 0, 0)),
                  wspec(pe), wspec(w1), wspec(b1), wspec(w2), wspec(w2T), wspec(b2), wspec(b2T)],
        out_specs=(pl.BlockSpec((1, 1, 1, nbp, HD), lambda b, t, g: (b, t, g, 0, 0)),
                   pl.BlockSpec((1, 1, 1, HD, nbp), lambda b, t, g: (b, t, g, 0, 0))),
        out_shape=(jax.ShapeDtypeStruct((B, 2, G, nbp, HD), BF16),
                   jax.ShapeDtypeStruct((B, 2, G, HD, nbp), BF16)),
        compiler_params=_cparams(("parallel", "parallel", "parallel")),
        name="kv_compress",
    )(c, pe, w1, b1, w2, w2T, b2, b2T)


def _topk_mask_rows(score, k):
    R = score.shape[0]
    r_io = lax.broadcasted_iota(jnp.int32, score.shape, 0)
    s = score
    for _ in range(k):
        m = jnp.max(s, axis=0, keepdims=True)
        idx = jnp.min(jnp.where(s == m, r_io, R), axis=0, keepdims=True)
        s = jnp.where(r_io == idx, -jnp.inf, s)
    return (s == -jnp.inf).astype(F32)


def _with_ones_row(vT):
    pad = (lax.broadcasted_iota(jnp.int32, (16, vT.shape[1]), 0) == 0).astype(BF16)
    return jnp.concatenate([vT, pad], axis=0)


def _nsa_kernel(q_off, qT_ref, kc_ref, vcT_ref, ks_ref, vsT_ref, kw_ref, vwT_ref, gT_ref, ovT_ref,
                o_ref, sel_ref):
    TQ, G, HD, KT = ATT_TQ, NSA_GROUP, NSA_HD, ATT_KT
    qi = pl.program_id(2) + q_off
    t0 = qi * TQ
    qT = qT_ref[0]
    q4 = jnp.concatenate([qT[r * HD:(r + 1) * HD, :] for r in range(G)], axis=1)
    pos1 = t0 + lax.broadcasted_iota(jnp.int32, (1, TQ), 1)
    pos4 = jnp.concatenate([pos1] * G, axis=1)

    nbp = kc_ref.shape[3]
    s_c = _nn(kc_ref[0, 0, 0], q4)
    n_io = lax.broadcasted_iota(jnp.int32, (nbp, 1), 0)
    mask_c = (n_io * CMP_STRIDE + (CMP_LEN - 1)) <= pos4
    s_c = jnp.where(mask_c, s_c, NEG)
    p_c = jnp.where(mask_c, jnp.exp2(s_c - jnp.max(s_c, axis=0, keepdims=True)), 0.0)
    p_c = p_c / jnp.maximum(jnp.sum(p_c, axis=0, keepdims=True), 1e-30)
    o_c = _nn(vcT_ref[0, 0, 0], p_c.astype(BF16))
    p_sum = p_c[:, 0:TQ]
    for r in range(1, G):
        p_sum = p_sum + p_c[:, r * TQ:(r + 1) * TQ]
    imp = _nn(ovT_ref[...], p_sum.astype(BF16))

    ns = imp.shape[0]
    j_io = lax.broadcasted_iota(jnp.int32, (ns, 1), 0)
    cur = pos1 >> int(math.log2(SLC_LEN))
    forced = (j_io == 0) | (j_io == cur) | (j_io == cur - 1)
    score = jnp.where(forced, FORCE_SCORE, jnp.where(j_io * SLC_LEN <= pos1, imp, -1.0))
    sel_ref[...] = (_topk_mask_rows(score, min(SLC_TOPK, ns)) - 1.0) * (-NEG)

    KS = ATT_KS
    bps = KS // SLC_LEN
    k_io = lax.broadcasted_iota(jnp.int32, (KS, 1), 0)
    zrows = jnp.zeros((HD - bps, G * TQ), BF16)

    def sel_step(ks_i, carry, diagonal):
        m, acc = carry
        k0 = pl.multiple_of(ks_i * KS, KS)
        bias = sel_ref[pl.ds(pl.multiple_of(ks_i * bps, bps), bps), :].astype(BF16)
        rhs = jnp.concatenate([q4, jnp.concatenate([bias] * G, axis=1), zrows], axis=0)
        s = _nn(ks_ref[0, 0, pl.ds(k0, KS), :], rhs)
        if diagonal:
            s = jnp.where((k0 + k_io) <= pos4, s, NEG)
        m_new = jnp.maximum(m, jnp.max(s, axis=0, keepdims=True))
        p = jnp.exp2(s - m_new).astype(BF16)
        acc = jnp.exp2(m - m_new) * acc
        for c in range(KS // KT):
            acc = acc + _nn(_with_ones_row(vsT_ref[0, 0, ks_i * (KS // KT) + c]), p[c * KT:(c + 1) * KT])
        return m_new, acc

    n_ks = (t0 + TQ - 1) // KS + 1
    init = (jnp.full((1, G * TQ), NEG, F32), jnp.zeros((HD + 16, G * TQ), F32))
    carry = lax.fori_loop(0, n_ks - 1, lambda i, c: sel_step(i, c, False), init)
    _, acc_s = sel_step(n_ks - 1, carry, True)
    o_s = acc_s[0:HD] / jnp.maximum(acc_s[HD:HD + 1], 1e-30)

    nwt = WIN // TQ + 1
    kw_t, vw_t, kp_t = [], [], []
    r_io = lax.broadcasted_iota(jnp.int32, (TQ, 1), 0)
    for i in range(nwt):
        idx = qi - (nwt - 1) + i
        idc = jnp.maximum(idx, 0)
        kw_t.append(kw_ref[0, 0, 0, pl.ds(pl.multiple_of(idc * TQ, TQ), TQ), :])
        vw_t.append(vwT_ref[0, 0, idc])
        kp_t.append(idx * TQ + r_io)
    kpos = jnp.concatenate(kp_t, axis=0)
    s_w = _nn(jnp.concatenate(kw_t, axis=0), q4)
    delta = pos4 - kpos
    s_w = jnp.where((delta >= 0) & (delta < WIN) & (kpos >= 0), s_w, NEG)
    p_w = jnp.exp2(s_w - jnp.max(s_w, axis=0, keepdims=True)).astype(BF16)
    acc_w = _nn(_with_ones_row(jnp.concatenate(vw_t, axis=1)), p_w)
    o_w = acc_w[0:HD] / jnp.maximum(acc_w[HD:HD + 1], 1e-30)

    gs = jax.nn.sigmoid(gT_ref[0, 0])
    gate = lambda br: jnp.concatenate([gs[r * 3 + br:r * 3 + br + 1, :] for r in range(G)], axis=1)
    o = gate(0) * o_c + gate(1) * o_s + gate(2) * o_w
    o_ref[0] = jnp.concatenate([o[:, r * TQ:(r + 1) * TQ].T for r in range(G)], axis=1).astype(BF16)


def _overlap_T(S):
    nbp = S // CMP_STRIDE
    n_slc = S // SLC_LEN
    cs = np.arange(nbp) * CMP_STRIDE
    ss = np.arange(n_slc) * SLC_LEN
    ov = np.clip(np.minimum(cs[:, None] + CMP_LEN, ss[None, :] + SLC_LEN)
                 - np.maximum(cs[:, None], ss[None, :]), 0, None) / CMP_LEN
    ov[nbp - 1:, :] = 0.0
    return jnp.asarray(ov.T, BF16)


def _nsa(qT, cmp_n, cmp_T, kn, ksE, vsT, vwT, gT, q_lo, q_len):
    B, _, S = qT.shape
    G2, TQ, HD = NSA_KV_HEADS, ATT_TQ, NSA_HD
    nbp = S // CMP_STRIDE
    ns = S // SLC_LEN
    ovT = _overlap_T(S)
    q_off = q_lo // TQ
    return pl.pallas_call(
        functools.partial(_nsa_kernel, q_off),
        grid=(B, G2, q_len // TQ),
        in_specs=[
            pl.BlockSpec((1, NSA_GROUP * HD, TQ), lambda b, g, i: (b, g, i + q_off)),
            pl.BlockSpec((1, 1, 1, nbp, HD), lambda b, g, i: (b, 0, g, 0, 0)),
            pl.BlockSpec((1, 1, 1, HD, nbp), lambda b, g, i: (b, 1, g, 0, 0)),
            pl.BlockSpec((1, 1, S, 2 * HD), lambda b, g, i: (b, g, 0, 0)),
            pl.BlockSpec((1, 1, S // ATT_KT, HD, ATT_KT), lambda b, g, i: (b, g, 0, 0, 0)),
            pl.BlockSpec((1, 1, 1, S, HD), lambda b, g, i: (b, 2, g, 0, 0)),
            pl.BlockSpec((1, 1, S // LANES, HD, LANES), lambda b, g, i: (b, g, 0, 0, 0)),
            pl.BlockSpec((1, 1, 16, TQ), lambda b, g, i: (b, g, 0, i + q_off)),
            pl.BlockSpec((ns, nbp), lambda b, g, i: (0, 0)),
        ],
        out_specs=pl.BlockSpec((1, TQ, NSA_GROUP * HD), lambda b, g, i: (b, i, g)),
        out_shape=jax.ShapeDtypeStruct((B, q_len, NSA_WIDTH), BF16),
        scratch_shapes=[pltpu.VMEM((ns, TQ), F32)],
        compiler_params=_cparams(("parallel", "parallel", "arbitrary")),
        name="nsa_attn",
    )(qT, cmp_n, cmp_T, ksE, vsT, kn, vwT, gT, ovT)


def _layernorm(t, g, b):
    mu = jnp.mean(t, axis=-1, keepdims=True)
    d = t - mu
    var = jnp.mean(d * d, axis=-1, keepdims=True)
    return d * lax.rsqrt(var + LN_EPS) * g + b


def _outproj_kernel(alpha, ohg_ref, onsa_ref, x_ref, w_ref, g_ref, b_ref, h_ref):
    mix = _nn(ohg_ref[...], w_ref[0:HG_WIDTH, :]) + _nn(onsa_ref[...], w_ref[HG_WIDTH:HG_WIDTH + NSA_WIDTH, :])
    h_ref[...] = _layernorm(alpha * x_ref[...] + mix, g_ref[...], b_ref[...])


def _outproj(alpha, o_hg, o_nsa, x2, w_out_l, g, b):
    n, D = x2.shape
    tm = OUT_TM
    row = lambda w: pl.BlockSpec((tm, w), lambda i: (i, 0))
    const = lambda shp: pl.BlockSpec(shp, lambda i: (0, 0))
    return pl.pallas_call(
        functools.partial(_outproj_kernel, alpha),
        grid=(n // tm,),
        in_specs=[row(HG_WIDTH), row(NSA_WIDTH), row(D), const(w_out_l.shape), const((1, D)), const((1, D))],
        out_specs=row(D),
        out_shape=jax.ShapeDtypeStruct((n, D), F32),
        compiler_params=_cparams(("parallel",)),
        name="out_proj_ln",
    )(o_hg, o_nsa, x2, w_out_l.astype(BF16), g.reshape(1, D).astype(F32), b.reshape(1, D).astype(F32))


def _topk_rows(s, k):
    R = s.shape[0]
    r_io = lax.broadcasted_iota(jnp.int32, s.shape, 0)
    vals, idxs = [], []
    for _ in range(k):
        m = jnp.max(s, axis=0, keepdims=True)
        idx = jnp.min(jnp.where(s == m, r_io, R), axis=0, keepdims=True)
        vals.append(m)
        idxs.append(idx)
        s = jnp.where(r_io == idx, -jnp.inf, s)
    return jnp.concatenate(vals, axis=0), jnp.concatenate(idxs, axis=0)


def _route_kernel(h_ref, wq_ref, k1_ref, k2_ref, e_ref, g_ref):
    K, half = PEER_TOPK, PEER_DQ // 2
    q = _nn(h_ref[...].astype(BF16), wq_ref[...]).astype(BF16)
    for h in range(PEER_HEADS):
        s1 = _nt(k1_ref[...], q[:, h * PEER_DQ:h * PEER_DQ + half])
        s2 = _nt(k2_ref[...], q[:, h * PEER_DQ + half:(h + 1) * PEER_DQ])
        v1, i1 = _topk_rows(s1, K)
        v2, i2 = _topk_rows(s2, K)
        sub = 8
        b_io = lax.broadcasted_iota(jnp.int32, (sub, 1), 0)
        cands = [v1[0:1, :] + v2]
        ids = [i1[0:1, :] * PEER_NKEYS + i2]
        for a in range(1, sub):
            cands.append(jnp.where(b_io < K // (a + 1), v1[a:a + 1, :] + v2[0:sub, :], -jnp.inf))
            ids.append(i1[a:a + 1, :] * PEER_NKEYS + i2[0:sub, :])
        cands.append(v1[sub:K, :] + v2[0:1, :])
        ids.append(i1[sub:K, :] * PEER_NKEYS + i2[0:1, :])
        cand = jnp.concatenate(cands, axis=0)
        cand_id = jnp.concatenate(ids, axis=0)
        r_io = lax.broadcasted_iota(jnp.int32, cand.shape, 0)
        top_s, top_e = [], []
        for _ in range(K):
            m = jnp.max(cand, axis=0, keepdims=True)
            hit = r_io == jnp.min(jnp.where(cand == m, r_io, cand.shape[0]), axis=0, keepdims=True)
            top_s.append(m)
            top_e.append(jnp.sum(jnp.where(hit, cand_id, 0), axis=0, keepdims=True))
            cand = jnp.where(hit, -jnp.inf, cand)
        top_s = jnp.concatenate(top_s, axis=0)
        e_ref[h * K:(h + 1) * K, :] = jnp.concatenate(top_e, axis=0)
        ex = jnp.exp(top_s - top_s[0:1, :])
        g_ref[h * K:(h + 1) * K, :] = ex / jnp.sum(ex, axis=0, keepdims=True)


def _route(h1, wq, k1, k2):
    n, D = h1.shape
    tr = ROUTE_TR
    hk = PEER_HEADS * PEER_TOPK
    const = lambda shp: pl.BlockSpec(shp, lambda i: (0, 0))
    return pl.pallas_call(
        _route_kernel,
        grid=(n // tr,),
        in_specs=[pl.BlockSpec((tr, D), lambda i: (i, 0)), const(wq.shape), const(k1.shape), const(k2.shape)],
        out_specs=(pl.BlockSpec((hk, tr), lambda i: (0, i)), pl.BlockSpec((hk, tr), lambda i: (0, i))),
        out_shape=(jax.ShapeDtypeStruct((hk, n), jnp.int32), jax.ShapeDtypeStruct((hk, n), F32)),
        compiler_params=_cparams(("parallel",)),
        name="peer_route",
    )(h1, wq.astype(BF16), k1.astype(BF16), k2.astype(BF16))


def _pack_rows(t):
    half = t.shape[1] // 2
    tb = t.astype(BF16)
    lo = lax.bitcast_convert_type(tb[:, :half], jnp.uint16).astype(jnp.uint32)
    hi = lax.bitcast_convert_type(tb[:, half:], jnp.uint16).astype(jnp.uint32)
    return lo | (hi << 16)


def _sc_peer(utab, vtab, idx, x):
    n = x.shape[0]
    w = utab.shape[1]
    hk = PEER_HEADS * PEER_TOPK
    L = SC_LANES
    T = n // SC_WORKERS
    uwin = SC_UWIN
    upt = hk // uwin
    vpt = hk // SC_VWIN
    per_u = vpt // upt
    vlead = SC_VSLOT // 2
    assert n % SC_WORKERS == 0 and T % SC_XB == 0 and uwin % L == 0
    assert upt % SC_USLOT == 0 and vpt % SC_VSLOT == 0 and vpt % upt == 0
    vsteps = T * vpt
    mesh = plsc.VectorSubcoreMesh(core_axis_name="c", subcore_axis_name="s")
    cp = dataclasses.replace(pltpu.CompilerParams(), needs_layout_passes=False)

    @pl.kernel(out_type=(jax.ShapeDtypeStruct((n * hk,), F32), jax.ShapeDtypeStruct((n * hk, w), utab.dtype)),
               mesh=mesh, compiler_params=cp,
               scratch_types=[pltpu.VMEM((T * hk,), jnp.int32),
                              pltpu.VMEM((SC_XB, 2 * w), F32),
                              pltpu.VMEM((SC_USLOT, uwin, w), utab.dtype),
                              pltpu.VMEM((SC_VSLOT, SC_VWIN, w), vtab.dtype),
                              pltpu.VMEM((L, L), F32),
                              pltpu.VMEM((SC_XB * hk,), F32),
                              pltpu.SemaphoreType.DMA((SC_USLOT,)),
                              pltpu.SemaphoreType.DMA((SC_VSLOT,)),
                              pltpu.SemaphoreType.DMA((SC_VSLOT,))])
    def peer(u_hbm, v_hbm, i_hbm, x_hbm, h_hbm, vg_hbm, idx_v, x_v, ubuf, vbuf, acc_s, h_v, usem, vsem, wsem):
        wid = lax.axis_index("c") * (SC_WORKERS // 2) + lax.axis_index("s")
        tok0 = wid * T
        row0 = tok0 * hk
        pltpu.sync_copy(i_hbm.at[pl.ds(row0, T * hk)], idx_v)

        def ufetch(t, j):
            slot = j % SC_USLOT
            return pltpu.make_async_copy(u_hbm.at[idx_v.at[pl.ds(t * hk + j * uwin, uwin)]], ubuf.at[slot], usem.at[slot])

        def vfetch(s, slot):
            return pltpu.make_async_copy(v_hbm.at[idx_v.at[pl.ds(s * SC_VWIN, SC_VWIN)]], vbuf.at[slot], vsem.at[slot])

        def vflush(s, slot):
            return pltpu.make_async_copy(vbuf.at[slot], vg_hbm.at[pl.ds(row0 + s * SC_VWIN, SC_VWIN)], wsem.at[slot])

        for j in range(SC_USLOT):
            ufetch(0, j).start()
        for j in range(vlead):
            vfetch(j, j).start()

        def vstep(s, slot):
            other = (slot + vlead) % SC_VSLOT
            vfetch(s, slot).wait()
            vflush(s, slot).start()

            @pl.when(s >= vlead)
            def _():
                vflush(s - vlead, other).wait()

            @pl.when(s + vlead < vsteps)
            def _():
                vfetch(s + vlead, other).start()

        lane = lax.iota(jnp.int32, L)

        @pl.loop(0, T // SC_XB)
        def _(tb):
            pltpu.sync_copy(x_hbm.at[pl.ds(pl.multiple_of(tok0 + tb * SC_XB, SC_XB), SC_XB)], x_v)

            @pl.loop(0, SC_XB)
            def _(tt):
                t = tb * SC_XB + tt
                for j in range(upt):
                    for i in range(per_u):
                        vstep(t * vpt + j * per_u + i, (j * per_u + i) % SC_VSLOT)
                    ufetch(t, j).wait()

                    def dot_body(jj, accs):
                        off = pl.multiple_of(jj * L, L)
                        xlo = x_v[tt, pl.ds(off, L)]
                        xhi = x_v[tt, pl.ds(w + off, L)]
                        out = []
                        for r in range(uwin):
                            wv = ubuf[j % SC_USLOT, r, pl.ds(off, L)]
                            lo = plsc.bitcast(wv << 16, F32)
                            hi = plsc.bitcast(wv & jnp.uint32(0xFFFF0000), F32)
                            out.append(accs[r] + lo * xlo + hi * xhi)
                        return tuple(out)

                    accs = lax.fori_loop(0, w // L, dot_body, tuple(jnp.zeros((L,), F32) for _ in range(uwin)))

                    if j + SC_USLOT < upt:
                        ufetch(t, j + SC_USLOT).start()
                    else:
                        @pl.when(t + 1 < T)
                        def _():
                            ufetch(t + 1, j + SC_USLOT - upt).start()

                    for g in range(uwin // L):
                        for r in range(L):
                            acc_s[r, :] = accs[g * L + r]
                        tot = jnp.zeros((L,), F32)
                        for c in range(L):
                            tot = tot + plsc.load_gather(acc_s, [lane, jnp.full((L,), c, jnp.int32)])
                        h_v[pl.ds(tt * hk + j * uwin + g * L, L)] = tot

            pltpu.sync_copy(h_v, h_hbm.at[pl.ds(pl.multiple_of(row0 + tb * SC_XB * hk, SC_XB * hk), SC_XB * hk)])

        for s in range(vsteps - vlead, vsteps):
            vflush(s, s % SC_VSLOT).wait()

    return peer(utab, vtab, idx, x)


def _unpack(w):
    lo = lax.bitcast_convert_type(w << 16, F32)
    hi = lax.bitcast_convert_type(w & jnp.uint32(0xFFFF0000), F32)
    return lo, hi


def _mix_kernel(alpha, h_ref, s_ref, gate_ref, vg_ref, g_ref, b_ref, o_ref):
    TT, HK = MIX_TT, PEER_HEADS * PEER_TOPK
    x = h_ref[...]
    wrow = gate_ref[...] * jax.nn.gelu(s_ref[...])
    eye = lax.broadcasted_iota(jnp.int32, (HK, HK), 0) == lax.broadcasted_iota(jnp.int32, (HK, HK), 1)
    rows = []
    for t in range(TT):
        wcol = jnp.sum(jnp.where(eye, jnp.broadcast_to(wrow[t:t + 1, :], (HK, HK)), 0.0),
                       axis=1, keepdims=True)
        vlo, vhi = _unpack(vg_ref[t * HK:(t + 1) * HK, :])
        rows.append(jnp.concatenate([jnp.sum(wcol * vlo, axis=0, keepdims=True),
                                     jnp.sum(wcol * vhi, axis=0, keepdims=True)], axis=1))
    ffn = jnp.concatenate(rows, axis=0)
    o_ref[...] = _layernorm(alpha * x + ffn, g_ref[...], b_ref[...])


def _mix(alpha, h1, scores, gates, vg, g, b):
    n, D = h1.shape
    tt = MIX_TT
    hk = PEER_HEADS * PEER_TOPK
    const = lambda shp: pl.BlockSpec(shp, lambda i: (0, 0))
    return pl.pallas_call(
        functools.partial(_mix_kernel, alpha),
        grid=(n // tt,),
        in_specs=[pl.BlockSpec((tt, D), lambda i: (i, 0)), pl.BlockSpec((tt, hk), lambda i: (i, 0)),
                  pl.BlockSpec((tt, hk), lambda i: (i, 0)), pl.BlockSpec((tt * hk, D // 2), lambda i: (i, 0)),
                  const((1, D)), const((1, D))],
        out_specs=pl.BlockSpec((tt, D), lambda i: (i, 0)),
        out_shape=jax.ShapeDtypeStruct((n, D), F32),
        compiler_params=_cparams(("parallel",)),
        name="peer_mix_ln",
    )(h1, scores, gates, vg, g.reshape(1, D).astype(F32), b.reshape(1, D).astype(F32))


PEER_CHUNK = 4096
SEQ_PIECE = 4096


def _peer_ffn_ln(alpha, h1, wq, k1, k2, up, vp, g, b):
    n, D = h1.shape
    hk = PEER_HEADS * PEER_TOPK
    eT, gT = _route(h1, wq, k1, k2)
    experts = eT.T
    gates = gT.T
    ch = min(PEER_CHUNK, n)
    outs = []
    for c in range(n // ch):
        hc = h1[c * ch:(c + 1) * ch]
        scores, vg = _sc_peer(up, vp, experts[c * ch:(c + 1) * ch].reshape(ch * hk), hc)
        outs.append(_mix(alpha, hc, scores.reshape(ch, hk), gates[c * ch:(c + 1) * ch], vg, g, b))
    return jnp.concatenate(outs, axis=0)


def kernel(x, w_in, w_out, hg_lb, hg_norm_g, cmpk_pe, cmpk_w1, cmpk_b1, cmpk_w2, cmpk_b2,
           cmpv_pe, cmpv_w1, cmpv_b1, cmpv_w2, cmpv_b2, ln1_g, ln1_b,
           peer_wq, peer_k1, peer_k2, peer_u, peer_v, ln2_g, ln2_b):
    B, S, D = x.shape
    depth = w_in.shape[0]
    alpha = (2.0 * depth) ** 0.25
    h = x
    for l in range(depth):
        up, vp = _pack_rows(peer_u[l]), _pack_rows(peer_v[l])
        cmpk = (cmpk_pe[l], cmpk_w1[l], cmpk_b1[l], cmpk_w2[l], cmpk_b2[l])
        cmpv = (cmpv_pe[l], cmpv_w1[l], cmpv_b1[l], cmpv_w2[l], cmpv_b2[l])
        outs = []
        for bi in range(B):
            hb = h[bi:bi + 1]
            hg, qT, kn, ksE, vsT, vwT, gT = _project(hb, w_in[l])
            o_hg = _hgrn2(hg, hg_lb, hg_norm_g[l], l)
            cmp_n, cmp_T = _compress(kn, cmpk, cmpv)
            piece = min(SEQ_PIECE, S)
            for lo in range(0, S, piece):
                o_nsa = _nsa(qT, cmp_n, cmp_T, kn, ksE, vsT, vwT, gT, lo, piece)
                h1 = _outproj(alpha, o_hg[0, lo:lo + piece], o_nsa.reshape(piece, NSA_WIDTH),
                              hb[0, lo:lo + piece], w_out[l], ln1_g[l], ln1_b[l])
                outs.append(_peer_ffn_ln(alpha, h1, peer_wq[l], peer_k1[l], peer_k2[l], up, vp, ln2_g[l], ln2_b[l]))
        h = jnp.concatenate(outs, axis=0).reshape(B, S, D)
    return h
```

```python
import dataclasses
import functools
import math

import jax
import jax.numpy as jnp
import numpy as np
from jax import lax
from jax.experimental import pallas as pl
from jax.experimental.pallas import tpu as pltpu
from jax.experimental.pallas import tpu_sc as plsc

F32 = jnp.float32
BF16 = jnp.bfloat16

HG_HEADS = 4
HG_DK = 128
HG_DV = 128
HG_CHUNK = 64
HG_SUB = 16
HG_WIDTH = HG_HEADS * HG_DV
NSA_HEADS = 8
NSA_KV_HEADS = 2
NSA_GROUP = NSA_HEADS // NSA_KV_HEADS
NSA_HD = 64
NSA_WIDTH = NSA_HEADS * NSA_HD
CMP_LEN = 32
CMP_STRIDE = 16
CMP_HIDDEN = 256
SLC_LEN = 64
SLC_TOPK = 16
WIN = 512
FORCE_SCORE = 1.0e4
PEER_HEADS = 8
PEER_DQ = 256
PEER_NKEYS = 128
PEER_TOPK = 16
LN_EPS = 1e-5
NEG = -1e30
LOG2E = math.log2(math.e)

LANES = 128
PROJ_TM = 512
HG_TC = 256
ATT_TQ = 128
ATT_KT = 512
ATT_KS = 1024
OUT_TM = 512
ROUTE_TR = 256
MIX_TT = 16
SC_WORKERS = 32
SC_LANES = 16
SC_UWIN = 16
SC_USLOT = 4
SC_VWIN = 16
SC_VSLOT = 8
SC_XB = 8
VMEM_LIMIT = 56 * 1024 * 1024


def _cparams(sem):
    return pltpu.CompilerParams(dimension_semantics=sem, vmem_limit_bytes=VMEM_LIMIT)


def _nt(a, b):
    return lax.dot_general(a, b, (((1,), (1,)), ((), ())), preferred_element_type=F32)


def _tn(a, b):
    return lax.dot_general(a, b, (((0,), (0,)), ((), ())), preferred_element_type=F32)


def _nn(a, b):
    return jnp.dot(a, b, preferred_element_type=F32)


def _proj_kernel(x_ref, whg_ref, wqT_ref, wkn_ref, wvT_ref, wgT_ref,
                 hg_ref, qT_ref, kn_ref, ksE_ref, vsT_ref, vwT_ref, gT_ref):
    xb = x_ref[0].astype(BF16)
    hg_ref[0] = _nn(xb, whg_ref[...])
    qT_ref[0] = (_nt(wqT_ref[...], xb) * (NSA_HD ** -0.5 * LOG2E)).astype(BF16)
    kn = _nn(xb, wkn_ref[...])
    for j in range(3):
        for g in range(NSA_KV_HEADS):
            lo = j * 2 * NSA_HD + g * NSA_HD
            kn_ref[0, j, g] = kn[:, lo:lo + NSA_HD].astype(BF16)
    row = pl.program_id(1) * kn.shape[0] + lax.broadcasted_iota(jnp.int32, (kn.shape[0], NSA_HD), 0)
    blk = (row >> int(math.log2(SLC_LEN))) & (ATT_KS // SLC_LEN - 1)
    onehot = (lax.broadcasted_iota(jnp.int32, (kn.shape[0], NSA_HD), 1) == blk).astype(F32)
    for g in range(NSA_KV_HEADS):
        lo = 3 * 2 * NSA_HD + g * NSA_HD
        ksE_ref[0, g] = jnp.concatenate([kn[:, lo:lo + NSA_HD], onehot], axis=1).astype(BF16)
    vT = _nt(wvT_ref[...], xb).astype(BF16)
    tm = vT.shape[1]
    for g in range(NSA_KV_HEADS):
        for c in range(tm // ATT_KT):
            vsT_ref[0, g, c] = vT[g * NSA_HD:(g + 1) * NSA_HD, c * ATT_KT:(c + 1) * ATT_KT]
        for c in range(tm // LANES):
            vwT_ref[0, g, c] = vT[(2 + g) * NSA_HD:(3 + g) * NSA_HD, c * LANES:(c + 1) * LANES]
    gT = _nt(wgT_ref[...], xb)
    gT_ref[0, 0] = gT[0:16]
    gT_ref[0, 1] = gT[16:32]


def _project(x, w_in_l):
    B, S, D = x.shape
    tm = PROJ_TM
    o = np.cumsum((0, 512, 512, 512, 512, 512, 128, 128, 128, 128, 128, 128, 24))
    wb = w_in_l.astype(BF16)
    whg = wb[:, o[0]:o[4]]
    wqT = wb[:, o[4]:o[5]].T
    wkn = jnp.concatenate([wb[:, o[5]:o[6]], wb[:, o[6]:o[7]], wb[:, o[9]:o[10]], wb[:, o[7]:o[8]]], axis=1)
    wvT = jnp.concatenate([wb[:, o[8]:o[9]], wb[:, o[10]:o[11]]], axis=1).T
    wg = wb[:, o[11]:o[12]].reshape(D, NSA_KV_HEADS, NSA_GROUP * 3)
    wgT = jnp.pad(wg, ((0, 0), (0, 0), (0, 16 - NSA_GROUP * 3))).reshape(D, 32).T
    const = lambda shp: pl.BlockSpec(shp, lambda b, i: (0,) * len(shp))
    out_shape = (
        jax.ShapeDtypeStruct((B, S, 4 * 512), F32),
        jax.ShapeDtypeStruct((B, NSA_WIDTH, S), BF16),
        jax.ShapeDtypeStruct((B, 3, NSA_KV_HEADS, S, NSA_HD), BF16),
        jax.ShapeDtypeStruct((B, NSA_KV_HEADS, S, 2 * NSA_HD), BF16),
        jax.ShapeDtypeStruct((B, NSA_KV_HEADS, S // ATT_KT, NSA_HD, ATT_KT), BF16),
        jax.ShapeDtypeStruct((B, NSA_KV_HEADS, S // LANES, NSA_HD, LANES), BF16),
        jax.ShapeDtypeStruct((B, NSA_KV_HEADS, 16, S), F32),
    )
    out_specs = (
        pl.BlockSpec((1, tm, 2048), lambda b, i: (b, i, 0)),
        pl.BlockSpec((1, NSA_WIDTH, tm), lambda b, i: (b, 0, i)),
        pl.BlockSpec((1, 3, NSA_KV_HEADS, tm, NSA_HD), lambda b, i: (b, 0, 0, i, 0)),
        pl.BlockSpec((1, NSA_KV_HEADS, tm, 2 * NSA_HD), lambda b, i: (b, 0, i, 0)),
        pl.BlockSpec((1, NSA_KV_HEADS, tm // ATT_KT, NSA_HD, ATT_KT), lambda b, i: (b, 0, i, 0, 0)),
        pl.BlockSpec((1, NSA_KV_HEADS, tm // LANES, NSA_HD, LANES), lambda b, i: (b, 0, i, 0, 0)),
        pl.BlockSpec((1, NSA_KV_HEADS, 16, tm), lambda b, i: (b, 0, 0, i)),
    )
    return pl.pallas_call(
        _proj_kernel,
        grid=(B, S // tm),
        in_specs=[pl.BlockSpec((1, tm, D), lambda b, i: (b, i, 0)),
                  const(whg.shape), const(wqT.shape), const(wkn.shape), const(wvT.shape), const(wgT.shape)],
        out_specs=out_specs,
        out_shape=out_shape,
        compiler_params=_cparams(("parallel", "arbitrary")),
        name="in_proj",
    )(x, whg, wqT, wkn, wvT, wgT)


def _split3(a):
    hi = a.astype(BF16)
    r = a - hi.astype(F32)
    mid = r.astype(BF16)
    lo = (r - mid.astype(F32)).astype(BF16)
    return hi, mid, lo


def _hgrn_kernel(layer, hg_ref, lb_ref, ng_ref, o_ref, st_ref):
    C, SB, H, DK = HG_CHUNK, HG_SUB, HG_HEADS, HG_DK
    nsb = C // SB

    @pl.when(pl.program_id(1) == 0)
    def _():
        st_ref[...] = jnp.zeros_like(st_ref)

    lbp = lb_ref[...]
    e = jnp.exp(lbp - jnp.max(lbp, axis=0, keepdims=True))
    lb = jnp.sum(e[:layer + 1], axis=0, keepdims=True) / jnp.sum(e, axis=0, keepdims=True)

    ri = lax.broadcasted_iota(jnp.int32, (C, C), 0)
    ci = lax.broadcasted_iota(jnp.int32, (C, C), 1)
    tril = (ci <= ri).astype(BF16)
    t_sub = lax.broadcasted_iota(jnp.int32, (SB, H * DK), 0)
    lane16 = lax.broadcasted_iota(jnp.int32, (SB, SB), 1)
    ng = ng_ref[...]

    def chunk(c, carry):
        r0 = pl.multiple_of(c * C, C)
        blk = hg_ref[0, pl.ds(r0, C), :]
        q = blk[:, 0:512]
        f = lb + (1.0 - lb) * jax.nn.sigmoid(blk[:, 512:1024])
        lf = jnp.log(f)
        k = 1.0 - f
        v = blk[:, 1024:1536]
        gate = blk[:, 1536:2048]
        hi, mid, lo = _split3(lf)
        b = _nn(tril, hi) + _nn(tril, mid) + _nn(tril, lo)
        b_last = b[C - 1:C, :]
        qe = (q * jnp.exp(b)).astype(BF16)
        ks = (k * jnp.exp(b_last - b)).astype(BF16)
        dec = jnp.exp(b_last)
        vb = v.astype(BF16)
        rblk = jnp.concatenate([jnp.broadcast_to(b[i * SB:i * SB + 1, :], (SB, H * DK)) for i in range(nsb)], axis=0)
        qn = (q * jnp.exp(b - rblk)).astype(BF16)
        outs = []
        for h in range(H):
            sl = slice(h * DK, (h + 1) * DK)
            stT = st_ref[h]
            o_h = _nt(qe[:, sl], stT.astype(BF16))
            rows = []
            for i in range(nsb):
                rs = slice(i * SB, (i + 1) * SB)
                acc = jnp.zeros((SB, HG_DV), F32)
                if i > 0:
                    ke = (k[0:i * SB, sl] * jnp.exp(b[i * SB:i * SB + 1, sl] - b[0:i * SB, sl])).astype(BF16)
                    a_off = _nt(qn[rs, sl], ke)
                    acc = acc + _nn(a_off.astype(BF16), vb[0:i * SB, sl])
                qi, ki, bi = q[rs, sl], k[rs, sl], b[rs, sl]
                a_d = jnp.zeros((SB, SB), F32)
                for s in range(SB):
                    w = qi * ki[s:s + 1, :] * jnp.exp(jnp.minimum(bi - bi[s:s + 1, :], 0.0))
                    col = jnp.sum(w, axis=1, keepdims=True)
                    a_d = a_d + jnp.where((lane16 == s) & (t_sub[:, 0:SB] >= s), col, 0.0)
                acc = acc + _nn(a_d.astype(BF16), vb[rs, sl])
                rows.append(acc)
            o_h = o_h + jnp.concatenate(rows, axis=0)
            st_ref[h] = dec[:, sl] * stT + _tn(vb[:, sl], ks[:, sl])
            o_h = o_h * lax.rsqrt(jnp.mean(o_h * o_h, axis=1, keepdims=True) + LN_EPS) * ng
            outs.append(o_h)
        o = jnp.concatenate(outs, axis=1) * jax.nn.silu(gate)
        o_ref[0, pl.ds(r0, C), :] = o.astype(BF16)
        return carry

    lax.fori_loop(0, hg_ref.shape[1] // C, chunk, 0)


def _hgrn2(hg, hg_lb, norm_g_l, layer):
    B, S, _ = hg.shape
    tc = HG_TC
    return pl.pallas_call(
        functools.partial(_hgrn_kernel, layer),
        grid=(B, S // tc),
        in_specs=[pl.BlockSpec((1, tc, 2048), lambda b, i: (b, i, 0)),
                  pl.BlockSpec(hg_lb.shape, lambda b, i: (0, 0)),
                  pl.BlockSpec((1, HG_DV), lambda b, i: (0, 0))],
        out_specs=pl.BlockSpec((1, tc, HG_WIDTH), lambda b, i: (b, i, 0)),
        out_shape=jax.ShapeDtypeStruct((B, S, HG_WIDTH), BF16),
        scratch_shapes=[pltpu.VMEM((HG_HEADS, HG_DV, HG_DK), F32)],
        compiler_params=_cparams(("parallel", "arbitrary")),
        name="hgrn2",
    )(hg, hg_lb.astype(F32), norm_g_l.reshape(1, HG_DV).astype(F32))


def _compress_kernel(c_ref, pe_ref, w1_ref, b1_ref, w2_ref, w2T_ref, b2_ref, b2T_ref, o_ref, oT_ref):
    half = CMP_STRIDE * NSA_HD
    c = c_ref[0, 0, 0].astype(F32)
    pe = pe_ref[0]
    ca = (c + pe[:, 0:half]).astype(BF16)
    cb = (c + pe[:, half:2 * half]).astype(BF16)
    pa = _nn(ca, w1_ref[0, 0:half, :])
    pb = _nn(cb, w1_ref[0, half:2 * half, :])
    nbp = pa.shape[0]
    h = pa + pltpu.roll(pb, nbp - 1, 0) + b1_ref[0]
    h = jax.nn.gelu(h).astype(BF16)
    o_ref[0, 0, 0] = (_nn(h, w2_ref[0]) + b2_ref[0]).astype(BF16)
    oT_ref[0, 0, 0] = (_nt(w2T_ref[0], h) + b2T_ref[0]).astype(BF16)


def _compress(kn, cmpk, cmpv):
    B, _, G, S, HD = kn.shape
    nbp = S // CMP_STRIDE
    c = kn[:, 0:2].reshape(B, 2, G, nbp, CMP_STRIDE * HD)
    stack = lambda a, b, f: jnp.stack([f(a), f(b)], axis=0)
    pe = stack(cmpk[0], cmpv[0], lambda t: t.reshape(1, CMP_LEN * HD).astype(F32))
    w1 = stack(cmpk[1], cmpv[1], lambda t: t.astype(BF16))
    b1 = stack(cmpk[2], cmpv[2], lambda t: t.reshape(1, CMP_HIDDEN).astype(F32))
    w2 = stack(cmpk[3], cmpv[3], lambda t: t.astype(BF16))
    w2T = stack(cmpk[3], cmpv[3], lambda t: t.astype(BF16).T)
    b2 = stack(cmpk[4], cmpv[4], lambda t: t.reshape(1, HD).astype(F32))
    b2T = stack(cmpk[4], cmpv[4], lambda t: t.reshape(HD, 1).astype(F32))
    wspec = lambda a: pl.BlockSpec((1,) + a.shape[1:], lambda b, t, g: (t,) + (0,) * (a.ndim - 1))
    return pl.pallas_call(
        _compress_kernel,
        grid=(B, 2, G),
        in_specs=[pl.BlockSpec((1, 1, 1, nbp, CMP_STRIDE * HD), lambda b, t, g: (b, t, g, 0, 0)),
                  wspec(pe), wspec(w1), wspec(b1), wspec(w2), wspec(w2T), wspec(b2), wspec(b2T)],
        out_specs=(pl.BlockSpec((1, 1, 1, nbp, HD), lambda b, t, g: (b, t, g, 0, 0)),
                   pl.BlockSpec((1, 1, 1, HD, nbp), lambda b, t, g: (b, t, g, 0, 0))),
        out_shape=(jax.ShapeDtypeStruct((B, 2, G, nbp, HD), BF16),
                   jax.ShapeDtypeStruct((B, 2, G, HD, nbp), BF16)),
        compiler_params=_cparams(("parallel", "parallel", "parallel")),
        name="kv_compress",
    )(c, pe, w1, b1, w2, w2T, b2, b2T)


def _topk_mask_rows(score, k):
    R = score.shape[0]
    r_io = lax.broadcasted_iota(jnp.int32, score.shape, 0)
    s = score
    for _ in range(k):
        m = jnp.max(s, axis=0, keepdims=True)
        idx = jnp.min(jnp.where(s == m, r_io, R), axis=0, keepdims=True)
        s = jnp.where(r_io == idx, -jnp.inf, s)
    return (s == -jnp.inf).astype(F32)


def _with_ones_row(vT):
    pad = (lax.broadcasted_iota(jnp.int32, (16, vT.shape[1]), 0) == 0).astype(BF16)
    return jnp.concatenate([vT, pad], axis=0)


def _nsa_kernel(q_off, qT_ref, kc_ref, vcT_ref, ks_ref, vsT_ref, kw_ref, vwT_ref, gT_ref, ovT_ref,
                o_ref, sel_ref):
    TQ, G, HD, KT, NG = ATT_TQ, NSA_GROUP, NSA_HD, ATT_KT, NSA_KV_HEADS
    qi = pl.program_id(1) + q_off
    t0 = qi * TQ
    pos1 = t0 + lax.broadcasted_iota(jnp.int32, (1, TQ), 1)
    pos4 = jnp.concatenate([pos1] * G, axis=1)
    nbp = kc_ref.shape[3]
    ns = ovT_ref.shape[0]
    n_io = lax.broadcasted_iota(jnp.int32, (nbp, 1), 0)
    mask_c = (n_io * CMP_STRIDE + (CMP_LEN - 1)) <= pos4
    j_io = lax.broadcasted_iota(jnp.int32, (ns, 1), 0)
    cur = pos1 >> int(math.log2(SLC_LEN))
    forced = (j_io == 0) | (j_io == cur) | (j_io == cur - 1)
    causal_blk = j_io * SLC_LEN <= pos1

    q4s, o_cs = [], []
    for g in range(NG):
        qT = qT_ref[0, g * G * HD:(g + 1) * G * HD, :]
        q4 = jnp.concatenate([qT[r * HD:(r + 1) * HD, :] for r in range(G)], axis=1)
        s_c = jnp.where(mask_c, _nn(kc_ref[0, 0, g], q4), NEG)
        p_c = jnp.where(mask_c, jnp.exp2(s_c - jnp.max(s_c, axis=0, keepdims=True)), 0.0)
        p_c = p_c / jnp.maximum(jnp.sum(p_c, axis=0, keepdims=True), 1e-30)
        o_cs.append(_nn(vcT_ref[0, 0, g], p_c.astype(BF16)))
        p_sum = p_c[:, 0:TQ]
        for r in range(1, G):
            p_sum = p_sum + p_c[:, r * TQ:(r + 1) * TQ]
        imp = _nn(ovT_ref[...], p_sum.astype(BF16))
        score = jnp.where(forced, FORCE_SCORE, jnp.where(causal_blk, imp, -1.0))
        sel_ref[g] = (_topk_mask_rows(score, min(SLC_TOPK, ns)) - 1.0) * (-NEG)
        q4s.append(q4)

    KS = ATT_KS
    bps = KS // SLC_LEN
    k_io = lax.broadcasted_iota(jnp.int32, (KS, 1), 0)
    zrows = jnp.zeros((HD - bps, G * TQ), BF16)

    def sel_step(ks_i, carry, diagonal):
        k0 = pl.multiple_of(ks_i * KS, KS)
        out = []
        for g in range(NG):
            m, acc = carry[g]
            bias = sel_ref[g, pl.ds(pl.multiple_of(ks_i * bps, bps), bps), :].astype(BF16)
            rhs = jnp.concatenate([q4s[g], jnp.concatenate([bias] * G, axis=1), zrows], axis=0)
            s = _nn(ks_ref[0, g, pl.ds(k0, KS), :], rhs)
            if diagonal:
                s = jnp.where((k0 + k_io) <= pos4, s, NEG)
            m_new = jnp.maximum(m, jnp.max(s, axis=0, keepdims=True))
            p = jnp.exp2(s - m_new).astype(BF16)
            acc = jnp.exp2(m - m_new) * acc
            for c in range(KS // KT):
                acc = acc + _nn(_with_ones_row(vsT_ref[0, g, ks_i * (KS // KT) + c]), p[c * KT:(c + 1) * KT])
            out.append((m_new, acc))
        return tuple(out)

    n_ks = (t0 + TQ - 1) // KS + 1
    init = tuple((jnp.full((1, G * TQ), NEG, F32), jnp.zeros((HD + 16, G * TQ), F32)) for _ in range(NG))
    carry = lax.fori_loop(0, n_ks - 1, lambda i, c: sel_step(i, c, False), init)
    carry = sel_step(n_ks - 1, carry, True)

    nwt = WIN // TQ + 1
    r_io = lax.broadcasted_iota(jnp.int32, (TQ, 1), 0)
    idxs = [qi - (nwt - 1) + i for i in range(nwt)]
    kpos = jnp.concatenate([idx * TQ + r_io for idx in idxs], axis=0)
    delta = pos4 - kpos
    mask_w = (delta >= 0) & (delta < WIN) & (kpos >= 0)
    outs = []
    for g in range(NG):
        acc_s = carry[g][1]
        o_s = acc_s[0:HD] / jnp.maximum(acc_s[HD:HD + 1], 1e-30)
        kw_t = [kw_ref[0, 0, g, pl.ds(pl.multiple_of(jnp.maximum(idx, 0) * TQ, TQ), TQ), :] for idx in idxs]
        vw_t = [vwT_ref[0, g, jnp.maximum(idx, 0)] for idx in idxs]
        s_w = jnp.where(mask_w, _nn(jnp.concatenate(kw_t, axis=0), q4s[g]), NEG)
        p_w = jnp.exp2(s_w - jnp.max(s_w, axis=0, keepdims=True)).astype(BF16)
        acc_w = _nn(_with_ones_row(jnp.concatenate(vw_t, axis=1)), p_w)
        o_w = acc_w[0:HD] / jnp.maximum(acc_w[HD:HD + 1], 1e-30)
        gs = jax.nn.sigmoid(gT_ref[0, g])
        gate = lambda br: jnp.concatenate([gs[r * 3 + br:r * 3 + br + 1, :] for r in range(G)], axis=1)
        o = gate(0) * o_cs[g] + gate(1) * o_s + gate(2) * o_w
        outs.extend(o[:, r * TQ:(r + 1) * TQ].T for r in range(G))
    o_ref[0] = jnp.concatenate(outs, axis=1).astype(BF16)


def _overlap_T(S):
    nbp = S // CMP_STRIDE
    n_slc = S // SLC_LEN
    cs = np.arange(nbp) * CMP_STRIDE
    ss = np.arange(n_slc) * SLC_LEN
    ov = np.clip(np.minimum(cs[:, None] + CMP_LEN, ss[None, :] + SLC_LEN)
                 - np.maximum(cs[:, None], ss[None, :]), 0, None) / CMP_LEN
    ov[nbp - 1:, :] = 0.0
    return jnp.asarray(ov.T, BF16)


def _nsa(qT, cmp_n, cmp_T, kn, ksE, vsT, vwT, gT, q_lo, q_len):
    B, _, S = qT.shape
    G2, TQ, HD = NSA_KV_HEADS, ATT_TQ, NSA_HD
    nbp = S // CMP_STRIDE
    ns = S // SLC_LEN
    ovT = _overlap_T(S)
    q_off = q_lo // TQ
    return pl.pallas_call(
        functools.partial(_nsa_kernel, q_off),
        grid=(B, q_len // TQ),
        in_specs=[
            pl.BlockSpec((1, NSA_WIDTH, TQ), lambda b, i: (b, 0, i + q_off)),
            pl.BlockSpec((1, 1, G2, nbp, HD), lambda b, i: (b, 0, 0, 0, 0)),
            pl.BlockSpec((1, 1, G2, HD, nbp), lambda b, i: (b, 1, 0, 0, 0)),
            pl.BlockSpec((1, G2, S, 2 * HD), lambda b, i: (b, 0, 0, 0)),
            pl.BlockSpec((1, G2, S // ATT_KT, HD, ATT_KT), lambda b, i: (b, 0, 0, 0, 0)),
            pl.BlockSpec((1, 1, G2, S, HD), lambda b, i: (b, 2, 0, 0, 0)),
            pl.BlockSpec((1, G2, S // LANES, HD, LANES), lambda b, i: (b, 0, 0, 0, 0)),
            pl.BlockSpec((1, G2, 16, TQ), lambda b, i: (b, 0, 0, i + q_off)),
            pl.BlockSpec((ns, nbp), lambda b, i: (0, 0)),
        ],
        out_specs=pl.BlockSpec((1, TQ, NSA_WIDTH), lambda b, i: (b, i, 0)),
        out_shape=jax.ShapeDtypeStruct((B, q_len, NSA_WIDTH), BF16),
        scratch_shapes=[pltpu.VMEM((G2, ns, TQ), F32)],
        compiler_params=_cparams(("parallel", "arbitrary")),
        name="nsa_attn",
    )(qT, cmp_n, cmp_T, ksE, vsT, kn, vwT, gT, ovT)


def _layernorm(t, g, b):
    mu = jnp.mean(t, axis=-1, keepdims=True)
    d = t - mu
    var = jnp.mean(d * d, axis=-1, keepdims=True)
    return d * lax.rsqrt(var + LN_EPS) * g + b


def _outproj_kernel(alpha, ohg_ref, onsa_ref, x_ref, w_ref, g_ref, b_ref, h_ref):
    mix = _nn(ohg_ref[...], w_ref[0:HG_WIDTH, :]) + _nn(onsa_ref[...], w_ref[HG_WIDTH:HG_WIDTH + NSA_WIDTH, :])
    h_ref[...] = _layernorm(alpha * x_ref[...] + mix, g_ref[...], b_ref[...])


def _outproj(alpha, o_hg, o_nsa, x2, w_out_l, g, b):
    n, D = x2.shape
    tm = OUT_TM
    row = lambda w: pl.BlockSpec((tm, w), lambda i: (i, 0))
    const = lambda shp: pl.BlockSpec(shp, lambda i: (0, 0))
    return pl.pallas_call(
        functools.partial(_outproj_kernel, alpha),
        grid=(n // tm,),
        in_specs=[row(HG_WIDTH), row(NSA_WIDTH), row(D), const(w_out_l.shape), const((1, D)), const((1, D))],
        out_specs=row(D),
        out_shape=jax.ShapeDtypeStruct((n, D), F32),
        compiler_params=_cparams(("parallel",)),
        name="out_proj_ln",
    )(o_hg, o_nsa, x2, w_out_l.astype(BF16), g.reshape(1, D).astype(F32), b.reshape(1, D).astype(F32))


def _topk_rows(s, k):
    R = s.shape[0]
    r_io = lax.broadcasted_iota(jnp.int32, s.shape, 0)
    vals, idxs = [], []
    for _ in range(k):
        m = jnp.max(s, axis=0, keepdims=True)
        idx = jnp.min(jnp.where(s == m, r_io, R), axis=0, keepdims=True)
        vals.append(m)
        idxs.append(idx)
        s = jnp.where(r_io == idx, -jnp.inf, s)
    return jnp.concatenate(vals, axis=0), jnp.concatenate(idxs, axis=0)


def _route_kernel(h_ref, wq_ref, k1_ref, k2_ref, e_ref, g_ref):
    K, half = PEER_TOPK, PEER_DQ // 2
    q = _nn(h_ref[...].astype(BF16), wq_ref[...]).astype(BF16)
    for h in range(PEER_HEADS):
        s1 = _nt(k1_ref[...], q[:, h * PEER_DQ:h * PEER_DQ + half])
        s2 = _nt(k2_ref[...], q[:, h * PEER_DQ + half:(h + 1) * PEER_DQ])
        v1, i1 = _topk_rows(s1, K)
        v2, i2 = _topk_rows(s2, K)
        sub = 8
        b_io = lax.broadcasted_iota(jnp.int32, (sub, 1), 0)
        cands = [v1[0:1, :] + v2]
        ids = [i1[0:1, :] * PEER_NKEYS + i2]
        for a in range(1, sub):
            cands.append(jnp.where(b_io < K // (a + 1), v1[a:a + 1, :] + v2[0:sub, :], -jnp.inf))
            ids.append(i1[a:a + 1, :] * PEER_NKEYS + i2[0:sub, :])
        cands.append(v1[sub:K, :] + v2[0:1, :])
        ids.append(i1[sub:K, :] * PEER_NKEYS + i2[0:1, :])
        cand = jnp.concatenate(cands, axis=0)
        cand_id = jnp.concatenate(ids, axis=0)
        r_io = lax.broadcasted_iota(jnp.int32, cand.shape, 0)
        top_s, top_e = [], []
        for _ in range(K):
            m = jnp.max(cand, axis=0, keepdims=True)
            hit = r_io == jnp.min(jnp.where(cand == m, r_io, cand.shape[0]), axis=0, keepdims=True)
            top_s.append(m)
            top_e.append(jnp.sum(jnp.where(hit, cand_id, 0), axis=0, keepdims=True))
            cand = jnp.where(hit, -jnp.inf, cand)
        top_s = jnp.concatenate(top_s, axis=0)
        e_ref[h * K:(h + 1) * K, :] = jnp.concatenate(top_e, axis=0)
        ex = jnp.exp(top_s - top_s[0:1, :])
        g_ref[h * K:(h + 1) * K, :] = ex / jnp.sum(ex, axis=0, keepdims=True)


def _route(h1, wq, k1, k2):
    n, D = h1.shape
    tr = ROUTE_TR
    hk = PEER_HEADS * PEER_TOPK
    const = lambda shp: pl.BlockSpec(shp, lambda i: (0, 0))
    return pl.pallas_call(
        _route_kernel,
        grid=(n // tr,),
        in_specs=[pl.BlockSpec((tr, D), lambda i: (i, 0)), const(wq.shape), const(k1.shape), const(k2.shape)],
        out_specs=(pl.BlockSpec((hk, tr), lambda i: (0, i)), pl.BlockSpec((hk, tr), lambda i: (0, i))),
        out_shape=(jax.ShapeDtypeStruct((hk, n), jnp.int32), jax.ShapeDtypeStruct((hk, n), F32)),
        compiler_params=_cparams(("parallel",)),
        name="peer_route",
    )(h1, wq.astype(BF16), k1.astype(BF16), k2.astype(BF16))


def _pack_rows(t):
    half = t.shape[1] // 2
    tb = t.astype(BF16)
    lo = lax.bitcast_convert_type(tb[:, :half], jnp.uint16).astype(jnp.uint32)
    hi = lax.bitcast_convert_type(tb[:, half:], jnp.uint16).astype(jnp.uint32)
    return lo | (hi << 16)


def _sc_peer(utab, vtab, idx, x):
    n = x.shape[0]
    w = utab.shape[1]
    hk = PEER_HEADS * PEER_TOPK
    L = SC_LANES
    T = n // SC_WORKERS
    uwin = SC_UWIN
    upt = hk // uwin
    vpt = hk // SC_VWIN
    per_u = vpt // upt
    vlead = SC_VSLOT // 2
    assert n % SC_WORKERS == 0 and T % SC_XB == 0 and uwin % L == 0
    assert upt % SC_USLOT == 0 and vpt % SC_VSLOT == 0 and vpt % upt == 0
    vsteps = T * vpt
    mesh = plsc.VectorSubcoreMesh(core_axis_name="c", subcore_axis_name="s")
    cp = dataclasses.replace(pltpu.CompilerParams(), needs_layout_passes=False)

    @pl.kernel(out_type=(jax.ShapeDtypeStruct((n * hk,), F32), jax.ShapeDtypeStruct((n * hk, w), utab.dtype)),
               mesh=mesh, compiler_params=cp,
               scratch_types=[pltpu.VMEM((T * hk,), jnp.int32),
                              pltpu.VMEM((SC_XB, 2 * w), F32),
                              pltpu.VMEM((SC_USLOT, uwin, w), utab.dtype),
                              pltpu.VMEM((SC_VSLOT, SC_VWIN, w), vtab.dtype),
                              pltpu.VMEM((L, L), F32),
                              pltpu.VMEM((SC_XB * hk,), F32),
                              pltpu.SemaphoreType.DMA((SC_USLOT,)),
                              pltpu.SemaphoreType.DMA((SC_VSLOT,)),
                              pltpu.SemaphoreType.DMA((SC_VSLOT,))])
    def peer(u_hbm, v_hbm, i_hbm, x_hbm, h_hbm, vg_hbm, idx_v, x_v, ubuf, vbuf, acc_s, h_v, usem, vsem, wsem):
        wid = lax.axis_index("c") * (SC_WORKERS // 2) + lax.axis_index("s")
        tok0 = wid * T
        row0 = tok0 * hk
        pltpu.sync_copy(i_hbm.at[pl.ds(row0, T * hk)], idx_v)

        def ufetch(t, j):
            slot = j % SC_USLOT
            return pltpu.make_async_copy(u_hbm.at[idx_v.at[pl.ds(t * hk + j * uwin, uwin)]], ubuf.at[slot], usem.at[slot])

        def vfetch(s, slot):
            return pltpu.make_async_copy(v_hbm.at[idx_v.at[pl.ds(s * SC_VWIN, SC_VWIN)]], vbuf.at[slot], vsem.at[slot])

        def vflush(s, slot):
            return pltpu.make_async_copy(vbuf.at[slot], vg_hbm.at[pl.ds(row0 + s * SC_VWIN, SC_VWIN)], wsem.at[slot])

        for j in range(SC_USLOT):
            ufetch(0, j).start()
        for j in range(vlead):
            vfetch(j, j).start()

        def vstep(s, slot):
            other = (slot + vlead) % SC_VSLOT
            vfetch(s, slot).wait()
            vflush(s, slot).start()

            @pl.when(s >= vlead)
            def _():
                vflush(s - vlead, other).wait()

            @pl.when(s + vlead < vsteps)
            def _():
                vfetch(s + vlead, other).start()

        lane = lax.iota(jnp.int32, L)

        @pl.loop(0, T // SC_XB)
        def _(tb):
            pltpu.sync_copy(x_hbm.at[pl.ds(pl.multiple_of(tok0 + tb * SC_XB, SC_XB), SC_XB)], x_v)

            @pl.loop(0, SC_XB)
            def _(tt):
                t = tb * SC_XB + tt
                for j in range(upt):
                    for i in range(per_u):
                        vstep(t * vpt + j * per_u + i, (j * per_u + i) % SC_VSLOT)
                    ufetch(t, j).wait()

                    def dot_body(jj, accs):
                        off = pl.multiple_of(jj * L, L)
                        xlo = x_v[tt, pl.ds(off, L)]
                        xhi = x_v[tt, pl.ds(w + off, L)]
                        out = []
                        for r in range(uwin):
                            wv = ubuf[j % SC_USLOT, r, pl.ds(off, L)]
                            lo = plsc.bitcast(wv << 16, F32)
                            hi = plsc.bitcast(wv & jnp.uint32(0xFFFF0000), F32)
                            out.append(accs[r] + lo * xlo + hi * xhi)
                        return tuple(out)

                    accs = lax.fori_loop(0, w // L, dot_body, tuple(jnp.zeros((L,), F32) for _ in range(uwin)))

                    if j + SC_USLOT < upt:
                        ufetch(t, j + SC_USLOT).start()
                    else:
                        @pl.when(t + 1 < T)
                        def _():
                            ufetch(t + 1, j + SC_USLOT - upt).start()

                    for g in range(uwin // L):
                        for r in range(L):
                            acc_s[r, :] = accs[g * L + r]
                        tot = jnp.zeros((L,), F32)
                        for c in range(L):
                            tot = tot + plsc.load_gather(acc_s, [lane, jnp.full((L,), c, jnp.int32)])
                        h_v[pl.ds(tt * hk + j * uwin + g * L, L)] = tot

            pltpu.sync_copy(h_v, h_hbm.at[pl.ds(pl.multiple_of(row0 + tb * SC_XB * hk, SC_XB * hk), SC_XB * hk)])

        for s in range(vsteps - vlead, vsteps):
            vflush(s, s % SC_VSLOT).wait()

    return peer(utab, vtab, idx, x)


def _unpack(w):
    lo = lax.bitcast_convert_type(w << 16, F32)
    hi = lax.bitcast_convert_type(w & jnp.uint32(0xFFFF0000), F32)
    return lo, hi


def _mix_kernel(alpha, h_ref, s_ref, gate_ref, vg_ref, g_ref, b_ref, o_ref):
    TT, HK = MIX_TT, PEER_HEADS * PEER_TOPK
    x = h_ref[...]
    wrow = gate_ref[...] * jax.nn.gelu(s_ref[...])
    eye = lax.broadcasted_iota(jnp.int32, (HK, HK), 0) == lax.broadcasted_iota(jnp.int32, (HK, HK), 1)
    rows = []
    for t in range(TT):
        wcol = jnp.sum(jnp.where(eye, jnp.broadcast_to(wrow[t:t + 1, :], (HK, HK)), 0.0),
                       axis=1, keepdims=True)
        vlo, vhi = _unpack(vg_ref[t * HK:(t + 1) * HK, :])
        rows.append(jnp.concatenate([jnp.sum(wcol * vlo, axis=0, keepdims=True),
                                     jnp.sum(wcol * vhi, axis=0, keepdims=True)], axis=1))
    ffn = jnp.concatenate(rows, axis=0)
    o_ref[...] = _layernorm(alpha * x + ffn, g_ref[...], b_ref[...])


def _mix(alpha, h1, scores, gates, vg, g, b):
    n, D = h1.shape
    tt = MIX_TT
    hk = PEER_HEADS * PEER_TOPK
    const = lambda shp: pl.BlockSpec(shp, lambda i: (0, 0))
    return pl.pallas_call(
        functools.partial(_mix_kernel, alpha),
        grid=(n // tt,),
        in_specs=[pl.BlockSpec((tt, D), lambda i: (i, 0)), pl.BlockSpec((tt, hk), lambda i: (i, 0)),
                  pl.BlockSpec((tt, hk), lambda i: (i, 0)), pl.BlockSpec((tt * hk, D // 2), lambda i: (i, 0)),
                  const((1, D)), const((1, D))],
        out_specs=pl.BlockSpec((tt, D), lambda i: (i, 0)),
        out_shape=jax.ShapeDtypeStruct((n, D), F32),
        compiler_params=_cparams(("parallel",)),
        name="peer_mix_ln",
    )(h1, scores, gates, vg, g.reshape(1, D).astype(F32), b.reshape(1, D).astype(F32))


PEER_CHUNK = 4096
SEQ_PIECE = 4096


def _peer_ffn_ln(alpha, h1, wq, k1, k2, up, vp, g, b):
    eT, gT = _route(h1, wq, k1, k2)
    return _peer_experts_ln(alpha, h1, eT, gT, up, vp, g, b)


def _peer_experts_ln(alpha, h1, eT, gT, up, vp, g, b):
    n, D = h1.shape
    hk = PEER_HEADS * PEER_TOPK
    experts = eT.T
    gates = gT.T
    ch = min(PEER_CHUNK, n)
    outs = []
    for c in range(n // ch):
        hc = h1[c * ch:(c + 1) * ch]
        scores, vg = _sc_peer(up, vp, experts[c * ch:(c + 1) * ch].reshape(ch * hk), hc)
        outs.append(_mix(alpha, hc, scores.reshape(ch, hk), gates[c * ch:(c + 1) * ch], vg, g, b))
    return jnp.concatenate(outs, axis=0)


def kernel(x, w_in, w_out, hg_lb, hg_norm_g, cmpk_pe, cmpk_w1, cmpk_b1, cmpk_w2, cmpk_b2,
           cmpv_pe, cmpv_w1, cmpv_b1, cmpv_w2, cmpv_b2, ln1_g, ln1_b,
           peer_wq, peer_k1, peer_k2, peer_u, peer_v, ln2_g, ln2_b):
    B, S, D = x.shape
    depth = w_in.shape[0]
    alpha = (2.0 * depth) ** 0.25
    h = x
    for l in range(depth):
        up, vp = _pack_rows(peer_u[l]), _pack_rows(peer_v[l])
        cmpk = (cmpk_pe[l], cmpk_w1[l], cmpk_b1[l], cmpk_w2[l], cmpk_b2[l])
        cmpv = (cmpv_pe[l], cmpv_w1[l], cmpv_b1[l], cmpv_w2[l], cmpv_b2[l])
        outs = []
        for bi in range(B):
            hb = h[bi:bi + 1]
            hg, qT, kn, ksE, vsT, vwT, gT = _project(hb, w_in[l])
            o_hg = _hgrn2(hg, hg_lb, hg_norm_g[l], l)
            cmp_n, cmp_T = _compress(kn, cmpk, cmpv)
            piece = min(SEQ_PIECE, S)
            for lo in range(0, S, piece):
                o_nsa = _nsa(qT, cmp_n, cmp_T, kn, ksE, vsT, vwT, gT, lo, piece)
                h1 = _outproj(alpha, o_hg[0, lo:lo + piece], o_nsa.reshape(piece, NSA_WIDTH),
                              hb[0, lo:lo + piece], w_out[l], ln1_g[l], ln1_b[l])
                eT, egT = _route(h1, peer_wq[l], peer_k1[l], peer_k2[l])
                qT, eT, egT = lax.optimization_barrier((qT, eT, egT))
                outs.append(_peer_experts_ln(alpha, h1, eT, egT, up, vp, ln2_g[l], ln2_b[l]))
        h = jnp.concatenate(outs, axis=0).reshape(B, S, D)
    return h
```

```python
import dataclasses
import functools
import math

import jax
import jax.numpy as jnp
import numpy as np
from jax import lax
from jax.experimental import pallas as pl
from jax.experimental.pallas import tpu as pltpu
from jax.experimental.pallas import tpu_sc as plsc

F32 = jnp.float32
BF16 = jnp.bfloat16

HG_HEADS = 4
HG_DK = 128
HG_DV = 128
HG_CHUNK = 64
HG_SUB = 16
HG_WIDTH = HG_HEADS * HG_DV
NSA_HEADS = 8
NSA_KV_HEADS = 2
NSA_GROUP = NSA_HEADS // NSA_KV_HEADS
NSA_HD = 64
NSA_WIDTH = NSA_HEADS * NSA_HD
CMP_LEN = 32
CMP_STRIDE = 16
CMP_HIDDEN = 256
SLC_LEN = 64
SLC_TOPK = 16
WIN = 512
FORCE_SCORE = 1.0e4
PEER_HEADS = 8
PEER_DQ = 256
PEER_NKEYS = 128
PEER_TOPK = 16
LN_EPS = 1e-5
NEG = -1e30
LOG2E = math.log2(math.e)

LANES = 128
PROJ_TM = 512
HG_TC = 256
ATT_TQ = 128
ATT_KT = 512
ATT_KS = 1024
OUT_TM = 512
ROUTE_TR = 256
MIX_TT = 16
SC_WORKERS = 32
SC_LANES = 16
SC_UWIN = 16
SC_USLOT = 4
SC_VWIN = 16
SC_VSLOT = 8
SC_XB = 8
VMEM_LIMIT = 56 * 1024 * 1024


def _cparams(sem):
    return pltpu.CompilerParams(dimension_semantics=sem, vmem_limit_bytes=VMEM_LIMIT)


def _nt(a, b):
    return lax.dot_general(a, b, (((1,), (1,)), ((), ())), preferred_element_type=F32)


def _tn(a, b):
    return lax.dot_general(a, b, (((0,), (0,)), ((), ())), preferred_element_type=F32)


def _nn(a, b):
    return jnp.dot(a, b, preferred_element_type=F32)


def _proj_kernel(x_ref, whg_ref, wqT_ref, wkn_ref, wvT_ref, wgT_ref,
                 hg_ref, qT_ref, kn_ref, ksE_ref, vsT_ref, vwT_ref, gT_ref):
    xb = x_ref[0].astype(BF16)
    hg_ref[0] = _nn(xb, whg_ref[...])
    qT_ref[0] = (_nt(wqT_ref[...], xb) * (NSA_HD ** -0.5 * LOG2E)).astype(BF16)
    kn = _nn(xb, wkn_ref[...])
    for j in range(3):
        for g in range(NSA_KV_HEADS):
            lo = j * 2 * NSA_HD + g * NSA_HD
            kn_ref[0, j, g] = kn[:, lo:lo + NSA_HD].astype(BF16)
    row = pl.program_id(1) * kn.shape[0] + lax.broadcasted_iota(jnp.int32, (kn.shape[0], NSA_HD), 0)
    blk = (row >> int(math.log2(SLC_LEN))) & (ATT_KS // SLC_LEN - 1)
    onehot = (lax.broadcasted_iota(jnp.int32, (kn.shape[0], NSA_HD), 1) == blk).astype(F32)
    for g in range(NSA_KV_HEADS):
        lo = 3 * 2 * NSA_HD + g * NSA_HD
        ksE_ref[0, g] = jnp.concatenate([kn[:, lo:lo + NSA_HD], onehot], axis=1).astype(BF16)
    vT = _nt(wvT_ref[...], xb).astype(BF16)
    tm = vT.shape[1]
    for g in range(NSA_KV_HEADS):
        for c in range(tm // ATT_KT):
            vsT_ref[0, g, c] = vT[g * NSA_HD:(g + 1) * NSA_HD, c * ATT_KT:(c + 1) * ATT_KT]
        for c in range(tm // LANES):
            vwT_ref[0, g, c] = vT[(2 + g) * NSA_HD:(3 + g) * NSA_HD, c * LANES:(c + 1) * LANES]
    gT = _nt(wgT_ref[...], xb)
    gT_ref[0, 0] = gT[0:16]
    gT_ref[0, 1] = gT[16:32]


def _project(x, w_in_l):
    B, S, D = x.shape
    tm = PROJ_TM
    o = np.cumsum((0, 512, 512, 512, 512, 512, 128, 128, 128, 128, 128, 128, 24))
    wb = w_in_l.astype(BF16)
    whg = wb[:, o[0]:o[4]]
    wqT = wb[:, o[4]:o[5]].T
    wkn = jnp.concatenate([wb[:, o[5]:o[6]], wb[:, o[6]:o[7]], wb[:, o[9]:o[10]], wb[:, o[7]:o[8]]], axis=1)
    wvT = jnp.concatenate([wb[:, o[8]:o[9]], wb[:, o[10]:o[11]]], axis=1).T
    wg = wb[:, o[11]:o[12]].reshape(D, NSA_KV_HEADS, NSA_GROUP * 3)
    wgT = jnp.pad(wg, ((0, 0), (0, 0), (0, 16 - NSA_GROUP * 3))).reshape(D, 32).T
    const = lambda shp: pl.BlockSpec(shp, lambda b, i: (0,) * len(shp))
    out_shape = (
        jax.ShapeDtypeStruct((B, S, 4 * 512), F32),
        jax.ShapeDtypeStruct((B, NSA_WIDTH, S), BF16),
        jax.ShapeDtypeStruct((B, 3, NSA_KV_HEADS, S, NSA_HD), BF16),
        jax.ShapeDtypeStruct((B, NSA_KV_HEADS, S, 2 * NSA_HD), BF16),
        jax.ShapeDtypeStruct((B, NSA_KV_HEADS, S // ATT_KT, NSA_HD, ATT_KT), BF16),
        jax.ShapeDtypeStruct((B, NSA_KV_HEADS, S // LANES, NSA_HD, LANES), BF16),
        jax.ShapeDtypeStruct((B, NSA_KV_HEADS, 16, S), F32),
    )
    out_specs = (
        pl.BlockSpec((1, tm, 2048), lambda b, i: (b, i, 0)),
        pl.BlockSpec((1, NSA_WIDTH, tm), lambda b, i: (b, 0, i)),
        pl.BlockSpec((1, 3, NSA_KV_HEADS, tm, NSA_HD), lambda b, i: (b, 0, 0, i, 0)),
        pl.BlockSpec((1, NSA_KV_HEADS, tm, 2 * NSA_HD), lambda b, i: (b, 0, i, 0)),
        pl.BlockSpec((1, NSA_KV_HEADS, tm // ATT_KT, NSA_HD, ATT_KT), lambda b, i: (b, 0, i, 0, 0)),
        pl.BlockSpec((1, NSA_KV_HEADS, tm // LANES, NSA_HD, LANES), lambda b, i: (b, 0, i, 0, 0)),
        pl.BlockSpec((1, NSA_KV_HEADS, 16, tm), lambda b, i: (b, 0, 0, i)),
    )
    return pl.pallas_call(
        _proj_kernel,
        grid=(B, S // tm),
        in_specs=[pl.BlockSpec((1, tm, D), lambda b, i: (b, i, 0)),
                  const(whg.shape), const(wqT.shape), const(wkn.shape), const(wvT.shape), const(wgT.shape)],
        out_specs=out_specs,
        out_shape=out_shape,
        compiler_params=_cparams(("parallel", "arbitrary")),
        name="in_proj",
    )(x, whg, wqT, wkn, wvT, wgT)


def _split3(a):
    hi = a.astype(BF16)
    r = a - hi.astype(F32)
    mid = r.astype(BF16)
    lo = (r - mid.astype(F32)).astype(BF16)
    return hi, mid, lo


def _hgrn_kernel(layer, hg_ref, lb_ref, ng_ref, o_ref, st_ref):
    C, SB, H, DK = HG_CHUNK, HG_SUB, HG_HEADS, HG_DK
    nsb = C // SB

    @pl.when(pl.program_id(1) == 0)
    def _():
        st_ref[...] = jnp.zeros_like(st_ref)

    lbp = lb_ref[...]
    e = jnp.exp(lbp - jnp.max(lbp, axis=0, keepdims=True))
    lb = jnp.sum(e[:layer + 1], axis=0, keepdims=True) / jnp.sum(e, axis=0, keepdims=True)

    ri = lax.broadcasted_iota(jnp.int32, (C, C), 0)
    ci = lax.broadcasted_iota(jnp.int32, (C, C), 1)
    tril = (ci <= ri).astype(BF16)
    t_sub = lax.broadcasted_iota(jnp.int32, (SB, H * DK), 0)
    lane16 = lax.broadcasted_iota(jnp.int32, (SB, SB), 1)
    ng = ng_ref[...]

    def chunk(c, carry):
        r0 = pl.multiple_of(c * C, C)
        blk = hg_ref[0, pl.ds(r0, C), :]
        q = blk[:, 0:512]
        f = lb + (1.0 - lb) * jax.nn.sigmoid(blk[:, 512:1024])
        lf = jnp.log(f)
        k = 1.0 - f
        v = blk[:, 1024:1536]
        gate = blk[:, 1536:2048]
        hi, mid, lo = _split3(lf)
        b = _nn(tril, hi) + _nn(tril, mid) + _nn(tril, lo)
        b_last = b[C - 1:C, :]
        qe = (q * jnp.exp(b)).astype(BF16)
        ks = (k * jnp.exp(b_last - b)).astype(BF16)
        dec = jnp.exp(b_last)
        vb = v.astype(BF16)
        rblk = jnp.concatenate([jnp.broadcast_to(b[i * SB:i * SB + 1, :], (SB, H * DK)) for i in range(nsb)], axis=0)
        qn = (q * jnp.exp(b - rblk)).astype(BF16)
        outs = []
        for h in range(H):
            sl = slice(h * DK, (h + 1) * DK)
            stT = st_ref[h]
            o_h = _nt(qe[:, sl], stT.astype(BF16))
            rows = []
            for i in range(nsb):
                rs = slice(i * SB, (i + 1) * SB)
                acc = jnp.zeros((SB, HG_DV), F32)
                if i > 0:
                    ke = (k[0:i * SB, sl] * jnp.exp(b[i * SB:i * SB + 1, sl] - b[0:i * SB, sl])).astype(BF16)
                    a_off = _nt(qn[rs, sl], ke)
                    acc = acc + _nn(a_off.astype(BF16), vb[0:i * SB, sl])
                qi, ki, bi = q[rs, sl], k[rs, sl], b[rs, sl]
                a_d = jnp.zeros((SB, SB), F32)
                for s in range(SB):
                    w = qi * ki[s:s + 1, :] * jnp.exp(jnp.minimum(bi - bi[s:s + 1, :], 0.0))
                    col = jnp.sum(w, axis=1, keepdims=True)
                    a_d = a_d + jnp.where((lane16 == s) & (t_sub[:, 0:SB] >= s), col, 0.0)
                acc = acc + _nn(a_d.astype(BF16), vb[rs, sl])
                rows.append(acc)
            o_h = o_h + jnp.concatenate(rows, axis=0)
            st_ref[h] = dec[:, sl] * stT + _tn(vb[:, sl], ks[:, sl])
            o_h = o_h * lax.rsqrt(jnp.mean(o_h * o_h, axis=1, keepdims=True) + LN_EPS) * ng
            outs.append(o_h)
        o = jnp.concatenate(outs, axis=1) * jax.nn.silu(gate)
        o_ref[0, pl.ds(r0, C), :] = o.astype(BF16)
        return carry

    lax.fori_loop(0, hg_ref.shape[1] // C, chunk, 0)


def _hgrn2(hg, hg_lb, norm_g_l, layer):
    B, S, _ = hg.shape
    tc = HG_TC
    return pl.pallas_call(
        functools.partial(_hgrn_kernel, layer),
        grid=(B, S // tc),
        in_specs=[pl.BlockSpec((1, tc, 2048), lambda b, i: (b, i, 0)),
                  pl.BlockSpec(hg_lb.shape, lambda b, i: (0, 0)),
                  pl.BlockSpec((1, HG_DV), lambda b, i: (0, 0))],
        out_specs=pl.BlockSpec((1, tc, HG_WIDTH), lambda b, i: (b, i, 0)),
        out_shape=jax.ShapeDtypeStruct((B, S, HG_WIDTH), BF16),
        scratch_shapes=[pltpu.VMEM((HG_HEADS, HG_DV, HG_DK), F32)],
        compiler_params=_cparams(("parallel", "arbitrary")),
        name="hgrn2",
    )(hg, hg_lb.astype(F32), norm_g_l.reshape(1, HG_DV).astype(F32))


def _compress_kernel(c_ref, pe_ref, w1_ref, b1_ref, w2_ref, w2T_ref, b2_ref, b2T_ref, o_ref, oT_ref):
    half = CMP_STRIDE * NSA_HD
    c = c_ref[0, 0, 0].astype(F32)
    pe = pe_ref[0]
    ca = (c + pe[:, 0:half]).astype(BF16)
    cb = (c + pe[:, half:2 * half]).astype(BF16)
    pa = _nn(ca, w1_ref[0, 0:half, :])
    pb = _nn(cb, w1_ref[0, half:2 * half, :])
    nbp = pa.shape[0]
    h = pa + pltpu.roll(pb, nbp - 1, 0) + b1_ref[0]
    h = jax.nn.gelu(h).astype(BF16)
    o_ref[0, 0, 0] = (_nn(h, w2_ref[0]) + b2_ref[0]).astype(BF16)
    oT_ref[0, 0, 0] = (_nt(w2T_ref[0], h) + b2T_ref[0]).astype(BF16)


def _compress(kn, cmpk, cmpv):
    B, _, G, S, HD = kn.shape
    nbp = S // CMP_STRIDE
    c = kn[:, 0:2].reshape(B, 2, G, nbp, CMP_STRIDE * HD)
    stack = lambda a, b, f: jnp.stack([f(a), f(b)], axis=0)
    pe = stack(cmpk[0], cmpv[0], lambda t: t.reshape(1, CMP_LEN * HD).astype(F32))
    w1 = stack(cmpk[1], cmpv[1], lambda t: t.astype(BF16))
    b1 = stack(cmpk[2], cmpv[2], lambda t: t.reshape(1, CMP_HIDDEN).astype(F32))
    w2 = stack(cmpk[3], cmpv[3], lambda t: t.astype(BF16))
    w2T = stack(cmpk[3], cmpv[3], lambda t: t.astype(BF16).T)
    b2 = stack(cmpk[4], cmpv[4], lambda t: t.reshape(1, HD).astype(F32))
    b2T = stack(cmpk[4], cmpv[4], lambda t: t.reshape(HD, 1).astype(F32))
    wspec = lambda a: pl.BlockSpec((1,) + a.shape[1:], lambda b, t, g: (t,) + (0,) * (a.ndim - 1))
    return pl.pallas_call(
        _compress_kernel,
        grid=(B, 2, G),
        in_specs=[pl.BlockSpec((1, 1, 1, nbp, CMP_STRIDE * HD), lambda b, t, g: (b, t, g, 0, 0)),
                  wspec(pe), wspec(w1), wspec(b1), wspec(w2), wspec(w2T), wspec(b2), wspec(b2T)],
        out_specs=(pl.BlockSpec((1, 1, 1, nbp, HD), lambda b, t, g: (b, t, g, 0, 0)),
                   pl.BlockSpec((1, 1, 1, HD, nbp), lambda b, t, g: (b, t, g, 0, 0))),
        out_shape=(jax.ShapeDtypeStruct((B, 2, G, nbp, HD), BF16),
                   jax.ShapeDtypeStruct((B, 2, G, HD, nbp), BF16)),
        compiler_params=_cparams(("parallel", "parallel", "parallel")),
        name="kv_compress",
    )(c, pe, w1, b1, w2, w2T, b2, b2T)


def _topk_mask_rows(score, k):
    R = score.shape[0]
    r_io = lax.broadcasted_iota(jnp.int32, score.shape, 0)
    s = score
    for _ in range(k):
        m = jnp.max(s, axis=0, keepdims=True)
        idx = jnp.min(jnp.where(s == m, r_io, R), axis=0, keepdims=True)
        s = jnp.where(r_io == idx, -jnp.inf, s)
    return (s == -jnp.inf).astype(F32)


def _with_ones_row(vT):
    pad = (lax.broadcasted_iota(jnp.int32, (16, vT.shape[1]), 0) == 0).astype(BF16)
    return jnp.concatenate([vT, pad], axis=0)


def _nsa_kernel(q_off, qT_ref, kc_ref, vcT_ref, ks_ref, vsT_ref, kw_ref, vwT_ref, gT_ref, ovT_ref,
                o_ref, sel_ref):
    TQ, G, HD, KT, NG = ATT_TQ, NSA_GROUP, NSA_HD, ATT_KT, NSA_KV_HEADS
    qi = pl.program_id(1) + q_off
    t0 = qi * TQ
    pos1 = t0 + lax.broadcasted_iota(jnp.int32, (1, TQ), 1)
    pos4 = jnp.concatenate([pos1] * G, axis=1)
    nbp = kc_ref.shape[3]
    ns = ovT_ref.shape[0]
    n_io = lax.broadcasted_iota(jnp.int32, (nbp, 1), 0)
    mask_c = (n_io * CMP_STRIDE + (CMP_LEN - 1)) <= pos4
    j_io = lax.broadcasted_iota(jnp.int32, (ns, 1), 0)
    cur = pos1 >> int(math.log2(SLC_LEN))
    forced = (j_io == 0) | (j_io == cur) | (j_io == cur - 1)
    causal_blk = j_io * SLC_LEN <= pos1

    q4s, o_cs = [], []
    for g in range(NG):
        qT = qT_ref[0, g * G * HD:(g + 1) * G * HD, :]
        q4 = jnp.concatenate([qT[r * HD:(r + 1) * HD, :] for r in range(G)], axis=1)
        s_c = jnp.where(mask_c, _nn(kc_ref[0, 0, g], q4), NEG)
        p_c = jnp.where(mask_c, jnp.exp2(s_c - jnp.max(s_c, axis=0, keepdims=True)), 0.0)
        p_c = p_c / jnp.maximum(jnp.sum(p_c, axis=0, keepdims=True), 1e-30)
        o_cs.append(_nn(vcT_ref[0, 0, g], p_c.astype(BF16)))
        p_sum = p_c[:, 0:TQ]
        for r in range(1, G):
            p_sum = p_sum + p_c[:, r * TQ:(r + 1) * TQ]
        imp = _nn(ovT_ref[...], p_sum.astype(BF16))
        score = jnp.where(forced, FORCE_SCORE, jnp.where(causal_blk, imp, -1.0))
        sel_ref[g] = (_topk_mask_rows(score, min(SLC_TOPK, ns)) - 1.0) * (-NEG)
        q4s.append(q4)

    KS = ATT_KS
    bps = KS // SLC_LEN
    k_io = lax.broadcasted_iota(jnp.int32, (KS, 1), 0)
    zrows = jnp.zeros((HD - bps, G * TQ), BF16)

    def sel_step(ks_i, carry, diagonal):
        k0 = pl.multiple_of(ks_i * KS, KS)
        out = []
        for g in range(NG):
            m, acc = carry[g]
            bias = sel_ref[g, pl.ds(pl.multiple_of(ks_i * bps, bps), bps), :].astype(BF16)
            rhs = jnp.concatenate([q4s[g], jnp.concatenate([bias] * G, axis=1), zrows], axis=0)
            s = _nn(ks_ref[0, g, pl.ds(k0, KS), :], rhs)
            if diagonal:
                s = jnp.where((k0 + k_io) <= pos4, s, NEG)
            m_new = jnp.maximum(m, jnp.max(s, axis=0, keepdims=True))
            p = jnp.exp2(s - m_new).astype(BF16)
            acc = jnp.exp2(m - m_new) * acc
            for c in range(KS // KT):
                acc = acc + _nn(_with_ones_row(vsT_ref[0, g, ks_i * (KS // KT) + c]), p[c * KT:(c + 1) * KT])
            out.append((m_new, acc))
        return tuple(out)

    n_ks = (t0 + TQ - 1) // KS + 1
    init = tuple((jnp.full((1, G * TQ), NEG, F32), jnp.zeros((HD + 16, G * TQ), F32)) for _ in range(NG))
    carry = lax.fori_loop(0, n_ks - 1, lambda i, c: sel_step(i, c, False), init)
    carry = sel_step(n_ks - 1, carry, True)

    nwt = WIN // TQ + 1
    r_io = lax.broadcasted_iota(jnp.int32, (TQ, 1), 0)
    idxs = [qi - (nwt - 1) + i for i in range(nwt)]
    kpos = jnp.concatenate([idx * TQ + r_io for idx in idxs], axis=0)
    delta = pos4 - kpos
    mask_w = (delta >= 0) & (delta < WIN) & (kpos >= 0)
    outs = []
    for g in range(NG):
        acc_s = carry[g][1]
        o_s = acc_s[0:HD] / jnp.maximum(acc_s[HD:HD + 1], 1e-30)
        kw_t = [kw_ref[0, 0, g, pl.ds(pl.multiple_of(jnp.maximum(idx, 0) * TQ, TQ), TQ), :] for idx in idxs]
        vw_t = [vwT_ref[0, g, jnp.maximum(idx, 0)] for idx in idxs]
        s_w = jnp.where(mask_w, _nn(jnp.concatenate(kw_t, axis=0), q4s[g]), NEG)
        p_w = jnp.exp2(s_w - jnp.max(s_w, axis=0, keepdims=True)).astype(BF16)
        acc_w = _nn(_with_ones_row(jnp.concatenate(vw_t, axis=1)), p_w)
        o_w = acc_w[0:HD] / jnp.maximum(acc_w[HD:HD + 1], 1e-30)
        gs = jax.nn.sigmoid(gT_ref[0, g])
        gate = lambda br: jnp.concatenate([gs[r * 3 + br:r * 3 + br + 1, :] for r in range(G)], axis=1)
        o = gate(0) * o_cs[g] + gate(1) * o_s + gate(2) * o_w
        outs.extend(o[:, r * TQ:(r + 1) * TQ].T for r in range(G))
    o_ref[0] = jnp.concatenate(outs, axis=1).astype(BF16)


def _overlap_T(S):
    nbp = S // CMP_STRIDE
    n_slc = S // SLC_LEN
    cs = np.arange(nbp) * CMP_STRIDE
    ss = np.arange(n_slc) * SLC_LEN
    ov = np.clip(np.minimum(cs[:, None] + CMP_LEN, ss[None, :] + SLC_LEN)
                 - np.maximum(cs[:, None], ss[None, :]), 0, None) / CMP_LEN
    ov[nbp - 1:, :] = 0.0
    return jnp.asarray(ov.T, BF16)


def _nsa(qT, cmp_n, cmp_T, kn, ksE, vsT, vwT, gT, q_lo, q_len):
    B, _, S = qT.shape
    G2, TQ, HD = NSA_KV_HEADS, ATT_TQ, NSA_HD
    nbp = S // CMP_STRIDE
    ns = S // SLC_LEN
    ovT = _overlap_T(S)
    q_off = q_lo // TQ
    return pl.pallas_call(
        functools.partial(_nsa_kernel, q_off),
        grid=(B, q_len // TQ),
        in_specs=[
            pl.BlockSpec((1, NSA_WIDTH, TQ), lambda b, i: (b, 0, i + q_off)),
            pl.BlockSpec((1, 1, G2, nbp, HD), lambda b, i: (b, 0, 0, 0, 0)),
            pl.BlockSpec((1, 1, G2, HD, nbp), lambda b, i: (b, 1, 0, 0, 0)),
            pl.BlockSpec((1, G2, S, 2 * HD), lambda b, i: (b, 0, 0, 0)),
            pl.BlockSpec((1, G2, S // ATT_KT, HD, ATT_KT), lambda b, i: (b, 0, 0, 0, 0)),
            pl.BlockSpec((1, 1, G2, S, HD), lambda b, i: (b, 2, 0, 0, 0)),
            pl.BlockSpec((1, G2, S // LANES, HD, LANES), lambda b, i: (b, 0, 0, 0, 0)),
            pl.BlockSpec((1, G2, 16, TQ), lambda b, i: (b, 0, 0, i + q_off)),
            pl.BlockSpec((ns, nbp), lambda b, i: (0, 0)),
        ],
        out_specs=pl.BlockSpec((1, TQ, NSA_WIDTH), lambda b, i: (b, i, 0)),
        out_shape=jax.ShapeDtypeStruct((B, q_len, NSA_WIDTH), BF16),
        scratch_shapes=[pltpu.VMEM((G2, ns, TQ), F32)],
        compiler_params=_cparams(("parallel", "arbitrary")),
        name="nsa_attn",
    )(qT, cmp_n, cmp_T, ksE, vsT, kn, vwT, gT, ovT)


def _layernorm(t, g, b):
    mu = jnp.mean(t, axis=-1, keepdims=True)
    d = t - mu
    var = jnp.mean(d * d, axis=-1, keepdims=True)
    return d * lax.rsqrt(var + LN_EPS) * g + b


def _outproj_kernel(alpha, ohg_ref, onsa_ref, x_ref, w_ref, g_ref, b_ref, h_ref):
    mix = _nn(ohg_ref[...], w_ref[0:HG_WIDTH, :]) + _nn(onsa_ref[...], w_ref[HG_WIDTH:HG_WIDTH + NSA_WIDTH, :])
    h_ref[...] = _layernorm(alpha * x_ref[...] + mix, g_ref[...], b_ref[...])


def _outproj(alpha, o_hg, o_nsa, x2, w_out_l, g, b):
    n, D = x2.shape
    tm = OUT_TM
    row = lambda w: pl.BlockSpec((tm, w), lambda i: (i, 0))
    const = lambda shp: pl.BlockSpec(shp, lambda i: (0, 0))
    return pl.pallas_call(
        functools.partial(_outproj_kernel, alpha),
        grid=(n // tm,),
        in_specs=[row(HG_WIDTH), row(NSA_WIDTH), row(D), const(w_out_l.shape), const((1, D)), const((1, D))],
        out_specs=row(D),
        out_shape=jax.ShapeDtypeStruct((n, D), F32),
        compiler_params=_cparams(("parallel",)),
        name="out_proj_ln",
    )(o_hg, o_nsa, x2, w_out_l.astype(BF16), g.reshape(1, D).astype(F32), b.reshape(1, D).astype(F32))


def _topk_rows(s, k):
    R = s.shape[0]
    r_io = lax.broadcasted_iota(jnp.int32, s.shape, 0)
    vals, idxs = [], []
    for _ in range(k):
        m = jnp.max(s, axis=0, keepdims=True)
        idx = jnp.min(jnp.where(s == m, r_io, R), axis=0, keepdims=True)
        vals.append(m)
        idxs.append(idx)
        s = jnp.where(r_io == idx, -jnp.inf, s)
    return jnp.concatenate(vals, axis=0), jnp.concatenate(idxs, axis=0)


def _route_kernel(h_ref, wq_ref, k1_ref, k2_ref, e_ref, g_ref):
    K, half = PEER_TOPK, PEER_DQ // 2
    q = _nn(h_ref[...].astype(BF16), wq_ref[...]).astype(BF16)
    for h in range(PEER_HEADS):
        s1 = _nt(k1_ref[...], q[:, h * PEER_DQ:h * PEER_DQ + half])
        s2 = _nt(k2_ref[...], q[:, h * PEER_DQ + half:(h + 1) * PEER_DQ])
        v1, i1 = _topk_rows(s1, K)
        v2, i2 = _topk_rows(s2, K)
        sub = 8
        b_io = lax.broadcasted_iota(jnp.int32, (sub, 1), 0)
        cands = [v1[0:1, :] + v2]
        ids = [i1[0:1, :] * PEER_NKEYS + i2]
        for a in range(1, sub):
            cands.append(jnp.where(b_io < K // (a + 1), v1[a:a + 1, :] + v2[0:sub, :], -jnp.inf))
            ids.append(i1[a:a + 1, :] * PEER_NKEYS + i2[0:sub, :])
        cands.append(v1[sub:K, :] + v2[0:1, :])
        ids.append(i1[sub:K, :] * PEER_NKEYS + i2[0:1, :])
        cand = jnp.concatenate(cands, axis=0)
        cand_id = jnp.concatenate(ids, axis=0)
        r_io = lax.broadcasted_iota(jnp.int32, cand.shape, 0)
        top_s, top_e = [], []
        for _ in range(K):
            m = jnp.max(cand, axis=0, keepdims=True)
            hit = r_io == jnp.min(jnp.where(cand == m, r_io, cand.shape[0]), axis=0, keepdims=True)
            top_s.append(m)
            top_e.append(jnp.sum(jnp.where(hit, cand_id, 0), axis=0, keepdims=True))
            cand = jnp.where(hit, -jnp.inf, cand)
        top_s = jnp.concatenate(top_s, axis=0)
        e_ref[h * K:(h + 1) * K, :] = jnp.concatenate(top_e, axis=0)
        ex = jnp.exp(top_s - top_s[0:1, :])
        g_ref[h * K:(h + 1) * K, :] = ex / jnp.sum(ex, axis=0, keepdims=True)


def _route(h1, wq, k1, k2):
    n, D = h1.shape
    tr = ROUTE_TR
    hk = PEER_HEADS * PEER_TOPK
    const = lambda shp: pl.BlockSpec(shp, lambda i: (0, 0))
    return pl.pallas_call(
        _route_kernel,
        grid=(n // tr,),
        in_specs=[pl.BlockSpec((tr, D), lambda i: (i, 0)), const(wq.shape), const(k1.shape), const(k2.shape)],
        out_specs=(pl.BlockSpec((hk, tr), lambda i: (0, i)), pl.BlockSpec((hk, tr), lambda i: (0, i))),
        out_shape=(jax.ShapeDtypeStruct((hk, n), jnp.int32), jax.ShapeDtypeStruct((hk, n), F32)),
        compiler_params=_cparams(("parallel",)),
        name="peer_route",
    )(h1, wq.astype(BF16), k1.astype(BF16), k2.astype(BF16))


def _pack_rows(t):
    half = t.shape[1] // 2
    tb = t.astype(BF16)
    lo = lax.bitcast_convert_type(tb[:, :half], jnp.uint16).astype(jnp.uint32)
    hi = lax.bitcast_convert_type(tb[:, half:], jnp.uint16).astype(jnp.uint32)
    return lo | (hi << 16)


def _sc_peer(utab, vtab, idx, x):
    n = x.shape[0]
    w = utab.shape[1]
    hk = PEER_HEADS * PEER_TOPK
    L = SC_LANES
    T = n // SC_WORKERS
    uwin = SC_UWIN
    upt = hk // uwin
    vpt = hk // SC_VWIN
    per_u = vpt // upt
    vlead = SC_VSLOT // 2
    assert n % SC_WORKERS == 0 and T % SC_XB == 0 and uwin % L == 0
    assert upt % SC_USLOT == 0 and vpt % SC_VSLOT == 0 and vpt % upt == 0
    vsteps = T * vpt
    mesh = plsc.VectorSubcoreMesh(core_axis_name="c", subcore_axis_name="s")
    cp = dataclasses.replace(pltpu.CompilerParams(), needs_layout_passes=False)

    @pl.kernel(out_type=(jax.ShapeDtypeStruct((n * hk,), F32), jax.ShapeDtypeStruct((n * hk, w), utab.dtype)),
               mesh=mesh, compiler_params=cp,
               scratch_types=[pltpu.VMEM((T * hk,), jnp.int32),
                              pltpu.VMEM((SC_XB, 2 * w), F32),
                              pltpu.VMEM((SC_USLOT, uwin, w), utab.dtype),
                              pltpu.VMEM((SC_VSLOT, SC_VWIN, w), vtab.dtype),
                              pltpu.VMEM((L, L), F32),
                              pltpu.VMEM((SC_XB * hk,), F32),
                              pltpu.SemaphoreType.DMA((SC_USLOT,)),
                              pltpu.SemaphoreType.DMA((SC_VSLOT,)),
                              pltpu.SemaphoreType.DMA((SC_VSLOT,))])
    def peer(u_hbm, v_hbm, i_hbm, x_hbm, h_hbm, vg_hbm, idx_v, x_v, ubuf, vbuf, acc_s, h_v, usem, vsem, wsem):
        wid = lax.axis_index("c") * (SC_WORKERS // 2) + lax.axis_index("s")
        tok0 = wid * T
        row0 = tok0 * hk
        pltpu.sync_copy(i_hbm.at[pl.ds(row0, T * hk)], idx_v)

        def ufetch(t, j):
            slot = j % SC_USLOT
            return pltpu.make_async_copy(u_hbm.at[idx_v.at[pl.ds(t * hk + j * uwin, uwin)]], ubuf.at[slot], usem.at[slot])

        def vfetch(s, slot):
            return pltpu.make_async_copy(v_hbm.at[idx_v.at[pl.ds(s * SC_VWIN, SC_VWIN)]], vbuf.at[slot], vsem.at[slot])

        def vflush(s, slot):
            return pltpu.make_async_copy(vbuf.at[slot], vg_hbm.at[pl.ds(row0 + s * SC_VWIN, SC_VWIN)], wsem.at[slot])

        for j in range(SC_USLOT):
            ufetch(0, j).start()
        for j in range(vlead):
            vfetch(j, j).start()

        def vstep(s, slot):
            other = (slot + vlead) % SC_VSLOT
            vfetch(s, slot).wait()
            vflush(s, slot).start()

            @pl.when(s >= vlead)
            def _():
                vflush(s - vlead, other).wait()

            @pl.when(s + vlead < vsteps)
            def _():
                vfetch(s + vlead, other).start()

        lane = lax.iota(jnp.int32, L)

        @pl.loop(0, T // SC_XB)
        def _(tb):
            pltpu.sync_copy(x_hbm.at[pl.ds(pl.multiple_of(tok0 + tb * SC_XB, SC_XB), SC_XB)], x_v)

            @pl.loop(0, SC_XB)
            def _(tt):
                t = tb * SC_XB + tt
                for j in range(upt):
                    for i in range(per_u):
                        vstep(t * vpt + j * per_u + i, (j * per_u + i) % SC_VSLOT)
                    ufetch(t, j).wait()

                    def dot_body(jj, accs):
                        off = pl.multiple_of(jj * L, L)
                        xlo = x_v[tt, pl.ds(off, L)]
                        xhi = x_v[tt, pl.ds(w + off, L)]
                        out = []
                        for r in range(uwin):
                            wv = ubuf[j % SC_USLOT, r, pl.ds(off, L)]
                            lo = plsc.bitcast(wv << 16, F32)
                            hi = plsc.bitcast(wv & jnp.uint32(0xFFFF0000), F32)
                            out.append(accs[r] + lo * xlo + hi * xhi)
                        return tuple(out)

                    accs = lax.fori_loop(0, w // L, dot_body, tuple(jnp.zeros((L,), F32) for _ in range(uwin)))

                    if j + SC_USLOT < upt:
                        ufetch(t, j + SC_USLOT).start()
                    else:
                        @pl.when(t + 1 < T)
                        def _():
                            ufetch(t + 1, j + SC_USLOT - upt).start()

                    for g in range(uwin // L):
                        for r in range(L):
                            acc_s[r, :] = accs[g * L + r]
                        tot = jnp.zeros((L,), F32)
                        for c in range(L):
                            tot = tot + plsc.load_gather(acc_s, [lane, jnp.full((L,), c, jnp.int32)])
                        h_v[pl.ds(tt * hk + j * uwin + g * L, L)] = tot

            pltpu.sync_copy(h_v, h_hbm.at[pl.ds(pl.multiple_of(row0 + tb * SC_XB * hk, SC_XB * hk), SC_XB * hk)])

        for s in range(vsteps - vlead, vsteps):
            vflush(s, s % SC_VSLOT).wait()

    return peer(utab, vtab, idx, x)


def _unpack(w):
    lo = lax.bitcast_convert_type(w << 16, F32)
    hi = lax.bitcast_convert_type(w & jnp.uint32(0xFFFF0000), F32)
    return lo, hi


def _mix_kernel(alpha, h_ref, s_ref, gate_ref, vg_ref, g_ref, b_ref, o_ref):
    TT, HK = MIX_TT, PEER_HEADS * PEER_TOPK
    x = h_ref[...]
    wrow = gate_ref[...] * jax.nn.gelu(s_ref[...])
    eye = lax.broadcasted_iota(jnp.int32, (HK, HK), 0) == lax.broadcasted_iota(jnp.int32, (HK, HK), 1)
    rows = []
    for t in range(TT):
        wcol = jnp.sum(jnp.where(eye, jnp.broadcast_to(wrow[t:t + 1, :], (HK, HK)), 0.0),
                       axis=1, keepdims=True)
        vlo, vhi = _unpack(vg_ref[t * HK:(t + 1) * HK, :])
        rows.append(jnp.concatenate([jnp.sum(wcol * vlo, axis=0, keepdims=True),
                                     jnp.sum(wcol * vhi, axis=0, keepdims=True)], axis=1))
    ffn = jnp.concatenate(rows, axis=0)
    o_ref[...] = _layernorm(alpha * x + ffn, g_ref[...], b_ref[...])


def _mix(alpha, h1, scores, gates, vg, g, b):
    n, D = h1.shape
    tt = MIX_TT
    hk = PEER_HEADS * PEER_TOPK
    const = lambda shp: pl.BlockSpec(shp, lambda i: (0, 0))
    return pl.pallas_call(
        functools.partial(_mix_kernel, alpha),
        grid=(n // tt,),
        in_specs=[pl.BlockSpec((tt, D), lambda i: (i, 0)), pl.BlockSpec((tt, hk), lambda i: (i, 0)),
                  pl.BlockSpec((tt, hk), lambda i: (i, 0)), pl.BlockSpec((tt * hk, D // 2), lambda i: (i, 0)),
                  const((1, D)), const((1, D))],
        out_specs=pl.BlockSpec((tt, D), lambda i: (i, 0)),
        out_shape=jax.ShapeDtypeStruct((n, D), F32),
        compiler_params=_cparams(("parallel",)),
        name="peer_mix_ln",
    )(h1, scores, gates, vg, g.reshape(1, D).astype(F32), b.reshape(1, D).astype(F32))


PEER_CHUNK = 2048
SEQ_PIECE = 2048


def _peer_ffn_ln(alpha, h1, wq, k1, k2, up, vp, g, b):
    eT, gT = _route(h1, wq, k1, k2)
    return _peer_experts_ln(alpha, h1, eT, gT, up, vp, g, b)


def _peer_experts_ln(alpha, h1, eT, gT, up, vp, g, b):
    n, D = h1.shape
    hk = PEER_HEADS * PEER_TOPK
    experts = eT.T
    gates = gT.T
    ch = min(PEER_CHUNK, n)
    outs = []
    for c in range(n // ch):
        hc = h1[c * ch:(c + 1) * ch]
        scores, vg = _sc_peer(up, vp, experts[c * ch:(c + 1) * ch].reshape(ch * hk), hc)
        outs.append(_mix(alpha, hc, scores.reshape(ch, hk), gates[c * ch:(c + 1) * ch], vg, g, b))
    return jnp.concatenate(outs, axis=0)


def kernel(x, w_in, w_out, hg_lb, hg_norm_g, cmpk_pe, cmpk_w1, cmpk_b1, cmpk_w2, cmpk_b2,
           cmpv_pe, cmpv_w1, cmpv_b1, cmpv_w2, cmpv_b2, ln1_g, ln1_b,
           peer_wq, peer_k1, peer_k2, peer_u, peer_v, ln2_g, ln2_b):
    B, S, D = x.shape
    depth = w_in.shape[0]
    alpha = (2.0 * depth) ** 0.25
    h = x
    for l in range(depth):
        up, vp = _pack_rows(peer_u[l]), _pack_rows(peer_v[l])
        cmpk = (cmpk_pe[l], cmpk_w1[l], cmpk_b1[l], cmpk_w2[l], cmpk_b2[l])
        cmpv = (cmpv_pe[l], cmpv_w1[l], cmpv_b1[l], cmpv_w2[l], cmpv_b2[l])
        outs = []
        for bi in range(B):
            hb = h[bi:bi + 1]
            hg, qT, kn, ksE, vsT, vwT, gT = _project(hb, w_in[l])
            o_hg = _hgrn2(hg, hg_lb, hg_norm_g[l], l)
            cmp_n, cmp_T = _compress(kn, cmpk, cmpv)
            piece = min(SEQ_PIECE, S)
            for lo in range(0, S, piece):
                o_nsa = _nsa(qT, cmp_n, cmp_T, kn, ksE, vsT, vwT, gT, lo, piece)
                h1 = _outproj(alpha, o_hg[0, lo:lo + piece], o_nsa.reshape(piece, NSA_WIDTH),
                              hb[0, lo:lo + piece], w_out[l], ln1_g[l], ln1_b[l])
                eT, egT = _route(h1, peer_wq[l], peer_k1[l], peer_k2[l])
                qT, eT, egT = lax.optimization_barrier((qT, eT, egT))
                outs.append(_peer_experts_ln(alpha, h1, eT, egT, up, vp, ln2_g[l], ln2_b[l]))
        h = jnp.concatenate(outs, axis=0).reshape(B, S, D)
    return h
```

```python
import dataclasses
import functools
import math

import jax
import jax.numpy as jnp
import numpy as np
from jax import lax
from jax.experimental import pallas as pl
from jax.experimental.pallas import tpu as pltpu
from jax.experimental.pallas import tpu_sc as plsc

F32 = jnp.float32
BF16 = jnp.bfloat16

HG_HEADS = 4
HG_DK = 128
HG_DV = 128
HG_CHUNK = 64
HG_SUB = 16
HG_WIDTH = HG_HEADS * HG_DV
NSA_HEADS = 8
NSA_KV_HEADS = 2
NSA_GROUP = NSA_HEADS // NSA_KV_HEADS
NSA_HD = 64
NSA_WIDTH = NSA_HEADS * NSA_HD
CMP_LEN = 32
CMP_STRIDE = 16
CMP_HIDDEN = 256
SLC_LEN = 64
SLC_TOPK = 16
WIN = 512
FORCE_SCORE = 1.0e4
PEER_HEADS = 8
PEER_DQ = 256
PEER_NKEYS = 128
PEER_TOPK = 16
LN_EPS = 1e-5
NEG = -1e30
LOG2E = math.log2(math.e)

LANES = 128
PROJ_TM = 512
HG_TC = 256
ATT_TQ = 128
ATT_KT = 512
ATT_KS = 1024
OUT_TM = 512
ROUTE_TR = 256
MIX_TT = 16
SC_WORKERS = 32
SC_LANES = 16
SC_UWIN = 16
SC_USLOT = 4
SC_VWIN = 16
SC_VSLOT = 8
SC_XB = 8
VMEM_LIMIT = 56 * 1024 * 1024


def _cparams(sem):
    return pltpu.CompilerParams(dimension_semantics=sem, vmem_limit_bytes=VMEM_LIMIT)


def _nt(a, b):
    return lax.dot_general(a, b, (((1,), (1,)), ((), ())), preferred_element_type=F32)


def _tn(a, b):
    return lax.dot_general(a, b, (((0,), (0,)), ((), ())), preferred_element_type=F32)


def _nn(a, b):
    return jnp.dot(a, b, preferred_element_type=F32)


def _proj_kernel(x_ref, whg_ref, wqT_ref, wkn_ref, wvT_ref, wgT_ref,
                 hg_ref, qT_ref, kn_ref, ksE_ref, vsT_ref, vwT_ref, gT_ref):
    xb = x_ref[0].astype(BF16)
    hg_ref[0] = _nn(xb, whg_ref[...])
    qT_ref[0] = (_nt(wqT_ref[...], xb) * (NSA_HD ** -0.5 * LOG2E)).astype(BF16)
    kn = _nn(xb, wkn_ref[...])
    for j in range(3):
        for g in range(NSA_KV_HEADS):
            lo = j * 2 * NSA_HD + g * NSA_HD
            kn_ref[0, j, g] = kn[:, lo:lo + NSA_HD].astype(BF16)
    row = pl.program_id(1) * kn.shape[0] + lax.broadcasted_iota(jnp.int32, (kn.shape[0], NSA_HD), 0)
    blk = (row >> int(math.log2(SLC_LEN))) & (ATT_KS // SLC_LEN - 1)
    onehot = (lax.broadcasted_iota(jnp.int32, (kn.shape[0], NSA_HD), 1) == blk).astype(F32)
    for g in range(NSA_KV_HEADS):
        lo = 3 * 2 * NSA_HD + g * NSA_HD
        ksE_ref[0, g] = jnp.concatenate([kn[:, lo:lo + NSA_HD], onehot], axis=1).astype(BF16)
    vT = _nt(wvT_ref[...], xb).astype(BF16)
    tm = vT.shape[1]
    for g in range(NSA_KV_HEADS):
        for c in range(tm // ATT_KT):
            vsT_ref[0, g, c] = vT[g * NSA_HD:(g + 1) * NSA_HD, c * ATT_KT:(c + 1) * ATT_KT]
        for c in range(tm // LANES):
            vwT_ref[0, g, c] = vT[(2 + g) * NSA_HD:(3 + g) * NSA_HD, c * LANES:(c + 1) * LANES]
    gT = _nt(wgT_ref[...], xb)
    gT_ref[0, 0] = gT[0:16]
    gT_ref[0, 1] = gT[16:32]


def _project(x, w_in_l):
    B, S, D = x.shape
    tm = PROJ_TM
    o = np.cumsum((0, 512, 512, 512, 512, 512, 128, 128, 128, 128, 128, 128, 24))
    wb = w_in_l.astype(BF16)
    whg = wb[:, o[0]:o[4]]
    wqT = wb[:, o[4]:o[5]].T
    wkn = jnp.concatenate([wb[:, o[5]:o[6]], wb[:, o[6]:o[7]], wb[:, o[9]:o[10]], wb[:, o[7]:o[8]]], axis=1)
    wvT = jnp.concatenate([wb[:, o[8]:o[9]], wb[:, o[10]:o[11]]], axis=1).T
    wg = wb[:, o[11]:o[12]].reshape(D, NSA_KV_HEADS, NSA_GROUP * 3)
    wgT = jnp.pad(wg, ((0, 0), (0, 0), (0, 16 - NSA_GROUP * 3))).reshape(D, 32).T
    const = lambda shp: pl.BlockSpec(shp, lambda b, i: (0,) * len(shp))
    out_shape = (
        jax.ShapeDtypeStruct((B, S, 4 * 512), F32),
        jax.ShapeDtypeStruct((B, NSA_WIDTH, S), BF16),
        jax.ShapeDtypeStruct((B, 3, NSA_KV_HEADS, S, NSA_HD), BF16),
        jax.ShapeDtypeStruct((B, NSA_KV_HEADS, S, 2 * NSA_HD), BF16),
        jax.ShapeDtypeStruct((B, NSA_KV_HEADS, S // ATT_KT, NSA_HD, ATT_KT), BF16),
        jax.ShapeDtypeStruct((B, NSA_KV_HEADS, S // LANES, NSA_HD, LANES), BF16),
        jax.ShapeDtypeStruct((B, NSA_KV_HEADS, 16, S), F32),
    )
    out_specs = (
        pl.BlockSpec((1, tm, 2048), lambda b, i: (b, i, 0)),
        pl.BlockSpec((1, NSA_WIDTH, tm), lambda b, i: (b, 0, i)),
        pl.BlockSpec((1, 3, NSA_KV_HEADS, tm, NSA_HD), lambda b, i: (b, 0, 0, i, 0)),
        pl.BlockSpec((1, NSA_KV_HEADS, tm, 2 * NSA_HD), lambda b, i: (b, 0, i, 0)),
        pl.BlockSpec((1, NSA_KV_HEADS, tm // ATT_KT, NSA_HD, ATT_KT), lambda b, i: (b, 0, i, 0, 0)),
        pl.BlockSpec((1, NSA_KV_HEADS, tm // LANES, NSA_HD, LANES), lambda b, i: (b, 0, i, 0, 0)),
        pl.BlockSpec((1, NSA_KV_HEADS, 16, tm), lambda b, i: (b, 0, 0, i)),
    )
    return pl.pallas_call(
        _proj_kernel,
        grid=(B, S // tm),
        in_specs=[pl.BlockSpec((1, tm, D), lambda b, i: (b, i, 0)),
                  const(whg.shape), const(wqT.shape), const(wkn.shape), const(wvT.shape), const(wgT.shape)],
        out_specs=out_specs,
        out_shape=out_shape,
        compiler_params=_cparams(("parallel", "arbitrary")),
        name="in_proj",
    )(x, whg, wqT, wkn, wvT, wgT)


def _split3(a):
    hi = a.astype(BF16)
    r = a - hi.astype(F32)
    mid = r.astype(BF16)
    lo = (r - mid.astype(F32)).astype(BF16)
    return hi, mid, lo


def _hgrn_kernel(layer, hg_ref, lb_ref, ng_ref, o_ref, st_ref):
    C, SB, H, DK = HG_CHUNK, HG_SUB, HG_HEADS, HG_DK
    nsb = C // SB

    @pl.when(pl.program_id(1) == 0)
    def _():
        st_ref[...] = jnp.zeros_like(st_ref)

    lbp = lb_ref[...]
    e = jnp.exp(lbp - jnp.max(lbp, axis=0, keepdims=True))
    lb = jnp.sum(e[:layer + 1], axis=0, keepdims=True) / jnp.sum(e, axis=0, keepdims=True)

    ri = lax.broadcasted_iota(jnp.int32, (C, C), 0)
    ci = lax.broadcasted_iota(jnp.int32, (C, C), 1)
    tril = (ci <= ri).astype(BF16)
    t_sub = lax.broadcasted_iota(jnp.int32, (SB, H * DK), 0)
    lane16 = lax.broadcasted_iota(jnp.int32, (SB, SB), 1)
    ng = ng_ref[...]

    def chunk(c, carry):
        r0 = pl.multiple_of(c * C, C)
        blk = hg_ref[0, pl.ds(r0, C), :]
        q = blk[:, 0:512]
        f = lb + (1.0 - lb) * jax.nn.sigmoid(blk[:, 512:1024])
        lf = jnp.log(f)
        k = 1.0 - f
        v = blk[:, 1024:1536]
        gate = blk[:, 1536:2048]
        hi, mid, lo = _split3(lf)
        b = _nn(tril, hi) + _nn(tril, mid) + _nn(tril, lo)
        b_last = b[C - 1:C, :]
        qe = (q * jnp.exp(b)).astype(BF16)
        ks = (k * jnp.exp(b_last - b)).astype(BF16)
        dec = jnp.exp(b_last)
        vb = v.astype(BF16)
        rblk = jnp.concatenate([jnp.broadcast_to(b[i * SB:i * SB + 1, :], (SB, H * DK)) for i in range(nsb)], axis=0)
        qn = (q * jnp.exp(b - rblk)).astype(BF16)
        outs = []
        for h in range(H):
            sl = slice(h * DK, (h + 1) * DK)
            stT = st_ref[h]
            o_h = _nt(qe[:, sl], stT.astype(BF16))
            rows = []
            for i in range(nsb):
                rs = slice(i * SB, (i + 1) * SB)
                acc = jnp.zeros((SB, HG_DV), F32)
                if i > 0:
                    ke = (k[0:i * SB, sl] * jnp.exp(b[i * SB:i * SB + 1, sl] - b[0:i * SB, sl])).astype(BF16)
                    a_off = _nt(qn[rs, sl], ke)
                    acc = acc + _nn(a_off.astype(BF16), vb[0:i * SB, sl])
                qi, ki, bi = q[rs, sl], k[rs, sl], b[rs, sl]
                a_d = jnp.zeros((SB, SB), F32)
                for s in range(SB):
                    w = qi * ki[s:s + 1, :] * jnp.exp(jnp.minimum(bi - bi[s:s + 1, :], 0.0))
                    col = jnp.sum(w, axis=1, keepdims=True)
                    a_d = a_d + jnp.where((lane16 == s) & (t_sub[:, 0:SB] >= s), col, 0.0)
                acc = acc + _nn(a_d.astype(BF16), vb[rs, sl])
                rows.append(acc)
            o_h = o_h + jnp.concatenate(rows, axis=0)
            st_ref[h] = dec[:, sl] * stT + _tn(vb[:, sl], ks[:, sl])
            o_h = o_h * lax.rsqrt(jnp.mean(o_h * o_h, axis=1, keepdims=True) + LN_EPS) * ng
            outs.append(o_h)
        o = jnp.concatenate(outs, axis=1) * jax.nn.silu(gate)
        o_ref[0, pl.ds(r0, C), :] = o.astype(BF16)
        return carry

    lax.fori_loop(0, hg_ref.shape[1] // C, chunk, 0)


def _hgrn2(hg, hg_lb, norm_g_l, layer):
    B, S, _ = hg.shape
    tc = HG_TC
    return pl.pallas_call(
        functools.partial(_hgrn_kernel, layer),
        grid=(B, S // tc),
        in_specs=[pl.BlockSpec((1, tc, 2048), lambda b, i: (b, i, 0)),
                  pl.BlockSpec(hg_lb.shape, lambda b, i: (0, 0)),
                  pl.BlockSpec((1, HG_DV), lambda b, i: (0, 0))],
        out_specs=pl.BlockSpec((1, tc, HG_WIDTH), lambda b, i: (b, i, 0)),
        out_shape=jax.ShapeDtypeStruct((B, S, HG_WIDTH), BF16),
        scratch_shapes=[pltpu.VMEM((HG_HEADS, HG_DV, HG_DK), F32)],
        compiler_params=_cparams(("parallel", "arbitrary")),
        name="hgrn2",
    )(hg, hg_lb.astype(F32), norm_g_l.reshape(1, HG_DV).astype(F32))


def _compress_kernel(c_ref, pe_ref, w1_ref, b1_ref, w2_ref, w2T_ref, b2_ref, b2T_ref, o_ref, oT_ref):
    half = CMP_STRIDE * NSA_HD
    c = c_ref[0, 0, 0].astype(F32)
    pe = pe_ref[0]
    ca = (c + pe[:, 0:half]).astype(BF16)
    cb = (c + pe[:, half:2 * half]).astype(BF16)
    pa = _nn(ca, w1_ref[0, 0:half, :])
    pb = _nn(cb, w1_ref[0, half:2 * half, :])
    nbp = pa.shape[0]
    h = pa + pltpu.roll(pb, nbp - 1, 0) + b1_ref[0]
    h = jax.nn.gelu(h).astype(BF16)
    o_ref[0, 0, 0] = (_nn(h, w2_ref[0]) + b2_ref[0]).astype(BF16)
    oT_ref[0, 0, 0] = (_nt(w2T_ref[0], h) + b2T_ref[0]).astype(BF16)


def _compress(kn, cmpk, cmpv):
    B, _, G, S, HD = kn.shape
    nbp = S // CMP_STRIDE
    c = kn[:, 0:2].reshape(B, 2, G, nbp, CMP_STRIDE * HD)
    stack = lambda a, b, f: jnp.stack([f(a), f(b)], axis=0)
    pe = stack(cmpk[0], cmpv[0], lambda t: t.reshape(1, CMP_LEN * HD).astype(F32))
    w1 = stack(cmpk[1], cmpv[1], lambda t: t.astype(BF16))
    b1 = stack(cmpk[2], cmpv[2], lambda t: t.reshape(1, CMP_HIDDEN).astype(F32))
    w2 = stack(cmpk[3], cmpv[3], lambda t: t.astype(BF16))
    w2T = stack(cmpk[3], cmpv[3], lambda t: t.astype(BF16).T)
    b2 = stack(cmpk[4], cmpv[4], lambda t: t.reshape(1, HD).astype(F32))
    b2T = stack(cmpk[4], cmpv[4], lambda t: t.reshape(HD, 1).astype(F32))
    wspec = lambda a: pl.BlockSpec((1,) + a.shape[1:], lambda b, t, g: (t,) + (0,) * (a.ndim - 1))
    return pl.pallas_call(
        _compress_kernel,
        grid=(B, 2, G),
        in_specs=[pl.BlockSpec((1, 1, 1, nbp, CMP_STRIDE * HD), lambda b, t, g: (b, t, g, 0, 0)),
                  wspec(pe), wspec(w1), wspec(b1), wspec(w2), wspec(w2T), wspec(b2), wspec(b2T)],
        out_specs=(pl.BlockSpec((1, 1, 1, nbp, HD), lambda b, t, g: (b, t, g, 0, 0)),
                   pl.BlockSpec((1, 1, 1, HD, nbp), lambda b, t, g: (b, t, g, 0, 0))),
        out_shape=(jax.ShapeDtypeStruct((B, 2, G, nbp, HD), BF16),
                   jax.ShapeDtypeStruct((B, 2, G, HD, nbp), BF16)),
        compiler_params=_cparams(("parallel", "parallel", "parallel")),
        name="kv_compress",
    )(c, pe, w1, b1, w2, w2T, b2, b2T)


def _topk_mask_rows(score, k):
    R = score.shape[0]
    r_io = lax.broadcasted_iota(jnp.int32, score.shape, 0)
    s = score
    for _ in range(k):
        m = jnp.max(s, axis=0, keepdims=True)
        idx = jnp.min(jnp.where(s == m, r_io, R), axis=0, keepdims=True)
        s = jnp.where(r_io == idx, -jnp.inf, s)
    return (s == -jnp.inf).astype(F32)


def _with_ones_row(vT):
    pad = (lax.broadcasted_iota(jnp.int32, (16, vT.shape[1]), 0) == 0).astype(BF16)
    return jnp.concatenate([vT, pad], axis=0)


def _nsa_kernel(q_off, qT_ref, kc_ref, vcT_ref, ks_ref, vsT_ref, kw_ref, vwT_ref, gT_ref, ovT_ref,
                o_ref, sel_ref):
    TQ, G, HD, KT, NG = ATT_TQ, NSA_GROUP, NSA_HD, ATT_KT, NSA_KV_HEADS
    qi = pl.program_id(1) + q_off
    t0 = qi * TQ
    pos1 = t0 + lax.broadcasted_iota(jnp.int32, (1, TQ), 1)
    pos4 = jnp.concatenate([pos1] * G, axis=1)
    nbp = kc_ref.shape[3]
    ns = ovT_ref.shape[0]
    n_io = lax.broadcasted_iota(jnp.int32, (nbp, 1), 0)
    mask_c = (n_io * CMP_STRIDE + (CMP_LEN - 1)) <= pos4
    j_io = lax.broadcasted_iota(jnp.int32, (ns, 1), 0)
    cur = pos1 >> int(math.log2(SLC_LEN))
    forced = (j_io == 0) | (j_io == cur) | (j_io == cur - 1)
    causal_blk = j_io * SLC_LEN <= pos1

    q4s, o_cs = [], []
    for g in range(NG):
        qT = qT_ref[0, g * G * HD:(g + 1) * G * HD, :]
        q4 = jnp.concatenate([qT[r * HD:(r + 1) * HD, :] for r in range(G)], axis=1)
        s_c = jnp.where(mask_c, _nn(kc_ref[0, 0, g], q4), NEG)
        p_c = jnp.where(mask_c, jnp.exp2(s_c - jnp.max(s_c, axis=0, keepdims=True)), 0.0)
        p_c = p_c / jnp.maximum(jnp.sum(p_c, axis=0, keepdims=True), 1e-30)
        o_cs.append(_nn(vcT_ref[0, 0, g], p_c.astype(BF16)))
        p_sum = p_c[:, 0:TQ]
        for r in range(1, G):
            p_sum = p_sum + p_c[:, r * TQ:(r + 1) * TQ]
        imp = _nn(ovT_ref[...], p_sum.astype(BF16))
        score = jnp.where(forced, FORCE_SCORE, jnp.where(causal_blk, imp, -1.0))
        sel_ref[g] = (_topk_mask_rows(score, min(SLC_TOPK, ns)) - 1.0) * (-NEG)
        q4s.append(q4)

    KS = ATT_KS
    bps = KS // SLC_LEN
    k_io = lax.broadcasted_iota(jnp.int32, (KS, 1), 0)
    zrows = jnp.zeros((HD - bps, G * TQ), BF16)

    def sel_step(ks_i, carry, diagonal):
        k0 = pl.multiple_of(ks_i * KS, KS)
        out = []
        for g in range(NG):
            m, acc = carry[g]
            bias = sel_ref[g, pl.ds(pl.multiple_of(ks_i * bps, bps), bps), :].astype(BF16)
            rhs = jnp.concatenate([q4s[g], jnp.concatenate([bias] * G, axis=1), zrows], axis=0)
            s = _nn(ks_ref[0, g, pl.ds(k0, KS), :], rhs)
            if diagonal:
                s = jnp.where((k0 + k_io) <= pos4, s, NEG)
            m_new = jnp.maximum(m, jnp.max(s, axis=0, keepdims=True))
            p = jnp.exp2(s - m_new).astype(BF16)
            acc = jnp.exp2(m - m_new) * acc
            for c in range(KS // KT):
                acc = acc + _nn(_with_ones_row(vsT_ref[0, g, ks_i * (KS // KT) + c]), p[c * KT:(c + 1) * KT])
            out.append((m_new, acc))
        return tuple(out)

    n_ks = (t0 + TQ - 1) // KS + 1
    init = tuple((jnp.full((1, G * TQ), NEG, F32), jnp.zeros((HD + 16, G * TQ), F32)) for _ in range(NG))
    carry = lax.fori_loop(0, n_ks - 1, lambda i, c: sel_step(i, c, False), init)
    carry = sel_step(n_ks - 1, carry, True)

    nwt = WIN // TQ + 1
    r_io = lax.broadcasted_iota(jnp.int32, (TQ, 1), 0)
    idxs = [qi - (nwt - 1) + i for i in range(nwt)]
    kpos = jnp.concatenate([idx * TQ + r_io for idx in idxs], axis=0)
    delta = pos4 - kpos
    mask_w = (delta >= 0) & (delta < WIN) & (kpos >= 0)
    outs = []
    for g in range(NG):
        acc_s = carry[g][1]
        o_s = acc_s[0:HD] / jnp.maximum(acc_s[HD:HD + 1], 1e-30)
        kw_t = [kw_ref[0, 0, g, pl.ds(pl.multiple_of(jnp.maximum(idx, 0) * TQ, TQ), TQ), :] for idx in idxs]
        vw_t = [vwT_ref[0, g, jnp.maximum(idx, 0)] for idx in idxs]
        s_w = jnp.where(mask_w, _nn(jnp.concatenate(kw_t, axis=0), q4s[g]), NEG)
        p_w = jnp.exp2(s_w - jnp.max(s_w, axis=0, keepdims=True)).astype(BF16)
        acc_w = _nn(_with_ones_row(jnp.concatenate(vw_t, axis=1)), p_w)
        o_w = acc_w[0:HD] / jnp.maximum(acc_w[HD:HD + 1], 1e-30)
        gs = jax.nn.sigmoid(gT_ref[0, g])
        gate = lambda br: jnp.concatenate([gs[r * 3 + br:r * 3 + br + 1, :] for r in range(G)], axis=1)
        o = gate(0) * o_cs[g] + gate(1) * o_s + gate(2) * o_w
        outs.extend(o[:, r * TQ:(r + 1) * TQ].T for r in range(G))
    o_ref[0] = jnp.concatenate(outs, axis=1).astype(BF16)


def _overlap_T(S):
    nbp = S // CMP_STRIDE
    n_slc = S // SLC_LEN
    cs = np.arange(nbp) * CMP_STRIDE
    ss = np.arange(n_slc) * SLC_LEN
    ov = np.clip(np.minimum(cs[:, None] + CMP_LEN, ss[None, :] + SLC_LEN)
                 - np.maximum(cs[:, None], ss[None, :]), 0, None) / CMP_LEN
    ov[nbp - 1:, :] = 0.0
    return jnp.asarray(ov.T, BF16)


def _nsa(qT, cmp_n, cmp_T, kn, ksE, vsT, vwT, gT, q_lo, q_len):
    B, _, S = qT.shape
    G2, TQ, HD = NSA_KV_HEADS, ATT_TQ, NSA_HD
    nbp = S // CMP_STRIDE
    ns = S // SLC_LEN
    ovT = _overlap_T(S)
    q_off = q_lo // TQ
    return pl.pallas_call(
        functools.partial(_nsa_kernel, q_off),
        grid=(B, q_len // TQ),
        in_specs=[
            pl.BlockSpec((1, NSA_WIDTH, TQ), lambda b, i: (b, 0, i + q_off)),
            pl.BlockSpec((1, 1, G2, nbp, HD), lambda b, i: (b, 0, 0, 0, 0)),
            pl.BlockSpec((1, 1, G2, HD, nbp), lambda b, i: (b, 1, 0, 0, 0)),
            pl.BlockSpec((1, G2, S, 2 * HD), lambda b, i: (b, 0, 0, 0)),
            pl.BlockSpec((1, G2, S // ATT_KT, HD, ATT_KT), lambda b, i: (b, 0, 0, 0, 0)),
            pl.BlockSpec((1, 1, G2, S, HD), lambda b, i: (b, 2, 0, 0, 0)),
            pl.BlockSpec((1, G2, S // LANES, HD, LANES), lambda b, i: (b, 0, 0, 0, 0)),
            pl.BlockSpec((1, G2, 16, TQ), lambda b, i: (b, 0, 0, i + q_off)),
            pl.BlockSpec((ns, nbp), lambda b, i: (0, 0)),
        ],
        out_specs=pl.BlockSpec((1, TQ, NSA_WIDTH), lambda b, i: (b, i, 0)),
        out_shape=jax.ShapeDtypeStruct((B, q_len, NSA_WIDTH), BF16),
        scratch_shapes=[pltpu.VMEM((G2, ns, TQ), F32)],
        compiler_params=_cparams(("parallel", "arbitrary")),
        name="nsa_attn",
    )(qT, cmp_n, cmp_T, ksE, vsT, kn, vwT, gT, ovT)


def _layernorm(t, g, b):
    mu = jnp.mean(t, axis=-1, keepdims=True)
    d = t - mu
    var = jnp.mean(d * d, axis=-1, keepdims=True)
    return d * lax.rsqrt(var + LN_EPS) * g + b


def _outproj_kernel(alpha, ohg_ref, onsa_ref, x_ref, w_ref, g_ref, b_ref, h_ref):
    mix = _nn(ohg_ref[...], w_ref[0:HG_WIDTH, :]) + _nn(onsa_ref[...], w_ref[HG_WIDTH:HG_WIDTH + NSA_WIDTH, :])
    h_ref[...] = _layernorm(alpha * x_ref[...] + mix, g_ref[...], b_ref[...])


def _outproj(alpha, o_hg, o_nsa, x2, w_out_l, g, b):
    n, D = x2.shape
    tm = OUT_TM
    row = lambda w: pl.BlockSpec((tm, w), lambda i: (i, 0))
    const = lambda shp: pl.BlockSpec(shp, lambda i: (0, 0))
    return pl.pallas_call(
        functools.partial(_outproj_kernel, alpha),
        grid=(n // tm,),
        in_specs=[row(HG_WIDTH), row(NSA_WIDTH), row(D), const(w_out_l.shape), const((1, D)), const((1, D))],
        out_specs=row(D),
        out_shape=jax.ShapeDtypeStruct((n, D), F32),
        compiler_params=_cparams(("parallel",)),
        name="out_proj_ln",
    )(o_hg, o_nsa, x2, w_out_l.astype(BF16), g.reshape(1, D).astype(F32), b.reshape(1, D).astype(F32))


def _topk_rows(s, k):
    R = s.shape[0]
    r_io = lax.broadcasted_iota(jnp.int32, s.shape, 0)
    vals, idxs = [], []
    for _ in range(k):
        m = jnp.max(s, axis=0, keepdims=True)
        idx = jnp.min(jnp.where(s == m, r_io, R), axis=0, keepdims=True)
        vals.append(m)
        idxs.append(idx)
        s = jnp.where(r_io == idx, -jnp.inf, s)
    return jnp.concatenate(vals, axis=0), jnp.concatenate(idxs, axis=0)


def _route_kernel(h_ref, wq_ref, k1_ref, k2_ref, e_ref, g_ref):
    K, half = PEER_TOPK, PEER_DQ // 2
    q = _nn(h_ref[...].astype(BF16), wq_ref[...]).astype(BF16)
    for h in range(PEER_HEADS):
        s1 = _nt(k1_ref[...], q[:, h * PEER_DQ:h * PEER_DQ + half])
        s2 = _nt(k2_ref[...], q[:, h * PEER_DQ + half:(h + 1) * PEER_DQ])
        v1, i1 = _topk_rows(s1, K)
        v2, i2 = _topk_rows(s2, K)
        sub = 8
        b_io = lax.broadcasted_iota(jnp.int32, (sub, 1), 0)
        cands = [v1[0:1, :] + v2]
        ids = [i1[0:1, :] * PEER_NKEYS + i2]
        for a in range(1, sub):
            cands.append(jnp.where(b_io < K // (a + 1), v1[a:a + 1, :] + v2[0:sub, :], -jnp.inf))
            ids.append(i1[a:a + 1, :] * PEER_NKEYS + i2[0:sub, :])
        cands.append(v1[sub:K, :] + v2[0:1, :])
        ids.append(i1[sub:K, :] * PEER_NKEYS + i2[0:1, :])
        cand = jnp.concatenate(cands, axis=0)
        cand_id = jnp.concatenate(ids, axis=0)
        r_io = lax.broadcasted_iota(jnp.int32, cand.shape, 0)
        top_s, top_e = [], []
        for _ in range(K):
            m = jnp.max(cand, axis=0, keepdims=True)
            hit = r_io == jnp.min(jnp.where(cand == m, r_io, cand.shape[0]), axis=0, keepdims=True)
            top_s.append(m)
            top_e.append(jnp.sum(jnp.where(hit, cand_id, 0), axis=0, keepdims=True))
            cand = jnp.where(hit, -jnp.inf, cand)
        top_s = jnp.concatenate(top_s, axis=0)
        e_ref[h * K:(h + 1) * K, :] = jnp.concatenate(top_e, axis=0)
        ex = jnp.exp(top_s - top_s[0:1, :])
        g_ref[h * K:(h + 1) * K, :] = ex / jnp.sum(ex, axis=0, keepdims=True)


def _route(h1, wq, k1, k2):
    n, D = h1.shape
    tr = ROUTE_TR
    hk = PEER_HEADS * PEER_TOPK
    const = lambda shp: pl.BlockSpec(shp, lambda i: (0, 0))
    return pl.pallas_call(
        _route_kernel,
        grid=(n // tr,),
        in_specs=[pl.BlockSpec((tr, D), lambda i: (i, 0)), const(wq.shape), const(k1.shape), const(k2.shape)],
        out_specs=(pl.BlockSpec((hk, tr), lambda i: (0, i)), pl.BlockSpec((hk, tr), lambda i: (0, i))),
        out_shape=(jax.ShapeDtypeStruct((hk, n), jnp.int32), jax.ShapeDtypeStruct((hk, n), F32)),
        compiler_params=_cparams(("parallel",)),
        name="peer_route",
    )(h1, wq.astype(BF16), k1.astype(BF16), k2.astype(BF16))


def _pack_rows(t):
    half = t.shape[1] // 2
    tb = t.astype(BF16)
    lo = lax.bitcast_convert_type(tb[:, :half], jnp.uint16).astype(jnp.uint32)
    hi = lax.bitcast_convert_type(tb[:, half:], jnp.uint16).astype(jnp.uint32)
    return lo | (hi << 16)


def _sc_peer(utab, vtab, idx, x):
    n = x.shape[0]
    w = utab.shape[1]
    hk = PEER_HEADS * PEER_TOPK
    L = SC_LANES
    T = n // SC_WORKERS
    uwin = SC_UWIN
    upt = hk // uwin
    vpt = hk // SC_VWIN
    per_u = vpt // upt
    vlead = SC_VSLOT // 2
    assert n % SC_WORKERS == 0 and T % SC_XB == 0 and uwin % L == 0
    assert upt % SC_USLOT == 0 and vpt % SC_VSLOT == 0 and vpt % upt == 0
    vsteps = T * vpt
    mesh = plsc.VectorSubcoreMesh(core_axis_name="c", subcore_axis_name="s")
    cp = dataclasses.replace(pltpu.CompilerParams(), needs_layout_passes=False)

    @pl.kernel(out_type=(jax.ShapeDtypeStruct((n * hk,), F32), jax.ShapeDtypeStruct((n * hk, w), utab.dtype)),
               mesh=mesh, compiler_params=cp,
               scratch_types=[pltpu.VMEM((T * hk,), jnp.int32),
                              pltpu.VMEM((SC_XB, 2 * w), F32),
                              pltpu.VMEM((SC_USLOT, uwin, w), utab.dtype),
                              pltpu.VMEM((SC_VSLOT, SC_VWIN, w), vtab.dtype),
                              pltpu.VMEM((L, L), F32),
                              pltpu.VMEM((SC_XB * hk,), F32),
                              pltpu.SemaphoreType.DMA((SC_USLOT,)),
                              pltpu.SemaphoreType.DMA((SC_VSLOT,)),
                              pltpu.SemaphoreType.DMA((SC_VSLOT,))])
    def peer(u_hbm, v_hbm, i_hbm, x_hbm, h_hbm, vg_hbm, idx_v, x_v, ubuf, vbuf, acc_s, h_v, usem, vsem, wsem):
        wid = lax.axis_index("c") * (SC_WORKERS // 2) + lax.axis_index("s")
        tok0 = wid * T
        row0 = tok0 * hk
        pltpu.sync_copy(i_hbm.at[pl.ds(row0, T * hk)], idx_v)

        def ufetch(t, j):
            slot = j % SC_USLOT
            return pltpu.make_async_copy(u_hbm.at[idx_v.at[pl.ds(t * hk + j * uwin, uwin)]], ubuf.at[slot], usem.at[slot])

        def vfetch(s, slot):
            return pltpu.make_async_copy(v_hbm.at[idx_v.at[pl.ds(s * SC_VWIN, SC_VWIN)]], vbuf.at[slot], vsem.at[slot])

        def vflush(s, slot):
            return pltpu.make_async_copy(vbuf.at[slot], vg_hbm.at[pl.ds(row0 + s * SC_VWIN, SC_VWIN)], wsem.at[slot])

        for j in range(SC_USLOT):
            ufetch(0, j).start()
        for j in range(vlead):
            vfetch(j, j).start()

        def vstep(s, slot):
            other = (slot + vlead) % SC_VSLOT
            vfetch(s, slot).wait()
            vflush(s, slot).start()

            @pl.when(s >= vlead)
            def _():
                vflush(s - vlead, other).wait()

            @pl.when(s + vlead < vsteps)
            def _():
                vfetch(s + vlead, other).start()

        lane = lax.iota(jnp.int32, L)

        @pl.loop(0, T // SC_XB)
        def _(tb):
            pltpu.sync_copy(x_hbm.at[pl.ds(pl.multiple_of(tok0 + tb * SC_XB, SC_XB), SC_XB)], x_v)

            @pl.loop(0, SC_XB)
            def _(tt):
                t = tb * SC_XB + tt
                for j in range(upt):
                    for i in range(per_u):
                        vstep(t * vpt + j * per_u + i, (j * per_u + i) % SC_VSLOT)
                    ufetch(t, j).wait()

                    def dot_body(jj, accs):
                        off = pl.multiple_of(jj * L, L)
                        xlo = x_v[tt, pl.ds(off, L)]
                        xhi = x_v[tt, pl.ds(w + off, L)]
                        out = []
                        for r in range(uwin):
                            wv = ubuf[j % SC_USLOT, r, pl.ds(off, L)]
                            lo = plsc.bitcast(wv << 16, F32)
                            hi = plsc.bitcast(wv & jnp.uint32(0xFFFF0000), F32)
                            out.append(accs[r] + lo * xlo + hi * xhi)
                        return tuple(out)

                    accs = lax.fori_loop(0, w // L, dot_body, tuple(jnp.zeros((L,), F32) for _ in range(uwin)))

                    if j + SC_USLOT < upt:
                        ufetch(t, j + SC_USLOT).start()
                    else:
                        @pl.when(t + 1 < T)
                        def _():
                            ufetch(t + 1, j + SC_USLOT - upt).start()

                    for g in range(uwin // L):
                        for r in range(L):
                            acc_s[r, :] = accs[g * L + r]
                        tot = jnp.zeros((L,), F32)
                        for c in range(L):
                            tot = tot + plsc.load_gather(acc_s, [lane, jnp.full((L,), c, jnp.int32)])
                        h_v[pl.ds(tt * hk + j * uwin + g * L, L)] = tot

            pltpu.sync_copy(h_v, h_hbm.at[pl.ds(pl.multiple_of(row0 + tb * SC_XB * hk, SC_XB * hk), SC_XB * hk)])

        for s in range(vsteps - vlead, vsteps):
            vflush(s, s % SC_VSLOT).wait()

    return peer(utab, vtab, idx, x)


def _unpack(w):
    lo = lax.bitcast_convert_type(w << 16, F32)
    hi = lax.bitcast_convert_type(w & jnp.uint32(0xFFFF0000), F32)
    return lo, hi


def _mix_kernel(alpha, h_ref, s_ref, gate_ref, vg_ref, g_ref, b_ref, o_ref):
    TT, HK = MIX_TT, PEER_HEADS * PEER_TOPK
    x = h_ref[...]
    wrow = gate_ref[...] * jax.nn.gelu(s_ref[...])
    eye = lax.broadcasted_iota(jnp.int32, (HK, HK), 0) == lax.broadcasted_iota(jnp.int32, (HK, HK), 1)
    rows = []
    for t in range(TT):
        wcol = jnp.sum(jnp.where(eye, jnp.broadcast_to(wrow[t:t + 1, :], (HK, HK)), 0.0),
                       axis=1, keepdims=True)
        vlo, vhi = _unpack(vg_ref[t * HK:(t + 1) * HK, :])
        rows.append(jnp.concatenate([jnp.sum(wcol * vlo, axis=0, keepdims=True),
                                     jnp.sum(wcol * vhi, axis=0, keepdims=True)], axis=1))
    ffn = jnp.concatenate(rows, axis=0)
    o_ref[...] = _layernorm(alpha * x + ffn, g_ref[...], b_ref[...])


def _mix(alpha, h1, scores, gates, vg, g, b):
    n, D = h1.shape
    tt = MIX_TT
    hk = PEER_HEADS * PEER_TOPK
    const = lambda shp: pl.BlockSpec(shp, lambda i: (0, 0))
    return pl.pallas_call(
        functools.partial(_mix_kernel, alpha),
        grid=(n // tt,),
        in_specs=[pl.BlockSpec((tt, D), lambda i: (i, 0)), pl.BlockSpec((tt, hk), lambda i: (i, 0)),
                  pl.BlockSpec((tt, hk), lambda i: (i, 0)), pl.BlockSpec((tt * hk, D // 2), lambda i: (i, 0)),
                  const((1, D)), const((1, D))],
        out_specs=pl.BlockSpec((tt, D), lambda i: (i, 0)),
        out_shape=jax.ShapeDtypeStruct((n, D), F32),
        compiler_params=_cparams(("parallel",)),
        name="peer_mix_ln",
    )(h1, scores, gates, vg, g.reshape(1, D).astype(F32), b.reshape(1, D).astype(F32))


PEER_CHUNK = 4096
SEQ_PIECE = 4096
EDGE_PIECE = 1024


def _pieces(S, first, last):
    cuts = list(range(0, S, min(SEQ_PIECE, S))) + [S]
    if first and EDGE_PIECE < cuts[1]:
        cuts.insert(1, EDGE_PIECE)
    if last and S - EDGE_PIECE > cuts[-2]:
        cuts.insert(-1, S - EDGE_PIECE)
    return [(a, b - a) for a, b in zip(cuts[:-1], cuts[1:])]


def _peer_ffn_ln(alpha, h1, wq, k1, k2, up, vp, g, b):
    eT, gT = _route(h1, wq, k1, k2)
    return _peer_experts_ln(alpha, h1, eT, gT, up, vp, g, b)


def _peer_experts_ln(alpha, h1, eT, gT, up, vp, g, b):
    n, D = h1.shape
    hk = PEER_HEADS * PEER_TOPK
    experts = eT.T
    gates = gT.T
    ch = min(PEER_CHUNK, n)
    outs = []
    for c in range(n // ch):
        hc = h1[c * ch:(c + 1) * ch]
        scores, vg = _sc_peer(up, vp, experts[c * ch:(c + 1) * ch].reshape(ch * hk), hc)
        outs.append(_mix(alpha, hc, scores.reshape(ch, hk), gates[c * ch:(c + 1) * ch], vg, g, b))
    return jnp.concatenate(outs, axis=0)


def kernel(x, w_in, w_out, hg_lb, hg_norm_g, cmpk_pe, cmpk_w1, cmpk_b1, cmpk_w2, cmpk_b2,
           cmpv_pe, cmpv_w1, cmpv_b1, cmpv_w2, cmpv_b2, ln1_g, ln1_b,
           peer_wq, peer_k1, peer_k2, peer_u, peer_v, ln2_g, ln2_b):
    B, S, D = x.shape
    depth = w_in.shape[0]
    alpha = (2.0 * depth) ** 0.25
    h = x
    for l in range(depth):
        up, vp = _pack_rows(peer_u[l]), _pack_rows(peer_v[l])
        cmpk = (cmpk_pe[l], cmpk_w1[l], cmpk_b1[l], cmpk_w2[l], cmpk_b2[l])
        cmpv = (cmpv_pe[l], cmpv_w1[l], cmpv_b1[l], cmpv_w2[l], cmpv_b2[l])
        outs = []
        for bi in range(B):
            hb = h[bi:bi + 1]
            hg, qT, kn, ksE, vsT, vwT, gT = _project(hb, w_in[l])
            o_hg = _hgrn2(hg, hg_lb, hg_norm_g[l], l)
            cmp_n, cmp_T = _compress(kn, cmpk, cmpv)
            for lo, piece in _pieces(S, bi == 0, bi == B - 1):
                o_nsa = _nsa(qT, cmp_n, cmp_T, kn, ksE, vsT, vwT, gT, lo, piece)
                h1 = _outproj(alpha, o_hg[0, lo:lo + piece], o_nsa.reshape(piece, NSA_WIDTH),
                              hb[0, lo:lo + piece], w_out[l], ln1_g[l], ln1_b[l])
                eT, egT = _route(h1, peer_wq[l], peer_k1[l], peer_k2[l])
                qT, eT, egT = lax.optimization_barrier((qT, eT, egT))
                outs.append(_peer_experts_ln(alpha, h1, eT, egT, up, vp, ln2_g[l], ln2_b[l]))
        h = jnp.concatenate(outs, axis=0).reshape(B, S, D)
    return h
```

```python
import dataclasses
import functools
import math

import jax
import jax.numpy as jnp
import numpy as np
from jax import lax
from jax.experimental import pallas as pl
from jax.experimental.pallas import tpu as pltpu
from jax.experimental.pallas import tpu_sc as plsc

F32 = jnp.float32
BF16 = jnp.bfloat16

HG_HEADS = 4
HG_DK = 128
HG_DV = 128
HG_CHUNK = 64
HG_SUB = 16
HG_WIDTH = HG_HEADS * HG_DV
NSA_HEADS = 8
NSA_KV_HEADS = 2
NSA_GROUP = NSA_HEADS // NSA_KV_HEADS
NSA_HD = 64
NSA_WIDTH = NSA_HEADS * NSA_HD
CMP_LEN = 32
CMP_STRIDE = 16
CMP_HIDDEN = 256
SLC_LEN = 64
SLC_TOPK = 16
WIN = 512
FORCE_SCORE = 1.0e4
PEER_HEADS = 8
PEER_DQ = 256
PEER_NKEYS = 128
PEER_TOPK = 16
LN_EPS = 1e-5
NEG = -1e30
LOG2E = math.log2(math.e)

LANES = 128
PROJ_TM = 512
HG_TC = 256
ATT_TQ = 128
ATT_KT = 512
ATT_KS = 1024
OUT_TM = 512
ROUTE_TR = 256
MIX_TT = 16
SC_WORKERS = 32
SC_LANES = 16
SC_UWIN = 16
SC_USLOT = 4
SC_VWIN = 16
SC_VSLOT = 8
SC_XB = 8
VMEM_LIMIT = 56 * 1024 * 1024


def _cparams(sem):
    return pltpu.CompilerParams(dimension_semantics=sem, vmem_limit_bytes=VMEM_LIMIT)


def _nt(a, b):
    return lax.dot_general(a, b, (((1,), (1,)), ((), ())), preferred_element_type=F32)


def _tn(a, b):
    return lax.dot_general(a, b, (((0,), (0,)), ((), ())), preferred_element_type=F32)


def _nn(a, b):
    return jnp.dot(a, b, preferred_element_type=F32)


def _proj_kernel(x_ref, whg_ref, wqT_ref, wkn_ref, wvT_ref, wgT_ref,
                 hg_ref, qT_ref, kn_ref, ksE_ref, vsT_ref, vwT_ref, gT_ref):
    xb = x_ref[0].astype(BF16)
    hg_ref[0] = _nn(xb, whg_ref[...])
    qT_ref[0] = (_nt(wqT_ref[...], xb) * (NSA_HD ** -0.5 * LOG2E)).astype(BF16)
    kn = _nn(xb, wkn_ref[...])
    for j in range(3):
        for g in range(NSA_KV_HEADS):
            lo = j * 2 * NSA_HD + g * NSA_HD
            kn_ref[0, j, g] = kn[:, lo:lo + NSA_HD].astype(BF16)
    row = pl.program_id(1) * kn.shape[0] + lax.broadcasted_iota(jnp.int32, (kn.shape[0], NSA_HD), 0)
    blk = (row >> int(math.log2(SLC_LEN))) & (ATT_KS // SLC_LEN - 1)
    onehot = (lax.broadcasted_iota(jnp.int32, (kn.shape[0], NSA_HD), 1) == blk).astype(F32)
    for g in range(NSA_KV_HEADS):
        lo = 3 * 2 * NSA_HD + g * NSA_HD
        ksE_ref[0, g] = jnp.concatenate([kn[:, lo:lo + NSA_HD], onehot], axis=1).astype(BF16)
    vT = _nt(wvT_ref[...], xb).astype(BF16)
    tm = vT.shape[1]
    for g in range(NSA_KV_HEADS):
        for c in range(tm // ATT_KT):
            vsT_ref[0, g, c] = vT[g * NSA_HD:(g + 1) * NSA_HD, c * ATT_KT:(c + 1) * ATT_KT]
        for c in range(tm // LANES):
            vwT_ref[0, g, c] = vT[(2 + g) * NSA_HD:(3 + g) * NSA_HD, c * LANES:(c + 1) * LANES]
    gT = _nt(wgT_ref[...], xb)
    gT_ref[0, 0] = gT[0:16]
    gT_ref[0, 1] = gT[16:32]


def _project(x, w_in_l):
    B, S, D = x.shape
    tm = PROJ_TM
    o = np.cumsum((0, 512, 512, 512, 512, 512, 128, 128, 128, 128, 128, 128, 24))
    wb = w_in_l.astype(BF16)
    whg = wb[:, o[0]:o[4]]
    wqT = wb[:, o[4]:o[5]].T
    wkn = jnp.concatenate([wb[:, o[5]:o[6]], wb[:, o[6]:o[7]], wb[:, o[9]:o[10]], wb[:, o[7]:o[8]]], axis=1)
    wvT = jnp.concatenate([wb[:, o[8]:o[9]], wb[:, o[10]:o[11]]], axis=1).T
    wg = wb[:, o[11]:o[12]].reshape(D, NSA_KV_HEADS, NSA_GROUP * 3)
    wgT = jnp.pad(wg, ((0, 0), (0, 0), (0, 16 - NSA_GROUP * 3))).reshape(D, 32).T
    const = lambda shp: pl.BlockSpec(shp, lambda b, i: (0,) * len(shp))
    out_shape = (
        jax.ShapeDtypeStruct((B, S, 4 * 512), F32),
        jax.ShapeDtypeStruct((B, NSA_WIDTH, S), BF16),
        jax.ShapeDtypeStruct((B, 3, NSA_KV_HEADS, S, NSA_HD), BF16),
        jax.ShapeDtypeStruct((B, NSA_KV_HEADS, S, 2 * NSA_HD), BF16),
        jax.ShapeDtypeStruct((B, NSA_KV_HEADS, S // ATT_KT, NSA_HD, ATT_KT), BF16),
        jax.ShapeDtypeStruct((B, NSA_KV_HEADS, S // LANES, NSA_HD, LANES), BF16),
        jax.ShapeDtypeStruct((B, NSA_KV_HEADS, 16, S), F32),
    )
    out_specs = (
        pl.BlockSpec((1, tm, 2048), lambda b, i: (b, i, 0)),
        pl.BlockSpec((1, NSA_WIDTH, tm), lambda b, i: (b, 0, i)),
        pl.BlockSpec((1, 3, NSA_KV_HEADS, tm, NSA_HD), lambda b, i: (b, 0, 0, i, 0)),
        pl.BlockSpec((1, NSA_KV_HEADS, tm, 2 * NSA_HD), lambda b, i: (b, 0, i, 0)),
        pl.BlockSpec((1, NSA_KV_HEADS, tm // ATT_KT, NSA_HD, ATT_KT), lambda b, i: (b, 0, i, 0, 0)),
        pl.BlockSpec((1, NSA_KV_HEADS, tm // LANES, NSA_HD, LANES), lambda b, i: (b, 0, i, 0, 0)),
        pl.BlockSpec((1, NSA_KV_HEADS, 16, tm), lambda b, i: (b, 0, 0, i)),
    )
    return pl.pallas_call(
        _proj_kernel,
        grid=(B, S // tm),
        in_specs=[pl.BlockSpec((1, tm, D), lambda b, i: (b, i, 0)),
                  const(whg.shape), const(wqT.shape), const(wkn.shape), const(wvT.shape), const(wgT.shape)],
        out_specs=out_specs,
        out_shape=out_shape,
        compiler_params=_cparams(("parallel", "arbitrary")),
        name="in_proj",
    )(x, whg, wqT, wkn, wvT, wgT)


def _split3(a):
    hi = a.astype(BF16)
    r = a - hi.astype(F32)
    mid = r.astype(BF16)
    lo = (r - mid.astype(F32)).astype(BF16)
    return hi, mid, lo


def _hgrn_kernel(layer, hg_ref, lb_ref, ng_ref, o_ref, st_ref):
    C, SB, H, DK = HG_CHUNK, HG_SUB, HG_HEADS, HG_DK
    nsb = C // SB

    @pl.when(pl.program_id(1) == 0)
    def _():
        st_ref[...] = jnp.zeros_like(st_ref)

    lbp = lb_ref[...]
    e = jnp.exp(lbp - jnp.max(lbp, axis=0, keepdims=True))
    lb = jnp.sum(e[:layer + 1], axis=0, keepdims=True) / jnp.sum(e, axis=0, keepdims=True)

    ri = lax.broadcasted_iota(jnp.int32, (C, C), 0)
    ci = lax.broadcasted_iota(jnp.int32, (C, C), 1)
    tril = (ci <= ri).astype(BF16)
    t_sub = lax.broadcasted_iota(jnp.int32, (SB, H * DK), 0)
    lane16 = lax.broadcasted_iota(jnp.int32, (SB, SB), 1)
    ng = ng_ref[...]

    def chunk(c, carry):
        r0 = pl.multiple_of(c * C, C)
        blk = hg_ref[0, pl.ds(r0, C), :]
        q = blk[:, 0:512]
        f = lb + (1.0 - lb) * jax.nn.sigmoid(blk[:, 512:1024])
        lf = jnp.log(f)
        k = 1.0 - f
        v = blk[:, 1024:1536]
        gate = blk[:, 1536:2048]
        hi, mid, lo = _split3(lf)
        b = _nn(tril, hi) + _nn(tril, mid) + _nn(tril, lo)
        b_last = b[C - 1:C, :]
        qe = (q * jnp.exp(b)).astype(BF16)
        ks = (k * jnp.exp(b_last - b)).astype(BF16)
        dec = jnp.exp(b_last)
        vb = v.astype(BF16)
        rblk = jnp.concatenate([jnp.broadcast_to(b[i * SB:i * SB + 1, :], (SB, H * DK)) for i in range(nsb)], axis=0)
        qn = (q * jnp.exp(b - rblk)).astype(BF16)
        outs = []
        for h in range(H):
            sl = slice(h * DK, (h + 1) * DK)
            stT = st_ref[h]
            o_h = _nt(qe[:, sl], stT.astype(BF16))
            rows = []
            for i in range(nsb):
                rs = slice(i * SB, (i + 1) * SB)
                acc = jnp.zeros((SB, HG_DV), F32)
                if i > 0:
                    ke = (k[0:i * SB, sl] * jnp.exp(b[i * SB:i * SB + 1, sl] - b[0:i * SB, sl])).astype(BF16)
                    a_off = _nt(qn[rs, sl], ke)
                    acc = acc + _nn(a_off.astype(BF16), vb[0:i * SB, sl])
                qi, ki, bi = q[rs, sl], k[rs, sl], b[rs, sl]
                a_d = jnp.zeros((SB, SB), F32)
                for s in range(SB):
                    w = qi * ki[s:s + 1, :] * jnp.exp(jnp.minimum(bi - bi[s:s + 1, :], 0.0))
                    col = jnp.sum(w, axis=1, keepdims=True)
                    a_d = a_d + jnp.where((lane16 == s) & (t_sub[:, 0:SB] >= s), col, 0.0)
                acc = acc + _nn(a_d.astype(BF16), vb[rs, sl])
                rows.append(acc)
            o_h = o_h + jnp.concatenate(rows, axis=0)
            st_ref[h] = dec[:, sl] * stT + _tn(vb[:, sl], ks[:, sl])
            o_h = o_h * lax.rsqrt(jnp.mean(o_h * o_h, axis=1, keepdims=True) + LN_EPS) * ng
            outs.append(o_h)
        o = jnp.concatenate(outs, axis=1) * jax.nn.silu(gate)
        o_ref[0, pl.ds(r0, C), :] = o.astype(BF16)
        return carry

    lax.fori_loop(0, hg_ref.shape[1] // C, chunk, 0)


def _hgrn2(hg, hg_lb, norm_g_l, layer):
    B, S, _ = hg.shape
    tc = HG_TC
    return pl.pallas_call(
        functools.partial(_hgrn_kernel, layer),
        grid=(B, S // tc),
        in_specs=[pl.BlockSpec((1, tc, 2048), lambda b, i: (b, i, 0)),
                  pl.BlockSpec(hg_lb.shape, lambda b, i: (0, 0)),
                  pl.BlockSpec((1, HG_DV), lambda b, i: (0, 0))],
        out_specs=pl.BlockSpec((1, tc, HG_WIDTH), lambda b, i: (b, i, 0)),
        out_shape=jax.ShapeDtypeStruct((B, S, HG_WIDTH), BF16),
        scratch_shapes=[pltpu.VMEM((HG_HEADS, HG_DV, HG_DK), F32)],
        compiler_params=_cparams(("parallel", "arbitrary")),
        name="hgrn2",
    )(hg, hg_lb.astype(F32), norm_g_l.reshape(1, HG_DV).astype(F32))


def _compress_kernel(c_ref, pe_ref, w1_ref, b1_ref, w2_ref, w2T_ref, b2_ref, b2T_ref, o_ref, oT_ref):
    half = CMP_STRIDE * NSA_HD
    c = c_ref[0, 0, 0].astype(F32)
    pe = pe_ref[0]
    ca = (c + pe[:, 0:half]).astype(BF16)
    cb = (c + pe[:, half:2 * half]).astype(BF16)
    pa = _nn(ca, w1_ref[0, 0:half, :])
    pb = _nn(cb, w1_ref[0, half:2 * half, :])
    nbp = pa.shape[0]
    h = pa + pltpu.roll(pb, nbp - 1, 0) + b1_ref[0]
    h = jax.nn.gelu(h).astype(BF16)
    o_ref[0, 0, 0] = (_nn(h, w2_ref[0]) + b2_ref[0]).astype(BF16)
    oT_ref[0, 0, 0] = (_nt(w2T_ref[0], h) + b2T_ref[0]).astype(BF16)


def _compress(kn, cmpk, cmpv):
    B, _, G, S, HD = kn.shape
    nbp = S // CMP_STRIDE
    c = kn[:, 0:2].reshape(B, 2, G, nbp, CMP_STRIDE * HD)
    stack = lambda a, b, f: jnp.stack([f(a), f(b)], axis=0)
    pe = stack(cmpk[0], cmpv[0], lambda t: t.reshape(1, CMP_LEN * HD).astype(F32))
    w1 = stack(cmpk[1], cmpv[1], lambda t: t.astype(BF16))
    b1 = stack(cmpk[2], cmpv[2], lambda t: t.reshape(1, CMP_HIDDEN).astype(F32))
    w2 = stack(cmpk[3], cmpv[3], lambda t: t.astype(BF16))
    w2T = stack(cmpk[3], cmpv[3], lambda t: t.astype(BF16).T)
    b2 = stack(cmpk[4], cmpv[4], lambda t: t.reshape(1, HD).astype(F32))
    b2T = stack(cmpk[4], cmpv[4], lambda t: t.reshape(HD, 1).astype(F32))
    wspec = lambda a: pl.BlockSpec((1,) + a.shape[1:], lambda b, t, g: (t,) + (0,) * (a.ndim - 1))
    return pl.pallas_call(
        _compress_kernel,
        grid=(B, 2, G),
        in_specs=[pl.BlockSpec((1, 1, 1, nbp, CMP_STRIDE * HD), lambda b, t, g: (b, t, g, 0, 0)),
                  wspec(pe), wspec(w1), wspec(b1), wspec(w2), wspec(w2T), wspec(b2), wspec(b2T)],
        out_specs=(pl.BlockSpec((1, 1, 1, nbp, HD), lambda b, t, g: (b, t, g, 0, 0)),
                   pl.BlockSpec((1, 1, 1, HD, nbp), lambda b, t, g: (b, t, g, 0, 0))),
        out_shape=(jax.ShapeDtypeStruct((B, 2, G, nbp, HD), BF16),
                   jax.ShapeDtypeStruct((B, 2, G, HD, nbp), BF16)),
        compiler_params=_cparams(("parallel", "parallel", "parallel")),
        name="kv_compress",
    )(c, pe, w1, b1, w2, w2T, b2, b2T)


def _topk_mask_rows(score, k):
    R = score.shape[0]
    r_io = lax.broadcasted_iota(jnp.int32, score.shape, 0)
    s = score
    for _ in range(k):
        m = jnp.max(s, axis=0, keepdims=True)
        idx = jnp.min(jnp.where(s == m, r_io, R), axis=0, keepdims=True)
        s = jnp.where(r_io == idx, -jnp.inf, s)
    return (s == -jnp.inf).astype(F32)


def _with_ones_row(vT):
    pad = (lax.broadcasted_iota(jnp.int32, (16, vT.shape[1]), 0) == 0).astype(BF16)
    return jnp.concatenate([vT, pad], axis=0)


def _nsa_kernel(q_off, qT_ref, kc_ref, vcT_ref, ks_ref, vsT_ref, kw_ref, vwT_ref, gT_ref, ovT_ref,
                o_ref, sel_ref):
    TQ, G, HD, KT, NG = ATT_TQ, NSA_GROUP, NSA_HD, ATT_KT, NSA_KV_HEADS
    qi = pl.program_id(1) + q_off
    t0 = qi * TQ
    pos1 = t0 + lax.broadcasted_iota(jnp.int32, (1, TQ), 1)
    pos4 = jnp.concatenate([pos1] * G, axis=1)
    nbp = kc_ref.shape[3]
    ns = ovT_ref.shape[0]
    n_io = lax.broadcasted_iota(jnp.int32, (nbp, 1), 0)
    mask_c = (n_io * CMP_STRIDE + (CMP_LEN - 1)) <= pos4
    j_io = lax.broadcasted_iota(jnp.int32, (ns, 1), 0)
    cur = pos1 >> int(math.log2(SLC_LEN))
    forced = (j_io == 0) | (j_io == cur) | (j_io == cur - 1)
    causal_blk = j_io * SLC_LEN <= pos1

    q4s, o_cs = [], []
    for g in range(NG):
        qT = qT_ref[0, g * G * HD:(g + 1) * G * HD, :]
        q4 = jnp.concatenate([qT[r * HD:(r + 1) * HD, :] for r in range(G)], axis=1)
        s_c = jnp.where(mask_c, _nn(kc_ref[0, 0, g], q4), NEG)
        p_c = jnp.where(mask_c, jnp.exp2(s_c - jnp.max(s_c, axis=0, keepdims=True)), 0.0)
        p_c = p_c / jnp.maximum(jnp.sum(p_c, axis=0, keepdims=True), 1e-30)
        o_cs.append(_nn(vcT_ref[0, 0, g], p_c.astype(BF16)))
        p_sum = p_c[:, 0:TQ]
        for r in range(1, G):
            p_sum = p_sum + p_c[:, r * TQ:(r + 1) * TQ]
        imp = _nn(ovT_ref[...], p_sum.astype(BF16))
        score = jnp.where(forced, FORCE_SCORE, jnp.where(causal_blk, imp, -1.0))
        sel_ref[g] = (_topk_mask_rows(score, min(SLC_TOPK, ns)) - 1.0) * (-NEG)
        q4s.append(q4)

    KS = ATT_KS
    bps = KS // SLC_LEN
    k_io = lax.broadcasted_iota(jnp.int32, (KS, 1), 0)
    zrows = jnp.zeros((HD - bps, G * TQ), BF16)

    def sel_step(ks_i, carry, diagonal):
        k0 = pl.multiple_of(ks_i * KS, KS)
        out = []
        for g in range(NG):
            m, acc = carry[g]
            bias = sel_ref[g, pl.ds(pl.multiple_of(ks_i * bps, bps), bps), :].astype(BF16)
            rhs = jnp.concatenate([q4s[g], jnp.concatenate([bias] * G, axis=1), zrows], axis=0)
            s = _nn(ks_ref[0, g, pl.ds(k0, KS), :], rhs)
            if diagonal:
                s = jnp.where((k0 + k_io) <= pos4, s, NEG)
            m_new = jnp.maximum(m, jnp.max(s, axis=0, keepdims=True))
            p = jnp.exp2(s - m_new).astype(BF16)
            acc = jnp.exp2(m - m_new) * acc
            for c in range(KS // KT):
                acc = acc + _nn(_with_ones_row(vsT_ref[0, g, ks_i * (KS // KT) + c]), p[c * KT:(c + 1) * KT])
            out.append((m_new, acc))
        return tuple(out)

    n_ks = (t0 + TQ - 1) // KS + 1
    init = tuple((jnp.full((1, G * TQ), NEG, F32), jnp.zeros((HD + 16, G * TQ), F32)) for _ in range(NG))
    carry = lax.fori_loop(0, n_ks - 1, lambda i, c: sel_step(i, c, False), init)
    carry = sel_step(n_ks - 1, carry, True)

    nwt = WIN // TQ + 1
    r_io = lax.broadcasted_iota(jnp.int32, (TQ, 1), 0)
    idxs = [qi - (nwt - 1) + i for i in range(nwt)]
    kpos = jnp.concatenate([idx * TQ + r_io for idx in idxs], axis=0)
    delta = pos4 - kpos
    mask_w = (delta >= 0) & (delta < WIN) & (kpos >= 0)
    outs = []
    for g in range(NG):
        acc_s = carry[g][1]
        o_s = acc_s[0:HD] / jnp.maximum(acc_s[HD:HD + 1], 1e-30)
        kw_t = [kw_ref[0, 0, g, pl.ds(pl.multiple_of(jnp.maximum(idx, 0) * TQ, TQ), TQ), :] for idx in idxs]
        vw_t = [vwT_ref[0, g, jnp.maximum(idx, 0)] for idx in idxs]
        s_w = jnp.where(mask_w, _nn(jnp.concatenate(kw_t, axis=0), q4s[g]), NEG)
        p_w = jnp.exp2(s_w - jnp.max(s_w, axis=0, keepdims=True)).astype(BF16)
        acc_w = _nn(_with_ones_row(jnp.concatenate(vw_t, axis=1)), p_w)
        o_w = acc_w[0:HD] / jnp.maximum(acc_w[HD:HD + 1], 1e-30)
        gs = jax.nn.sigmoid(gT_ref[0, g])
        gate = lambda br: jnp.concatenate([gs[r * 3 + br:r * 3 + br + 1, :] for r in range(G)], axis=1)
        o = gate(0) * o_cs[g] + gate(1) * o_s + gate(2) * o_w
        outs.extend(o[:, r * TQ:(r + 1) * TQ].T for r in range(G))
    o_ref[0] = jnp.concatenate(outs, axis=1).astype(BF16)


def _overlap_T(S):
    nbp = S // CMP_STRIDE
    n_slc = S // SLC_LEN
    cs = np.arange(nbp) * CMP_STRIDE
    ss = np.arange(n_slc) * SLC_LEN
    ov = np.clip(np.minimum(cs[:, None] + CMP_LEN, ss[None, :] + SLC_LEN)
                 - np.maximum(cs[:, None], ss[None, :]), 0, None) / CMP_LEN
    ov[nbp - 1:, :] = 0.0
    return jnp.asarray(ov.T, BF16)


def _nsa(qT, cmp_n, cmp_T, kn, ksE, vsT, vwT, gT, q_lo, q_len):
    B, _, S = qT.shape
    G2, TQ, HD = NSA_KV_HEADS, ATT_TQ, NSA_HD
    nbp = S // CMP_STRIDE
    ns = S // SLC_LEN
    ovT = _overlap_T(S)
    q_off = q_lo // TQ
    return pl.pallas_call(
        functools.partial(_nsa_kernel, q_off),
        grid=(B, q_len // TQ),
        in_specs=[
            pl.BlockSpec((1, NSA_WIDTH, TQ), lambda b, i: (b, 0, i + q_off)),
            pl.BlockSpec((1, 1, G2, nbp, HD), lambda b, i: (b, 0, 0, 0, 0)),
            pl.BlockSpec((1, 1, G2, HD, nbp), lambda b, i: (b, 1, 0, 0, 0)),
            pl.BlockSpec((1, G2, S, 2 * HD), lambda b, i: (b, 0, 0, 0)),
            pl.BlockSpec((1, G2, S // ATT_KT, HD, ATT_KT), lambda b, i: (b, 0, 0, 0, 0)),
            pl.BlockSpec((1, 1, G2, S, HD), lambda b, i: (b, 2, 0, 0, 0)),
            pl.BlockSpec((1, G2, S // LANES, HD, LANES), lambda b, i: (b, 0, 0, 0, 0)),
            pl.BlockSpec((1, G2, 16, TQ), lambda b, i: (b, 0, 0, i + q_off)),
            pl.BlockSpec((ns, nbp), lambda b, i: (0, 0)),
        ],
        out_specs=pl.BlockSpec((1, TQ, NSA_WIDTH), lambda b, i: (b, i, 0)),
        out_shape=jax.ShapeDtypeStruct((B, q_len, NSA_WIDTH), BF16),
        scratch_shapes=[pltpu.VMEM((G2, ns, TQ), F32)],
        compiler_params=_cparams(("parallel", "arbitrary")),
        name="nsa_attn",
    )(qT, cmp_n, cmp_T, ksE, vsT, kn, vwT, gT, ovT)


def _layernorm(t, g, b):
    mu = jnp.mean(t, axis=-1, keepdims=True)
    d = t - mu
    var = jnp.mean(d * d, axis=-1, keepdims=True)
    return d * lax.rsqrt(var + LN_EPS) * g + b


def _outproj_kernel(alpha, ohg_ref, onsa_ref, x_ref, w_ref, g_ref, b_ref, h_ref):
    mix = _nn(ohg_ref[...], w_ref[0:HG_WIDTH, :]) + _nn(onsa_ref[...], w_ref[HG_WIDTH:HG_WIDTH + NSA_WIDTH, :])
    h_ref[...] = _layernorm(alpha * x_ref[...] + mix, g_ref[...], b_ref[...])


def _outproj(alpha, o_hg, o_nsa, x2, w_out_l, g, b):
    n, D = x2.shape
    tm = OUT_TM
    row = lambda w: pl.BlockSpec((tm, w), lambda i: (i, 0))
    const = lambda shp: pl.BlockSpec(shp, lambda i: (0, 0))
    return pl.pallas_call(
        functools.partial(_outproj_kernel, alpha),
        grid=(n // tm,),
        in_specs=[row(HG_WIDTH), row(NSA_WIDTH), row(D), const(w_out_l.shape), const((1, D)), const((1, D))],
        out_specs=row(D),
        out_shape=jax.ShapeDtypeStruct((n, D), F32),
        compiler_params=_cparams(("parallel",)),
        name="out_proj_ln",
    )(o_hg, o_nsa, x2, w_out_l.astype(BF16), g.reshape(1, D).astype(F32), b.reshape(1, D).astype(F32))


def _topk_rows(s, k):
    R = s.shape[0]
    r_io = lax.broadcasted_iota(jnp.int32, s.shape, 0)
    vals, idxs = [], []
    for _ in range(k):
        m = jnp.max(s, axis=0, keepdims=True)
        idx = jnp.min(jnp.where(s == m, r_io, R), axis=0, keepdims=True)
        vals.append(m)
        idxs.append(idx)
        s = jnp.where(r_io == idx, -jnp.inf, s)
    return jnp.concatenate(vals, axis=0), jnp.concatenate(idxs, axis=0)


def _route_kernel(h_ref, wq_ref, k1_ref, k2_ref, e_ref, g_ref):
    K, half = PEER_TOPK, PEER_DQ // 2
    q = _nn(h_ref[...].astype(BF16), wq_ref[...]).astype(BF16)
    for h in range(PEER_HEADS):
        s1 = _nt(k1_ref[...], q[:, h * PEER_DQ:h * PEER_DQ + half])
        s2 = _nt(k2_ref[...], q[:, h * PEER_DQ + half:(h + 1) * PEER_DQ])
        v1, i1 = _topk_rows(s1, K)
        v2, i2 = _topk_rows(s2, K)
        sub = 8
        b_io = lax.broadcasted_iota(jnp.int32, (sub, 1), 0)
        cands = [v1[0:1, :] + v2]
        ids = [i1[0:1, :] * PEER_NKEYS + i2]
        for a in range(1, sub):
            cands.append(jnp.where(b_io < K // (a + 1), v1[a:a + 1, :] + v2[0:sub, :], -jnp.inf))
            ids.append(i1[a:a + 1, :] * PEER_NKEYS + i2[0:sub, :])
        cands.append(v1[sub:K, :] + v2[0:1, :])
        ids.append(i1[sub:K, :] * PEER_NKEYS + i2[0:1, :])
        cand = jnp.concatenate(cands, axis=0)
        cand_id = jnp.concatenate(ids, axis=0)
        r_io = lax.broadcasted_iota(jnp.int32, cand.shape, 0)
        top_s, top_e = [], []
        for _ in range(K):
            m = jnp.max(cand, axis=0, keepdims=True)
            hit = r_io == jnp.min(jnp.where(cand == m, r_io, cand.shape[0]), axis=0, keepdims=True)
            top_s.append(m)
            top_e.append(jnp.sum(jnp.where(hit, cand_id, 0), axis=0, keepdims=True))
            cand = jnp.where(hit, -jnp.inf, cand)
        top_s = jnp.concatenate(top_s, axis=0)
        e_ref[h * K:(h + 1) * K, :] = jnp.concatenate(top_e, axis=0)
        ex = jnp.exp(top_s - top_s[0:1, :])
        g_ref[h * K:(h + 1) * K, :] = ex / jnp.sum(ex, axis=0, keepdims=True)


def _route(h1, wq, k1, k2):
    n, D = h1.shape
    tr = ROUTE_TR
    hk = PEER_HEADS * PEER_TOPK
    const = lambda shp: pl.BlockSpec(shp, lambda i: (0, 0))
    return pl.pallas_call(
        _route_kernel,
        grid=(n // tr,),
        in_specs=[pl.BlockSpec((tr, D), lambda i: (i, 0)), const(wq.shape), const(k1.shape), const(k2.shape)],
        out_specs=(pl.BlockSpec((hk, tr), lambda i: (0, i)), pl.BlockSpec((hk, tr), lambda i: (0, i))),
        out_shape=(jax.ShapeDtypeStruct((hk, n), jnp.int32), jax.ShapeDtypeStruct((hk, n), F32)),
        compiler_params=_cparams(("parallel",)),
        name="peer_route",
    )(h1, wq.astype(BF16), k1.astype(BF16), k2.astype(BF16))


def _pack_rows(t):
    half = t.shape[1] // 2
    tb = t.astype(BF16)
    lo = lax.bitcast_convert_type(tb[:, :half], jnp.uint16).astype(jnp.uint32)
    hi = lax.bitcast_convert_type(tb[:, half:], jnp.uint16).astype(jnp.uint32)
    return lo | (hi << 16)


def _sc_peer(utab, vtab, idx, x):
    n = x.shape[0]
    w = utab.shape[1]
    hk = PEER_HEADS * PEER_TOPK
    L = SC_LANES
    T = n // SC_WORKERS
    uwin = SC_UWIN
    upt = hk // uwin
    vpt = hk // SC_VWIN
    per_u = vpt // upt
    vlead = SC_VSLOT // 2
    assert n % SC_WORKERS == 0 and T % SC_XB == 0 and uwin % L == 0
    assert upt % SC_USLOT == 0 and vpt % SC_VSLOT == 0 and vpt % upt == 0
    vsteps = T * vpt
    mesh = plsc.VectorSubcoreMesh(core_axis_name="c", subcore_axis_name="s")
    cp = dataclasses.replace(pltpu.CompilerParams(), needs_layout_passes=False)

    @pl.kernel(out_type=(jax.ShapeDtypeStruct((n * hk,), F32), jax.ShapeDtypeStruct((n * hk, w), utab.dtype)),
               mesh=mesh, compiler_params=cp,
               scratch_types=[pltpu.VMEM((T * hk,), jnp.int32),
                              pltpu.VMEM((SC_XB, 2 * w), F32),
                              pltpu.VMEM((SC_USLOT, uwin, w), utab.dtype),
                              pltpu.VMEM((SC_VSLOT, SC_VWIN, w), vtab.dtype),
                              pltpu.VMEM((L, L), F32),
                              pltpu.VMEM((SC_XB * hk,), F32),
                              pltpu.SemaphoreType.DMA((SC_USLOT,)),
                              pltpu.SemaphoreType.DMA((SC_VSLOT,)),
                              pltpu.SemaphoreType.DMA((SC_VSLOT,))])
    def peer(u_hbm, v_hbm, i_hbm, x_hbm, h_hbm, vg_hbm, idx_v, x_v, ubuf, vbuf, acc_s, h_v, usem, vsem, wsem):
        wid = lax.axis_index("c") * (SC_WORKERS // 2) + lax.axis_index("s")
        tok0 = wid * T
        row0 = tok0 * hk
        pltpu.sync_copy(i_hbm.at[pl.ds(row0, T * hk)], idx_v)

        def ufetch(t, j):
            slot = j % SC_USLOT
            return pltpu.make_async_copy(u_hbm.at[idx_v.at[pl.ds(t * hk + j * uwin, uwin)]], ubuf.at[slot], usem.at[slot])

        def vfetch(s, slot):
            return pltpu.make_async_copy(v_hbm.at[idx_v.at[pl.ds(s * SC_VWIN, SC_VWIN)]], vbuf.at[slot], vsem.at[slot])

        def vflush(s, slot):
            return pltpu.make_async_copy(vbuf.at[slot], vg_hbm.at[pl.ds(row0 + s * SC_VWIN, SC_VWIN)], wsem.at[slot])

        for j in range(SC_USLOT):
            ufetch(0, j).start()
        for j in range(vlead):
            vfetch(j, j).start()

        def vstep(s, slot):
            other = (slot + vlead) % SC_VSLOT
            vfetch(s, slot).wait()
            vflush(s, slot).start()

            @pl.when(s >= vlead)
            def _():
                vflush(s - vlead, other).wait()

            @pl.when(s + vlead < vsteps)
            def _():
                vfetch(s + vlead, other).start()

        lane = lax.iota(jnp.int32, L)

        @pl.loop(0, T // SC_XB)
        def _(tb):
            pltpu.sync_copy(x_hbm.at[pl.ds(pl.multiple_of(tok0 + tb * SC_XB, SC_XB), SC_XB)], x_v)

            @pl.loop(0, SC_XB)
            def _(tt):
                t = tb * SC_XB + tt
                for j in range(upt):
                    for i in range(per_u):
                        vstep(t * vpt + j * per_u + i, (j * per_u + i) % SC_VSLOT)
                    ufetch(t, j).wait()

                    def dot_body(jj, accs):
                        off = pl.multiple_of(jj * L, L)
                        xlo = x_v[tt, pl.ds(off, L)]
                        xhi = x_v[tt, pl.ds(w + off, L)]
                        out = []
                        for r in range(uwin):
                            wv = ubuf[j % SC_USLOT, r, pl.ds(off, L)]
                            lo = plsc.bitcast(wv << 16, F32)
                            hi = plsc.bitcast(wv & jnp.uint32(0xFFFF0000), F32)
                            out.append(accs[r] + lo * xlo + hi * xhi)
                        return tuple(out)

                    accs = lax.fori_loop(0, w // L, dot_body, tuple(jnp.zeros((L,), F32) for _ in range(uwin)))

                    if j + SC_USLOT < upt:
                        ufetch(t, j + SC_USLOT).start()
                    else:
                        @pl.when(t + 1 < T)
                        def _():
                            ufetch(t + 1, j + SC_USLOT - upt).start()

                    for g in range(uwin // L):
                        for r in range(L):
                            acc_s[r, :] = accs[g * L + r]
                        tot = jnp.zeros((L,), F32)
                        for c in range(L):
                            tot = tot + plsc.load_gather(acc_s, [lane, jnp.full((L,), c, jnp.int32)])
                        h_v[pl.ds(tt * hk + j * uwin + g * L, L)] = tot

            pltpu.sync_copy(h_v, h_hbm.at[pl.ds(pl.multiple_of(row0 + tb * SC_XB * hk, SC_XB * hk), SC_XB * hk)])

        for s in range(vsteps - vlead, vsteps):
            vflush(s, s % SC_VSLOT).wait()

    return peer(utab, vtab, idx, x)


def _unpack(w):
    lo = lax.bitcast_convert_type(w << 16, F32)
    hi = lax.bitcast_convert_type(w & jnp.uint32(0xFFFF0000), F32)
    return lo, hi


def _mix_kernel(alpha, h_ref, s_ref, gate_ref, vg_ref, g_ref, b_ref, o_ref):
    TT, HK = MIX_TT, PEER_HEADS * PEER_TOPK
    x = h_ref[...]
    wrow = gate_ref[...] * jax.nn.gelu(s_ref[...])
    eye = lax.broadcasted_iota(jnp.int32, (HK, HK), 0) == lax.broadcasted_iota(jnp.int32, (HK, HK), 1)
    rows = []
    for t in range(TT):
        wcol = jnp.sum(jnp.where(eye, jnp.broadcast_to(wrow[t:t + 1, :], (HK, HK)), 0.0),
                       axis=1, keepdims=True)
        vlo, vhi = _unpack(vg_ref[t * HK:(t + 1) * HK, :])
        rows.append(jnp.concatenate([jnp.sum(wcol * vlo, axis=0, keepdims=True),
                                     jnp.sum(wcol * vhi, axis=0, keepdims=True)], axis=1))
    ffn = jnp.concatenate(rows, axis=0)
    o_ref[...] = _layernorm(alpha * x + ffn, g_ref[...], b_ref[...])


def _mix(alpha, h1, scores, gates, vg, g, b):
    n, D = h1.shape
    tt = MIX_TT
    hk = PEER_HEADS * PEER_TOPK
    const = lambda shp: pl.BlockSpec(shp, lambda i: (0, 0))
    return pl.pallas_call(
        functools.partial(_mix_kernel, alpha),
        grid=(n // tt,),
        in_specs=[pl.BlockSpec((tt, D), lambda i: (i, 0)), pl.BlockSpec((tt, hk), lambda i: (i, 0)),
                  pl.BlockSpec((tt, hk), lambda i: (i, 0)), pl.BlockSpec((tt * hk, D // 2), lambda i: (i, 0)),
                  const((1, D)), const((1, D))],
        out_specs=pl.BlockSpec((tt, D), lambda i: (i, 0)),
        out_shape=jax.ShapeDtypeStruct((n, D), F32),
        compiler_params=_cparams(("parallel",)),
        name="peer_mix_ln",
    )(h1, scores, gates, vg, g.reshape(1, D).astype(F32), b.reshape(1, D).astype(F32))


PEER_CHUNK = 4096
SEQ_PIECE = 4096
EDGE_PIECE = 1024
MIX_LAG = 2


def _pieces(S, first, last):
    cuts = list(range(0, S, min(SEQ_PIECE, S))) + [S]
    if first and EDGE_PIECE < cuts[1]:
        cuts.insert(1, EDGE_PIECE)
    if last and S - EDGE_PIECE > cuts[-2]:
        cuts.insert(-1, S - EDGE_PIECE)
    return [(a, b - a) for a, b in zip(cuts[:-1], cuts[1:])]


def _peer_ffn_ln(alpha, h1, wq, k1, k2, up, vp, g, b):
    eT, gT = _route(h1, wq, k1, k2)
    return _peer_experts_ln(alpha, h1, eT, gT, up, vp, g, b)


def _peer_experts_ln(alpha, h1, eT, gT, up, vp, g, b):
    n, D = h1.shape
    hk = PEER_HEADS * PEER_TOPK
    experts = eT.T
    gates = gT.T
    ch = min(PEER_CHUNK, n)
    outs = []
    for c in range(n // ch):
        hc = h1[c * ch:(c + 1) * ch]
        scores, vg = _sc_peer(up, vp, experts[c * ch:(c + 1) * ch].reshape(ch * hk), hc)
        outs.append(_mix(alpha, hc, scores.reshape(ch, hk), gates[c * ch:(c + 1) * ch], vg, g, b))
    return jnp.concatenate(outs, axis=0)


def kernel(x, w_in, w_out, hg_lb, hg_norm_g, cmpk_pe, cmpk_w1, cmpk_b1, cmpk_w2, cmpk_b2,
           cmpv_pe, cmpv_w1, cmpv_b1, cmpv_w2, cmpv_b2, ln1_g, ln1_b,
           peer_wq, peer_k1, peer_k2, peer_u, peer_v, ln2_g, ln2_b):
    B, S, D = x.shape
    depth = w_in.shape[0]
    alpha = (2.0 * depth) ** 0.25
    h = x
    for l in range(depth):
        up, vp = _pack_rows(peer_u[l]), _pack_rows(peer_v[l])
        cmpk = (cmpk_pe[l], cmpk_w1[l], cmpk_b1[l], cmpk_w2[l], cmpk_b2[l])
        cmpv = (cmpv_pe[l], cmpv_w1[l], cmpv_b1[l], cmpv_w2[l], cmpv_b2[l])
        outs = []
        for bi in range(B):
            hb = h[bi:bi + 1]
            hg, qT, kn, ksE, vsT, vwT, gT = _project(hb, w_in[l])
            o_hg = _hgrn2(hg, hg_lb, hg_norm_g[l], l)
            cmp_n, cmp_T = _compress(kn, cmpk, cmpv)
            for lo, piece in _pieces(S, bi == 0, bi == B - 1):
                if len(outs) >= MIX_LAG:
                    qT, outs[-MIX_LAG] = lax.optimization_barrier((qT, outs[-MIX_LAG]))
                o_nsa = _nsa(qT, cmp_n, cmp_T, kn, ksE, vsT, vwT, gT, lo, piece)
                h1 = _outproj(alpha, o_hg[0, lo:lo + piece], o_nsa.reshape(piece, NSA_WIDTH),
                              hb[0, lo:lo + piece], w_out[l], ln1_g[l], ln1_b[l])
                eT, egT = _route(h1, peer_wq[l], peer_k1[l], peer_k2[l])
                qT, eT, egT = lax.optimization_barrier((qT, eT, egT))
                outs.append(_peer_experts_ln(alpha, h1, eT, egT, up, vp, ln2_g[l], ln2_b[l]))
        h = jnp.concatenate(outs, axis=0).reshape(B, S, D)
    return h
```

```python
import dataclasses
import functools
import math

import jax
import jax.numpy as jnp
import numpy as np
from jax import lax
from jax.experimental import pallas as pl
from jax.experimental.pallas import tpu as pltpu
from jax.experimental.pallas import tpu_sc as plsc

F32 = jnp.float32
BF16 = jnp.bfloat16

HG_HEADS = 4
HG_DK = 128
HG_DV = 128
HG_CHUNK = 64
HG_SUB = 16
HG_WIDTH = HG_HEADS * HG_DV
NSA_HEADS = 8
NSA_KV_HEADS = 2
NSA_GROUP = NSA_HEADS // NSA_KV_HEADS
NSA_HD = 64
NSA_WIDTH = NSA_HEADS * NSA_HD
CMP_LEN = 32
CMP_STRIDE = 16
CMP_HIDDEN = 256
SLC_LEN = 64
SLC_TOPK = 16
WIN = 512
FORCE_SCORE = 1.0e4
PEER_HEADS = 8
PEER_DQ = 256
PEER_NKEYS = 128
PEER_TOPK = 16
LN_EPS = 1e-5
NEG = -1e30
LOG2E = math.log2(math.e)

LANES = 128
PROJ_TM = 512
HG_TC = 256
ATT_TQ = 128
ATT_KT = 512
ATT_KS = 1024
OUT_TM = 512
ROUTE_TR = 256
MIX_TT = 16
MIX_SPLIT = 4
SC_WORKERS = 32
SC_LANES = 16
SC_UWIN = 16
SC_USLOT = 4
SC_VWIN = 16
SC_VSLOT = 8
SC_XB = 8
VMEM_LIMIT = 56 * 1024 * 1024


def _cparams(sem):
    return pltpu.CompilerParams(dimension_semantics=sem, vmem_limit_bytes=VMEM_LIMIT)


def _nt(a, b):
    return lax.dot_general(a, b, (((1,), (1,)), ((), ())), preferred_element_type=F32)


def _tn(a, b):
    return lax.dot_general(a, b, (((0,), (0,)), ((), ())), preferred_element_type=F32)


def _nn(a, b):
    return jnp.dot(a, b, preferred_element_type=F32)


def _proj_kernel(x_ref, whg_ref, wqT_ref, wkn_ref, wvT_ref, wgT_ref,
                 hg_ref, qT_ref, kn_ref, ksE_ref, vsT_ref, vwT_ref, gT_ref):
    xb = x_ref[0].astype(BF16)
    hg_ref[0] = _nn(xb, whg_ref[...])
    qT_ref[0] = (_nt(wqT_ref[...], xb) * (NSA_HD ** -0.5 * LOG2E)).astype(BF16)
    kn = _nn(xb, wkn_ref[...])
    for j in range(3):
        for g in range(NSA_KV_HEADS):
            lo = j * 2 * NSA_HD + g * NSA_HD
            kn_ref[0, j, g] = kn[:, lo:lo + NSA_HD].astype(BF16)
    row = pl.program_id(1) * kn.shape[0] + lax.broadcasted_iota(jnp.int32, (kn.shape[0], NSA_HD), 0)
    blk = (row >> int(math.log2(SLC_LEN))) & (ATT_KS // SLC_LEN - 1)
    onehot = (lax.broadcasted_iota(jnp.int32, (kn.shape[0], NSA_HD), 1) == blk).astype(F32)
    for g in range(NSA_KV_HEADS):
        lo = 3 * 2 * NSA_HD + g * NSA_HD
        ksE_ref[0, g] = jnp.concatenate([kn[:, lo:lo + NSA_HD], onehot], axis=1).astype(BF16)
    vT = _nt(wvT_ref[...], xb).astype(BF16)
    tm = vT.shape[1]
    for g in range(NSA_KV_HEADS):
        for c in range(tm // ATT_KT):
            vsT_ref[0, g, c] = vT[g * NSA_HD:(g + 1) * NSA_HD, c * ATT_KT:(c + 1) * ATT_KT]
        for c in range(tm // LANES):
            vwT_ref[0, g, c] = vT[(2 + g) * NSA_HD:(3 + g) * NSA_HD, c * LANES:(c + 1) * LANES]
    gT = _nt(wgT_ref[...], xb)
    gT_ref[0, 0] = gT[0:16]
    gT_ref[0, 1] = gT[16:32]


def _project(x, w_in_l):
    B, S, D = x.shape
    tm = PROJ_TM
    o = np.cumsum((0, 512, 512, 512, 512, 512, 128, 128, 128, 128, 128, 128, 24))
    wb = w_in_l.astype(BF16)
    whg = wb[:, o[0]:o[4]]
    wqT = wb[:, o[4]:o[5]].T
    wkn = jnp.concatenate([wb[:, o[5]:o[6]], wb[:, o[6]:o[7]], wb[:, o[9]:o[10]], wb[:, o[7]:o[8]]], axis=1)
    wvT = jnp.concatenate([wb[:, o[8]:o[9]], wb[:, o[10]:o[11]]], axis=1).T
    wg = wb[:, o[11]:o[12]].reshape(D, NSA_KV_HEADS, NSA_GROUP * 3)
    wgT = jnp.pad(wg, ((0, 0), (0, 0), (0, 16 - NSA_GROUP * 3))).reshape(D, 32).T
    const = lambda shp: pl.BlockSpec(shp, lambda b, i: (0,) * len(shp))
    out_shape = (
        jax.ShapeDtypeStruct((B, S, 4 * 512), F32),
        jax.ShapeDtypeStruct((B, NSA_WIDTH, S), BF16),
        jax.ShapeDtypeStruct((B, 3, NSA_KV_HEADS, S, NSA_HD), BF16),
        jax.ShapeDtypeStruct((B, NSA_KV_HEADS, S, 2 * NSA_HD), BF16),
        jax.ShapeDtypeStruct((B, NSA_KV_HEADS, S // ATT_KT, NSA_HD, ATT_KT), BF16),
        jax.ShapeDtypeStruct((B, NSA_KV_HEADS, S // LANES, NSA_HD, LANES), BF16),
        jax.ShapeDtypeStruct((B, NSA_KV_HEADS, 16, S), F32),
    )
    out_specs = (
        pl.BlockSpec((1, tm, 2048), lambda b, i: (b, i, 0)),
        pl.BlockSpec((1, NSA_WIDTH, tm), lambda b, i: (b, 0, i)),
        pl.BlockSpec((1, 3, NSA_KV_HEADS, tm, NSA_HD), lambda b, i: (b, 0, 0, i, 0)),
        pl.BlockSpec((1, NSA_KV_HEADS, tm, 2 * NSA_HD), lambda b, i: (b, 0, i, 0)),
        pl.BlockSpec((1, NSA_KV_HEADS, tm // ATT_KT, NSA_HD, ATT_KT), lambda b, i: (b, 0, i, 0, 0)),
        pl.BlockSpec((1, NSA_KV_HEADS, tm // LANES, NSA_HD, LANES), lambda b, i: (b, 0, i, 0, 0)),
        pl.BlockSpec((1, NSA_KV_HEADS, 16, tm), lambda b, i: (b, 0, 0, i)),
    )
    return pl.pallas_call(
        _proj_kernel,
        grid=(B, S // tm),
        in_specs=[pl.BlockSpec((1, tm, D), lambda b, i: (b, i, 0)),
                  const(whg.shape), const(wqT.shape), const(wkn.shape), const(wvT.shape), const(wgT.shape)],
        out_specs=out_specs,
        out_shape=out_shape,
        compiler_params=_cparams(("parallel", "arbitrary")),
        name="in_proj",
    )(x, whg, wqT, wkn, wvT, wgT)


def _split3(a):
    hi = a.astype(BF16)
    r = a - hi.astype(F32)
    mid = r.astype(BF16)
    lo = (r - mid.astype(F32)).astype(BF16)
    return hi, mid, lo


def _hgrn_kernel(layer, hg_ref, lb_ref, ng_ref, o_ref, st_ref):
    C, SB, H, DK = HG_CHUNK, HG_SUB, HG_HEADS, HG_DK
    nsb = C // SB

    @pl.when(pl.program_id(1) == 0)
    def _():
        st_ref[...] = jnp.zeros_like(st_ref)

    lbp = lb_ref[...]
    e = jnp.exp(lbp - jnp.max(lbp, axis=0, keepdims=True))
    lb = jnp.sum(e[:layer + 1], axis=0, keepdims=True) / jnp.sum(e, axis=0, keepdims=True)

    ri = lax.broadcasted_iota(jnp.int32, (C, C), 0)
    ci = lax.broadcasted_iota(jnp.int32, (C, C), 1)
    tril = (ci <= ri).astype(BF16)
    t_sub = lax.broadcasted_iota(jnp.int32, (SB, H * DK), 0)
    lane16 = lax.broadcasted_iota(jnp.int32, (SB, SB), 1)
    ng = ng_ref[...]

    def chunk(c, carry):
        r0 = pl.multiple_of(c * C, C)
        blk = hg_ref[0, pl.ds(r0, C), :]
        q = blk[:, 0:512]
        f = lb + (1.0 - lb) * jax.nn.sigmoid(blk[:, 512:1024])
        lf = jnp.log(f)
        k = 1.0 - f
        v = blk[:, 1024:1536]
        gate = blk[:, 1536:2048]
        hi, mid, lo = _split3(lf)
        b = _nn(tril, hi) + _nn(tril, mid) + _nn(tril, lo)
        b_last = b[C - 1:C, :]
        qe = (q * jnp.exp(b)).astype(BF16)
        ks = (k * jnp.exp(b_last - b)).astype(BF16)
        dec = jnp.exp(b_last)
        vb = v.astype(BF16)
        rblk = jnp.concatenate([jnp.broadcast_to(b[i * SB:i * SB + 1, :], (SB, H * DK)) for i in range(nsb)], axis=0)
        qn = (q * jnp.exp(b - rblk)).astype(BF16)
        outs = []
        for h in range(H):
            sl = slice(h * DK, (h + 1) * DK)
            stT = st_ref[h]
            o_h = _nt(qe[:, sl], stT.astype(BF16))
            rows = []
            for i in range(nsb):
                rs = slice(i * SB, (i + 1) * SB)
                acc = jnp.zeros((SB, HG_DV), F32)
                if i > 0:
                    ke = (k[0:i * SB, sl] * jnp.exp(b[i * SB:i * SB + 1, sl] - b[0:i * SB, sl])).astype(BF16)
                    a_off = _nt(qn[rs, sl], ke)
                    acc = acc + _nn(a_off.astype(BF16), vb[0:i * SB, sl])
                qi, ki, bi = q[rs, sl], k[rs, sl], b[rs, sl]
                a_d = jnp.zeros((SB, SB), F32)
                for s in range(SB):
                    w = qi * ki[s:s + 1, :] * jnp.exp(jnp.minimum(bi - bi[s:s + 1, :], 0.0))
                    col = jnp.sum(w, axis=1, keepdims=True)
                    a_d = a_d + jnp.where((lane16 == s) & (t_sub[:, 0:SB] >= s), col, 0.0)
                acc = acc + _nn(a_d.astype(BF16), vb[rs, sl])
                rows.append(acc)
            o_h = o_h + jnp.concatenate(rows, axis=0)
            st_ref[h] = dec[:, sl] * stT + _tn(vb[:, sl], ks[:, sl])
            o_h = o_h * lax.rsqrt(jnp.mean(o_h * o_h, axis=1, keepdims=True) + LN_EPS) * ng
            outs.append(o_h)
        o = jnp.concatenate(outs, axis=1) * jax.nn.silu(gate)
        o_ref[0, pl.ds(r0, C), :] = o.astype(BF16)
        return carry

    lax.fori_loop(0, hg_ref.shape[1] // C, chunk, 0)


def _hgrn2(hg, hg_lb, norm_g_l, layer):
    B, S, _ = hg.shape
    tc = HG_TC
    return pl.pallas_call(
        functools.partial(_hgrn_kernel, layer),
        grid=(B, S // tc),
        in_specs=[pl.BlockSpec((1, tc, 2048), lambda b, i: (b, i, 0)),
                  pl.BlockSpec(hg_lb.shape, lambda b, i: (0, 0)),
                  pl.BlockSpec((1, HG_DV), lambda b, i: (0, 0))],
        out_specs=pl.BlockSpec((1, tc, HG_WIDTH), lambda b, i: (b, i, 0)),
        out_shape=jax.ShapeDtypeStruct((B, S, HG_WIDTH), BF16),
        scratch_shapes=[pltpu.VMEM((HG_HEADS, HG_DV, HG_DK), F32)],
        compiler_params=_cparams(("parallel", "arbitrary")),
        name="hgrn2",
    )(hg, hg_lb.astype(F32), norm_g_l.reshape(1, HG_DV).astype(F32))


def _compress_kernel(c_ref, pe_ref, w1_ref, b1_ref, w2_ref, w2T_ref, b2_ref, b2T_ref, o_ref, oT_ref):
    half = CMP_STRIDE * NSA_HD
    c = c_ref[0, 0, 0].astype(F32)
    pe = pe_ref[0]
    ca = (c + pe[:, 0:half]).astype(BF16)
    cb = (c + pe[:, half:2 * half]).astype(BF16)
    pa = _nn(ca, w1_ref[0, 0:half, :])
    pb = _nn(cb, w1_ref[0, half:2 * half, :])
    nbp = pa.shape[0]
    h = pa + pltpu.roll(pb, nbp - 1, 0) + b1_ref[0]
    h = jax.nn.gelu(h).astype(BF16)
    o_ref[0, 0, 0] = (_nn(h, w2_ref[0]) + b2_ref[0]).astype(BF16)
    oT_ref[0, 0, 0] = (_nt(w2T_ref[0], h) + b2T_ref[0]).astype(BF16)


def _compress(kn, cmpk, cmpv):
    B, _, G, S, HD = kn.shape
    nbp = S // CMP_STRIDE
    c = kn[:, 0:2].reshape(B, 2, G, nbp, CMP_STRIDE * HD)
    stack = lambda a, b, f: jnp.stack([f(a), f(b)], axis=0)
    pe = stack(cmpk[0], cmpv[0], lambda t: t.reshape(1, CMP_LEN * HD).astype(F32))
    w1 = stack(cmpk[1], cmpv[1], lambda t: t.astype(BF16))
    b1 = stack(cmpk[2], cmpv[2], lambda t: t.reshape(1, CMP_HIDDEN).astype(F32))
    w2 = stack(cmpk[3], cmpv[3], lambda t: t.astype(BF16))
    w2T = stack(cmpk[3], cmpv[3], lambda t: t.astype(BF16).T)
    b2 = stack(cmpk[4], cmpv[4], lambda t: t.reshape(1, HD).astype(F32))
    b2T = stack(cmpk[4], cmpv[4], lambda t: t.reshape(HD, 1).astype(F32))
    wspec = lambda a: pl.BlockSpec((1,) + a.shape[1:], lambda b, t, g: (t,) + (0,) * (a.ndim - 1))
    return pl.pallas_call(
        _compress_kernel,
        grid=(B, 2, G),
        in_specs=[pl.BlockSpec((1, 1, 1, nbp, CMP_STRIDE * HD), lambda b, t, g: (b, t, g, 0, 0)),
                  wspec(pe), wspec(w1), wspec(b1), wspec(w2), wspec(w2T), wspec(b2), wspec(b2T)],
        out_specs=(pl.BlockSpec((1, 1, 1, nbp, HD), lambda b, t, g: (b, t, g, 0, 0)),
                   pl.BlockSpec((1, 1, 1, HD, nbp), lambda b, t, g: (b, t, g, 0, 0))),
        out_shape=(jax.ShapeDtypeStruct((B, 2, G, nbp, HD), BF16),
                   jax.ShapeDtypeStruct((B, 2, G, HD, nbp), BF16)),
        compiler_params=_cparams(("parallel", "parallel", "parallel")),
        name="kv_compress",
    )(c, pe, w1, b1, w2, w2T, b2, b2T)


def _topk_mask_rows(score, k):
    R = score.shape[0]
    r_io = lax.broadcasted_iota(jnp.int32, score.shape, 0)
    s = score
    for _ in range(k):
        m = jnp.max(s, axis=0, keepdims=True)
        idx = jnp.min(jnp.where(s == m, r_io, R), axis=0, keepdims=True)
        s = jnp.where(r_io == idx, -jnp.inf, s)
    return (s == -jnp.inf).astype(F32)


def _with_ones_row(vT):
    pad = (lax.broadcasted_iota(jnp.int32, (16, vT.shape[1]), 0) == 0).astype(BF16)
    return jnp.concatenate([vT, pad], axis=0)


def _nsa_kernel(q_off, qT_ref, kc_ref, vcT_ref, ks_ref, vsT_ref, kw_ref, vwT_ref, gT_ref, ovT_ref,
                o_ref, sel_ref):
    TQ, G, HD, KT, NG = ATT_TQ, NSA_GROUP, NSA_HD, ATT_KT, NSA_KV_HEADS
    qi = pl.program_id(1) + q_off
    t0 = qi * TQ
    pos1 = t0 + lax.broadcasted_iota(jnp.int32, (1, TQ), 1)
    pos4 = jnp.concatenate([pos1] * G, axis=1)
    nbp = kc_ref.shape[3]
    ns = ovT_ref.shape[0]
    n_io = lax.broadcasted_iota(jnp.int32, (nbp, 1), 0)
    mask_c = (n_io * CMP_STRIDE + (CMP_LEN - 1)) <= pos4
    j_io = lax.broadcasted_iota(jnp.int32, (ns, 1), 0)
    cur = pos1 >> int(math.log2(SLC_LEN))
    forced = (j_io == 0) | (j_io == cur) | (j_io == cur - 1)
    causal_blk = j_io * SLC_LEN <= pos1

    q4s, o_cs = [], []
    for g in range(NG):
        qT = qT_ref[0, g * G * HD:(g + 1) * G * HD, :]
        q4 = jnp.concatenate([qT[r * HD:(r + 1) * HD, :] for r in range(G)], axis=1)
        s_c = jnp.where(mask_c, _nn(kc_ref[0, 0, g], q4), NEG)
        p_c = jnp.where(mask_c, jnp.exp2(s_c - jnp.max(s_c, axis=0, keepdims=True)), 0.0)
        p_c = p_c / jnp.maximum(jnp.sum(p_c, axis=0, keepdims=True), 1e-30)
        o_cs.append(_nn(vcT_ref[0, 0, g], p_c.astype(BF16)))
        p_sum = p_c[:, 0:TQ]
        for r in range(1, G):
            p_sum = p_sum + p_c[:, r * TQ:(r + 1) * TQ]
        imp = _nn(ovT_ref[...], p_sum.astype(BF16))
        score = jnp.where(forced, FORCE_SCORE, jnp.where(causal_blk, imp, -1.0))
        sel_ref[g] = (_topk_mask_rows(score, min(SLC_TOPK, ns)) - 1.0) * (-NEG)
        q4s.append(q4)

    KS = ATT_KS
    bps = KS // SLC_LEN
    k_io = lax.broadcasted_iota(jnp.int32, (KS, 1), 0)
    zrows = jnp.zeros((HD - bps, G * TQ), BF16)

    def sel_step(ks_i, carry, diagonal):
        k0 = pl.multiple_of(ks_i * KS, KS)
        out = []
        for g in range(NG):
            m, acc = carry[g]
            bias = sel_ref[g, pl.ds(pl.multiple_of(ks_i * bps, bps), bps), :].astype(BF16)
            rhs = jnp.concatenate([q4s[g], jnp.concatenate([bias] * G, axis=1), zrows], axis=0)
            s = _nn(ks_ref[0, g, pl.ds(k0, KS), :], rhs)
            if diagonal:
                s = jnp.where((k0 + k_io) <= pos4, s, NEG)
            m_new = jnp.maximum(m, jnp.max(s, axis=0, keepdims=True))
            p = jnp.exp2(s - m_new).astype(BF16)
            acc = jnp.exp2(m - m_new) * acc
            for c in range(KS // KT):
                acc = acc + _nn(_with_ones_row(vsT_ref[0, g, ks_i * (KS // KT) + c]), p[c * KT:(c + 1) * KT])
            out.append((m_new, acc))
        return tuple(out)

    n_ks = (t0 + TQ - 1) // KS + 1
    init = tuple((jnp.full((1, G * TQ), NEG, F32), jnp.zeros((HD + 16, G * TQ), F32)) for _ in range(NG))
    carry = lax.fori_loop(0, n_ks - 1, lambda i, c: sel_step(i, c, False), init)
    carry = sel_step(n_ks - 1, carry, True)

    nwt = WIN // TQ + 1
    r_io = lax.broadcasted_iota(jnp.int32, (TQ, 1), 0)
    idxs = [qi - (nwt - 1) + i for i in range(nwt)]
    kpos = jnp.concatenate([idx * TQ + r_io for idx in idxs], axis=0)
    delta = pos4 - kpos
    mask_w = (delta >= 0) & (delta < WIN) & (kpos >= 0)
    outs = []
    for g in range(NG):
        acc_s = carry[g][1]
        o_s = acc_s[0:HD] / jnp.maximum(acc_s[HD:HD + 1], 1e-30)
        kw_t = [kw_ref[0, 0, g, pl.ds(pl.multiple_of(jnp.maximum(idx, 0) * TQ, TQ), TQ), :] for idx in idxs]
        vw_t = [vwT_ref[0, g, jnp.maximum(idx, 0)] for idx in idxs]
        s_w = jnp.where(mask_w, _nn(jnp.concatenate(kw_t, axis=0), q4s[g]), NEG)
        p_w = jnp.exp2(s_w - jnp.max(s_w, axis=0, keepdims=True)).astype(BF16)
        acc_w = _nn(_with_ones_row(jnp.concatenate(vw_t, axis=1)), p_w)
        o_w = acc_w[0:HD] / jnp.maximum(acc_w[HD:HD + 1], 1e-30)
        gs = jax.nn.sigmoid(gT_ref[0, g])
        gate = lambda br: jnp.concatenate([gs[r * 3 + br:r * 3 + br + 1, :] for r in range(G)], axis=1)
        o = gate(0) * o_cs[g] + gate(1) * o_s + gate(2) * o_w
        outs.extend(o[:, r * TQ:(r + 1) * TQ].T for r in range(G))
    o_ref[0] = jnp.concatenate(outs, axis=1).astype(BF16)


def _overlap_T(S):
    nbp = S // CMP_STRIDE
    n_slc = S // SLC_LEN
    cs = np.arange(nbp) * CMP_STRIDE
    ss = np.arange(n_slc) * SLC_LEN
    ov = np.clip(np.minimum(cs[:, None] + CMP_LEN, ss[None, :] + SLC_LEN)
                 - np.maximum(cs[:, None], ss[None, :]), 0, None) / CMP_LEN
    ov[nbp - 1:, :] = 0.0
    return jnp.asarray(ov.T, BF16)


def _nsa(qT, cmp_n, cmp_T, kn, ksE, vsT, vwT, gT, q_lo, q_len):
    B, _, S = qT.shape
    G2, TQ, HD = NSA_KV_HEADS, ATT_TQ, NSA_HD
    nbp = S // CMP_STRIDE
    ns = S // SLC_LEN
    ovT = _overlap_T(S)
    q_off = q_lo // TQ
    return pl.pallas_call(
        functools.partial(_nsa_kernel, q_off),
        grid=(B, q_len // TQ),
        in_specs=[
            pl.BlockSpec((1, NSA_WIDTH, TQ), lambda b, i: (b, 0, i + q_off)),
            pl.BlockSpec((1, 1, G2, nbp, HD), lambda b, i: (b, 0, 0, 0, 0)),
            pl.BlockSpec((1, 1, G2, HD, nbp), lambda b, i: (b, 1, 0, 0, 0)),
            pl.BlockSpec((1, G2, S, 2 * HD), lambda b, i: (b, 0, 0, 0)),
            pl.BlockSpec((1, G2, S // ATT_KT, HD, ATT_KT), lambda b, i: (b, 0, 0, 0, 0)),
            pl.BlockSpec((1, 1, G2, S, HD), lambda b, i: (b, 2, 0, 0, 0)),
            pl.BlockSpec((1, G2, S // LANES, HD, LANES), lambda b, i: (b, 0, 0, 0, 0)),
            pl.BlockSpec((1, G2, 16, TQ), lambda b, i: (b, 0, 0, i + q_off)),
            pl.BlockSpec((ns, nbp), lambda b, i: (0, 0)),
        ],
        out_specs=pl.BlockSpec((1, TQ, NSA_WIDTH), lambda b, i: (b, i, 0)),
        out_shape=jax.ShapeDtypeStruct((B, q_len, NSA_WIDTH), BF16),
        scratch_shapes=[pltpu.VMEM((G2, ns, TQ), F32)],
        compiler_params=_cparams(("parallel", "arbitrary")),
        name="nsa_attn",
    )(qT, cmp_n, cmp_T, ksE, vsT, kn, vwT, gT, ovT)


def _layernorm(t, g, b):
    mu = jnp.mean(t, axis=-1, keepdims=True)
    d = t - mu
    var = jnp.mean(d * d, axis=-1, keepdims=True)
    return d * lax.rsqrt(var + LN_EPS) * g + b


def _outproj_kernel(alpha, ohg_ref, onsa_ref, x_ref, w_ref, g_ref, b_ref, h_ref):
    mix = _nn(ohg_ref[...], w_ref[0:HG_WIDTH, :]) + _nn(onsa_ref[...], w_ref[HG_WIDTH:HG_WIDTH + NSA_WIDTH, :])
    h_ref[...] = _layernorm(alpha * x_ref[...] + mix, g_ref[...], b_ref[...])


def _outproj(alpha, o_hg, o_nsa, x2, w_out_l, g, b):
    n, D = x2.shape
    tm = OUT_TM
    row = lambda w: pl.BlockSpec((tm, w), lambda i: (i, 0))
    const = lambda shp: pl.BlockSpec(shp, lambda i: (0, 0))
    return pl.pallas_call(
        functools.partial(_outproj_kernel, alpha),
        grid=(n // tm,),
        in_specs=[row(HG_WIDTH), row(NSA_WIDTH), row(D), const(w_out_l.shape), const((1, D)), const((1, D))],
        out_specs=row(D),
        out_shape=jax.ShapeDtypeStruct((n, D), F32),
        compiler_params=_cparams(("parallel",)),
        name="out_proj_ln",
    )(o_hg, o_nsa, x2, w_out_l.astype(BF16), g.reshape(1, D).astype(F32), b.reshape(1, D).astype(F32))


def _topk_rows(s, k):
    R = s.shape[0]
    r_io = lax.broadcasted_iota(jnp.int32, s.shape, 0)
    vals, idxs = [], []
    for _ in range(k):
        m = jnp.max(s, axis=0, keepdims=True)
        idx = jnp.min(jnp.where(s == m, r_io, R), axis=0, keepdims=True)
        vals.append(m)
        idxs.append(idx)
        s = jnp.where(r_io == idx, -jnp.inf, s)
    return jnp.concatenate(vals, axis=0), jnp.concatenate(idxs, axis=0)


def _route_kernel(h_ref, wq_ref, k1_ref, k2_ref, e_ref, g_ref):
    K, half = PEER_TOPK, PEER_DQ // 2
    q = _nn(h_ref[...].astype(BF16), wq_ref[...]).astype(BF16)
    for h in range(PEER_HEADS):
        s1 = _nt(k1_ref[...], q[:, h * PEER_DQ:h * PEER_DQ + half])
        s2 = _nt(k2_ref[...], q[:, h * PEER_DQ + half:(h + 1) * PEER_DQ])
        v1, i1 = _topk_rows(s1, K)
        v2, i2 = _topk_rows(s2, K)
        sub = 8
        b_io = lax.broadcasted_iota(jnp.int32, (sub, 1), 0)
        cands = [v1[0:1, :] + v2]
        ids = [i1[0:1, :] * PEER_NKEYS + i2]
        for a in range(1, sub):
            cands.append(jnp.where(b_io < K // (a + 1), v1[a:a + 1, :] + v2[0:sub, :], -jnp.inf))
            ids.append(i1[a:a + 1, :] * PEER_NKEYS + i2[0:sub, :])
        cands.append(v1[sub:K, :] + v2[0:1, :])
        ids.append(i1[sub:K, :] * PEER_NKEYS + i2[0:1, :])
        cand = jnp.concatenate(cands, axis=0)
        cand_id = jnp.concatenate(ids, axis=0)
        r_io = lax.broadcasted_iota(jnp.int32, cand.shape, 0)
        top_s, top_e = [], []
        for _ in range(K):
            m = jnp.max(cand, axis=0, keepdims=True)
            hit = r_io == jnp.min(jnp.where(cand == m, r_io, cand.shape[0]), axis=0, keepdims=True)
            top_s.append(m)
            top_e.append(jnp.sum(jnp.where(hit, cand_id, 0), axis=0, keepdims=True))
            cand = jnp.where(hit, -jnp.inf, cand)
        top_s = jnp.concatenate(top_s, axis=0)
        e_ref[h * K:(h + 1) * K, :] = jnp.concatenate(top_e, axis=0)
        ex = jnp.exp(top_s - top_s[0:1, :])
        g_ref[h * K:(h + 1) * K, :] = ex / jnp.sum(ex, axis=0, keepdims=True)


def _route(h1, wq, k1, k2):
    n, D = h1.shape
    tr = ROUTE_TR
    hk = PEER_HEADS * PEER_TOPK
    const = lambda shp: pl.BlockSpec(shp, lambda i: (0, 0))
    return pl.pallas_call(
        _route_kernel,
        grid=(n // tr,),
        in_specs=[pl.BlockSpec((tr, D), lambda i: (i, 0)), const(wq.shape), const(k1.shape), const(k2.shape)],
        out_specs=(pl.BlockSpec((hk, tr), lambda i: (0, i)), pl.BlockSpec((hk, tr), lambda i: (0, i))),
        out_shape=(jax.ShapeDtypeStruct((hk, n), jnp.int32), jax.ShapeDtypeStruct((hk, n), F32)),
        compiler_params=_cparams(("parallel",)),
        name="peer_route",
    )(h1, wq.astype(BF16), k1.astype(BF16), k2.astype(BF16))


def _pack_rows(t):
    half = t.shape[1] // 2
    tb = t.astype(BF16)
    lo = lax.bitcast_convert_type(tb[:, :half], jnp.uint16).astype(jnp.uint32)
    hi = lax.bitcast_convert_type(tb[:, half:], jnp.uint16).astype(jnp.uint32)
    return lo | (hi << 16)


def _sc_peer(utab, vtab, idx, x):
    n = x.shape[0]
    w = utab.shape[1]
    hk = PEER_HEADS * PEER_TOPK
    L = SC_LANES
    T = n // SC_WORKERS
    uwin = SC_UWIN
    upt = hk // uwin
    vpt = hk // SC_VWIN
    per_u = vpt // upt
    vlead = SC_VSLOT // 2
    assert n % SC_WORKERS == 0 and T % SC_XB == 0 and uwin % L == 0
    assert upt % SC_USLOT == 0 and vpt % SC_VSLOT == 0 and vpt % upt == 0
    vsteps = T * vpt
    mesh = plsc.VectorSubcoreMesh(core_axis_name="c", subcore_axis_name="s")
    cp = dataclasses.replace(pltpu.CompilerParams(), needs_layout_passes=False)

    @pl.kernel(out_type=(jax.ShapeDtypeStruct((n * hk,), F32), jax.ShapeDtypeStruct((n * hk, w), utab.dtype)),
               mesh=mesh, compiler_params=cp,
               scratch_types=[pltpu.VMEM((T * hk,), jnp.int32),
                              pltpu.VMEM((SC_XB, 2 * w), F32),
                              pltpu.VMEM((SC_USLOT, uwin, w), utab.dtype),
                              pltpu.VMEM((SC_VSLOT, SC_VWIN, w), vtab.dtype),
                              pltpu.VMEM((L, L), F32),
                              pltpu.VMEM((SC_XB * hk,), F32),
                              pltpu.SemaphoreType.DMA((SC_USLOT,)),
                              pltpu.SemaphoreType.DMA((SC_VSLOT,)),
                              pltpu.SemaphoreType.DMA((SC_VSLOT,))])
    def peer(u_hbm, v_hbm, i_hbm, x_hbm, h_hbm, vg_hbm, idx_v, x_v, ubuf, vbuf, acc_s, h_v, usem, vsem, wsem):
        wid = lax.axis_index("c") * (SC_WORKERS // 2) + lax.axis_index("s")
        tok0 = wid * T
        row0 = tok0 * hk
        pltpu.sync_copy(i_hbm.at[pl.ds(row0, T * hk)], idx_v)

        def ufetch(t, j):
            slot = j % SC_USLOT
            return pltpu.make_async_copy(u_hbm.at[idx_v.at[pl.ds(t * hk + j * uwin, uwin)]], ubuf.at[slot], usem.at[slot])

        def vfetch(s, slot):
            return pltpu.make_async_copy(v_hbm.at[idx_v.at[pl.ds(s * SC_VWIN, SC_VWIN)]], vbuf.at[slot], vsem.at[slot])

        def vflush(s, slot):
            return pltpu.make_async_copy(vbuf.at[slot], vg_hbm.at[pl.ds(row0 + s * SC_VWIN, SC_VWIN)], wsem.at[slot])

        for j in range(SC_USLOT):
            ufetch(0, j).start()
        for j in range(vlead):
            vfetch(j, j).start()

        def vstep(s, slot):
            other = (slot + vlead) % SC_VSLOT
            vfetch(s, slot).wait()
            vflush(s, slot).start()

            @pl.when(s >= vlead)
            def _():
                vflush(s - vlead, other).wait()

            @pl.when(s + vlead < vsteps)
            def _():
                vfetch(s + vlead, other).start()

        lane = lax.iota(jnp.int32, L)

        @pl.loop(0, T // SC_XB)
        def _(tb):
            pltpu.sync_copy(x_hbm.at[pl.ds(pl.multiple_of(tok0 + tb * SC_XB, SC_XB), SC_XB)], x_v)

            @pl.loop(0, SC_XB)
            def _(tt):
                t = tb * SC_XB + tt
                for j in range(upt):
                    for i in range(per_u):
                        vstep(t * vpt + j * per_u + i, (j * per_u + i) % SC_VSLOT)
                    ufetch(t, j).wait()

                    def dot_body(jj, accs):
                        off = pl.multiple_of(jj * L, L)
                        xlo = x_v[tt, pl.ds(off, L)]
                        xhi = x_v[tt, pl.ds(w + off, L)]
                        out = []
                        for r in range(uwin):
                            wv = ubuf[j % SC_USLOT, r, pl.ds(off, L)]
                            lo = plsc.bitcast(wv << 16, F32)
                            hi = plsc.bitcast(wv & jnp.uint32(0xFFFF0000), F32)
                            out.append(accs[r] + lo * xlo + hi * xhi)
                        return tuple(out)

                    accs = lax.fori_loop(0, w // L, dot_body, tuple(jnp.zeros((L,), F32) for _ in range(uwin)))

                    if j + SC_USLOT < upt:
                        ufetch(t, j + SC_USLOT).start()
                    else:
                        @pl.when(t + 1 < T)
                        def _():
                            ufetch(t + 1, j + SC_USLOT - upt).start()

                    for g in range(uwin // L):
                        for r in range(L):
                            acc_s[r, :] = accs[g * L + r]
                        tot = jnp.zeros((L,), F32)
                        for c in range(L):
                            tot = tot + plsc.load_gather(acc_s, [lane, jnp.full((L,), c, jnp.int32)])
                        h_v[pl.ds(tt * hk + j * uwin + g * L, L)] = tot

            pltpu.sync_copy(h_v, h_hbm.at[pl.ds(pl.multiple_of(row0 + tb * SC_XB * hk, SC_XB * hk), SC_XB * hk)])

        for s in range(vsteps - vlead, vsteps):
            vflush(s, s % SC_VSLOT).wait()

    return peer(utab, vtab, idx, x)


def _unpack(w):
    lo = lax.bitcast_convert_type(w << 16, F32)
    hi = lax.bitcast_convert_type(w & jnp.uint32(0xFFFF0000), F32)
    return lo, hi


def _mix_kernel(alpha, h_ref, s_ref, gate_ref, *rest):
    vg_refs, (g_ref, b_ref, o_ref) = rest[:MIX_SPLIT], rest[MIX_SPLIT:]
    TT, HK = MIX_TT, PEER_HEADS * PEER_TOPK
    per = TT // MIX_SPLIT
    x = h_ref[...]
    wrow = gate_ref[...] * jax.nn.gelu(s_ref[...])
    eye = lax.broadcasted_iota(jnp.int32, (HK, HK), 0) == lax.broadcasted_iota(jnp.int32, (HK, HK), 1)
    rows = []
    for t in range(TT):
        wcol = jnp.sum(jnp.where(eye, jnp.broadcast_to(wrow[t:t + 1, :], (HK, HK)), 0.0),
                       axis=1, keepdims=True)
        vlo, vhi = _unpack(vg_refs[t // per][(t % per) * HK:(t % per + 1) * HK, :])
        rows.append(jnp.concatenate([jnp.sum(wcol * vlo, axis=0, keepdims=True),
                                     jnp.sum(wcol * vhi, axis=0, keepdims=True)], axis=1))
    ffn = jnp.concatenate(rows, axis=0)
    o_ref[...] = _layernorm(alpha * x + ffn, g_ref[...], b_ref[...])


def _mix(alpha, h1, scores, gates, vg, g, b):
    n, D = h1.shape
    tt = MIX_TT
    hk = PEER_HEADS * PEER_TOPK
    const = lambda shp: pl.BlockSpec(shp, lambda i: (0, 0))
    return pl.pallas_call(
        functools.partial(_mix_kernel, alpha),
        grid=(n // tt,),
        in_specs=[pl.BlockSpec((tt, D), lambda i: (i, 0)), pl.BlockSpec((tt, hk), lambda i: (i, 0)),
                  pl.BlockSpec((tt, hk), lambda i: (i, 0))]
                 + [pl.BlockSpec((tt // MIX_SPLIT * hk, D // 2), functools.partial(lambda q, i: (i * MIX_SPLIT + q, 0), q))
                    for q in range(MIX_SPLIT)]
                 + [const((1, D)), const((1, D))],
        out_specs=pl.BlockSpec((tt, D), lambda i: (i, 0)),
        out_shape=jax.ShapeDtypeStruct((n, D), F32),
        compiler_params=_cparams(("parallel",)),
        name="peer_mix_ln",
    )(h1, scores, gates, *([vg] * MIX_SPLIT), g.reshape(1, D).astype(F32), b.reshape(1, D).astype(F32))


PEER_CHUNK = 4096
SEQ_PIECE = 4096


def _peer_ffn_ln(alpha, h1, wq, k1, k2, up, vp, g, b):
    eT, gT = _route(h1, wq, k1, k2)
    return _peer_experts_ln(alpha, h1, eT, gT, up, vp, g, b)


def _peer_experts_ln(alpha, h1, eT, gT, up, vp, g, b):
    n, D = h1.shape
    hk = PEER_HEADS * PEER_TOPK
    experts = eT.T
    gates = gT.T
    ch = min(PEER_CHUNK, n)
    outs = []
    for c in range(n // ch):
        hc = h1[c * ch:(c + 1) * ch]
        scores, vg = _sc_peer(up, vp, experts[c * ch:(c + 1) * ch].reshape(ch * hk), hc)
        outs.append(_mix(alpha, hc, scores.reshape(ch, hk), gates[c * ch:(c + 1) * ch], vg, g, b))
    return jnp.concatenate(outs, axis=0)


def kernel(x, w_in, w_out, hg_lb, hg_norm_g, cmpk_pe, cmpk_w1, cmpk_b1, cmpk_w2, cmpk_b2,
           cmpv_pe, cmpv_w1, cmpv_b1, cmpv_w2, cmpv_b2, ln1_g, ln1_b,
           peer_wq, peer_k1, peer_k2, peer_u, peer_v, ln2_g, ln2_b):
    B, S, D = x.shape
    depth = w_in.shape[0]
    alpha = (2.0 * depth) ** 0.25
    h = x
    for l in range(depth):
        up, vp = _pack_rows(peer_u[l]), _pack_rows(peer_v[l])
        cmpk = (cmpk_pe[l], cmpk_w1[l], cmpk_b1[l], cmpk_w2[l], cmpk_b2[l])
        cmpv = (cmpv_pe[l], cmpv_w1[l], cmpv_b1[l], cmpv_w2[l], cmpv_b2[l])
        outs = []
        for bi in range(B):
            hb = h[bi:bi + 1]
            hg, qT, kn, ksE, vsT, vwT, gT = _project(hb, w_in[l])
            o_hg = _hgrn2(hg, hg_lb, hg_norm_g[l], l)
            cmp_n, cmp_T = _compress(kn, cmpk, cmpv)
            piece = min(SEQ_PIECE, S)
            for lo in range(0, S, piece):
                o_nsa = _nsa(qT, cmp_n, cmp_T, kn, ksE, vsT, vwT, gT, lo, piece)
                h1 = _outproj(alpha, o_hg[0, lo:lo + piece], o_nsa.reshape(piece, NSA_WIDTH),
                              hb[0, lo:lo + piece], w_out[l], ln1_g[l], ln1_b[l])
                eT, egT = _route(h1, peer_wq[l], peer_k1[l], peer_k2[l])
                qT, eT, egT = lax.optimization_barrier((qT, eT, egT))
                outs.append(_peer_experts_ln(alpha, h1, eT, egT, up, vp, ln2_g[l], ln2_b[l]))
        h = jnp.concatenate(outs, axis=0).reshape(B, S, D)
    return h
```

```python
import dataclasses
import functools
import math

import jax
import jax.numpy as jnp
import numpy as np
from jax import lax
from jax.experimental import pallas as pl
from jax.experimental.pallas import tpu as pltpu
from jax.experimental.pallas import tpu_sc as plsc

F32 = jnp.float32
BF16 = jnp.bfloat16

HG_HEADS = 4
HG_DK = 128
HG_DV = 128
HG_CHUNK = 64
HG_SUB = 16
HG_WIDTH = HG_HEADS * HG_DV
NSA_HEADS = 8
NSA_KV_HEADS = 2
NSA_GROUP = NSA_HEADS // NSA_KV_HEADS
NSA_HD = 64
NSA_WIDTH = NSA_HEADS * NSA_HD
CMP_LEN = 32
CMP_STRIDE = 16
CMP_HIDDEN = 256
SLC_LEN = 64
SLC_TOPK = 16
WIN = 512
FORCE_SCORE = 1.0e4
PEER_HEADS = 8
PEER_DQ = 256
PEER_NKEYS = 128
PEER_TOPK = 16
LN_EPS = 1e-5
NEG = -1e30
LOG2E = math.log2(math.e)

LANES = 128
PROJ_TM = 512
HG_TC = 256
ATT_TQ = 128
ATT_KT = 512
ATT_KS = 1024
OUT_TM = 512
ROUTE_TR = 256
MIX_TT = 16
SC_WORKERS = 32
SC_LANES = 16
SC_UWIN = 16
SC_USLOT = 4
SC_VWIN = 16
SC_VSLOT = 8
SC_XB = 8
VMEM_LIMIT = 56 * 1024 * 1024


def _cparams(sem):
    return pltpu.CompilerParams(dimension_semantics=sem, vmem_limit_bytes=VMEM_LIMIT)


def _nt(a, b):
    return lax.dot_general(a, b, (((1,), (1,)), ((), ())), preferred_element_type=F32)


def _tn(a, b):
    return lax.dot_general(a, b, (((0,), (0,)), ((), ())), preferred_element_type=F32)


def _nn(a, b):
    return jnp.dot(a, b, preferred_element_type=F32)


def _proj_kernel(x_ref, whg_ref, wqT_ref, wkn_ref, wvT_ref, wgT_ref,
                 hg_ref, qT_ref, kn_ref, ksE_ref, vsT_ref, vwT_ref, gT_ref):
    xb = x_ref[0].astype(BF16)
    hg_ref[0] = _nn(xb, whg_ref[...])
    qT_ref[0] = (_nt(wqT_ref[...], xb) * (NSA_HD ** -0.5 * LOG2E)).astype(BF16)
    kn = _nn(xb, wkn_ref[...])
    for j in range(3):
        for g in range(NSA_KV_HEADS):
            lo = j * 2 * NSA_HD + g * NSA_HD
            kn_ref[0, j, g] = kn[:, lo:lo + NSA_HD].astype(BF16)
    row = pl.program_id(1) * kn.shape[0] + lax.broadcasted_iota(jnp.int32, (kn.shape[0], NSA_HD), 0)
    blk = (row >> int(math.log2(SLC_LEN))) & (ATT_KS // SLC_LEN - 1)
    onehot = (lax.broadcasted_iota(jnp.int32, (kn.shape[0], NSA_HD), 1) == blk).astype(F32)
    for g in range(NSA_KV_HEADS):
        lo = 3 * 2 * NSA_HD + g * NSA_HD
        ksE_ref[0, g] = jnp.concatenate([kn[:, lo:lo + NSA_HD], onehot], axis=1).astype(BF16)
    vT = _nt(wvT_ref[...], xb).astype(BF16)
    tm = vT.shape[1]
    for g in range(NSA_KV_HEADS):
        for c in range(tm // ATT_KT):
            vsT_ref[0, g, c] = vT[g * NSA_HD:(g + 1) * NSA_HD, c * ATT_KT:(c + 1) * ATT_KT]
        for c in range(tm // LANES):
            vwT_ref[0, g, c] = vT[(2 + g) * NSA_HD:(3 + g) * NSA_HD, c * LANES:(c + 1) * LANES]
    gT = _nt(wgT_ref[...], xb)
    gT_ref[0, 0] = gT[0:16]
    gT_ref[0, 1] = gT[16:32]


def _project(x, w_in_l):
    B, S, D = x.shape
    tm = PROJ_TM
    o = np.cumsum((0, 512, 512, 512, 512, 512, 128, 128, 128, 128, 128, 128, 24))
    wb = w_in_l.astype(BF16)
    whg = wb[:, o[0]:o[4]]
    wqT = wb[:, o[4]:o[5]].T
    wkn = jnp.concatenate([wb[:, o[5]:o[6]], wb[:, o[6]:o[7]], wb[:, o[9]:o[10]], wb[:, o[7]:o[8]]], axis=1)
    wvT = jnp.concatenate([wb[:, o[8]:o[9]], wb[:, o[10]:o[11]]], axis=1).T
    wg = wb[:, o[11]:o[12]].reshape(D, NSA_KV_HEADS, NSA_GROUP * 3)
    wgT = jnp.pad(wg, ((0, 0), (0, 0), (0, 16 - NSA_GROUP * 3))).reshape(D, 32).T
    const = lambda shp: pl.BlockSpec(shp, lambda b, i: (0,) * len(shp))
    out_shape = (
        jax.ShapeDtypeStruct((B, S, 4 * 512), F32),
        jax.ShapeDtypeStruct((B, NSA_WIDTH, S), BF16),
        jax.ShapeDtypeStruct((B, 3, NSA_KV_HEADS, S, NSA_HD), BF16),
        jax.ShapeDtypeStruct((B, NSA_KV_HEADS, S, 2 * NSA_HD), BF16),
        jax.ShapeDtypeStruct((B, NSA_KV_HEADS, S // ATT_KT, NSA_HD, ATT_KT), BF16),
        jax.ShapeDtypeStruct((B, NSA_KV_HEADS, S // LANES, NSA_HD, LANES), BF16),
        jax.ShapeDtypeStruct((B, NSA_KV_HEADS, 16, S), F32),
    )
    out_specs = (
        pl.BlockSpec((1, tm, 2048), lambda b, i: (b, i, 0)),
        pl.BlockSpec((1, NSA_WIDTH, tm), lambda b, i: (b, 0, i)),
        pl.BlockSpec((1, 3, NSA_KV_HEADS, tm, NSA_HD), lambda b, i: (b, 0, 0, i, 0)),
        pl.BlockSpec((1, NSA_KV_HEADS, tm, 2 * NSA_HD), lambda b, i: (b, 0, i, 0)),
        pl.BlockSpec((1, NSA_KV_HEADS, tm // ATT_KT, NSA_HD, ATT_KT), lambda b, i: (b, 0, i, 0, 0)),
        pl.BlockSpec((1, NSA_KV_HEADS, tm // LANES, NSA_HD, LANES), lambda b, i: (b, 0, i, 0, 0)),
        pl.BlockSpec((1, NSA_KV_HEADS, 16, tm), lambda b, i: (b, 0, 0, i)),
    )
    return pl.pallas_call(
        _proj_kernel,
        grid=(B, S // tm),
        in_specs=[pl.BlockSpec((1, tm, D), lambda b, i: (b, i, 0)),
                  const(whg.shape), const(wqT.shape), const(wkn.shape), const(wvT.shape), const(wgT.shape)],
        out_specs=out_specs,
        out_shape=out_shape,
        compiler_params=_cparams(("parallel", "arbitrary")),
        name="in_proj",
    )(x, whg, wqT, wkn, wvT, wgT)


def _split3(a):
    hi = a.astype(BF16)
    r = a - hi.astype(F32)
    mid = r.astype(BF16)
    lo = (r - mid.astype(F32)).astype(BF16)
    return hi, mid, lo


def _hgrn_kernel(layer, hg_ref, lb_ref, ng_ref, o_ref, st_ref):
    C, SB, H, DK = HG_CHUNK, HG_SUB, HG_HEADS, HG_DK
    nsb = C // SB

    @pl.when(pl.program_id(1) == 0)
    def _():
        st_ref[...] = jnp.zeros_like(st_ref)

    lbp = lb_ref[...]
    e = jnp.exp(lbp - jnp.max(lbp, axis=0, keepdims=True))
    lb = jnp.sum(e[:layer + 1], axis=0, keepdims=True) / jnp.sum(e, axis=0, keepdims=True)

    ri = lax.broadcasted_iota(jnp.int32, (C, C), 0)
    ci = lax.broadcasted_iota(jnp.int32, (C, C), 1)
    tril = (ci <= ri).astype(BF16)
    t_sub = lax.broadcasted_iota(jnp.int32, (SB, H * DK), 0)
    lane16 = lax.broadcasted_iota(jnp.int32, (SB, SB), 1)
    ng = ng_ref[...]

    def chunk(c, carry):
        r0 = pl.multiple_of(c * C, C)
        blk = hg_ref[0, pl.ds(r0, C), :]
        q = blk[:, 0:512]
        f = lb + (1.0 - lb) * jax.nn.sigmoid(blk[:, 512:1024])
        lf = jnp.log(f)
        k = 1.0 - f
        v = blk[:, 1024:1536]
        gate = blk[:, 1536:2048]
        hi, mid, lo = _split3(lf)
        b = _nn(tril, hi) + _nn(tril, mid) + _nn(tril, lo)
        b_last = b[C - 1:C, :]
        qe = (q * jnp.exp(b)).astype(BF16)
        ks = (k * jnp.exp(b_last - b)).astype(BF16)
        dec = jnp.exp(b_last)
        vb = v.astype(BF16)
        rblk = jnp.concatenate([jnp.broadcast_to(b[i * SB:i * SB + 1, :], (SB, H * DK)) for i in range(nsb)], axis=0)
        qn = (q * jnp.exp(b - rblk)).astype(BF16)
        outs = []
        for h in range(H):
            sl = slice(h * DK, (h + 1) * DK)
            stT = st_ref[h]
            o_h = _nt(qe[:, sl], stT.astype(BF16))
            rows = []
            for i in range(nsb):
                rs = slice(i * SB, (i + 1) * SB)
                acc = jnp.zeros((SB, HG_DV), F32)
                if i > 0:
                    ke = (k[0:i * SB, sl] * jnp.exp(b[i * SB:i * SB + 1, sl] - b[0:i * SB, sl])).astype(BF16)
                    a_off = _nt(qn[rs, sl], ke)
                    acc = acc + _nn(a_off.astype(BF16), vb[0:i * SB, sl])
                qi, ki, bi = q[rs, sl], k[rs, sl], b[rs, sl]
                a_d = jnp.zeros((SB, SB), F32)
                for s in range(SB):
                    w = qi * ki[s:s + 1, :] * jnp.exp(jnp.minimum(bi - bi[s:s + 1, :], 0.0))
                    col = jnp.sum(w, axis=1, keepdims=True)
                    a_d = a_d + jnp.where((lane16 == s) & (t_sub[:, 0:SB] >= s), col, 0.0)
                acc = acc + _nn(a_d.astype(BF16), vb[rs, sl])
                rows.append(acc)
            o_h = o_h + jnp.concatenate(rows, axis=0)
            st_ref[h] = dec[:, sl] * stT + _tn(vb[:, sl], ks[:, sl])
            o_h = o_h * lax.rsqrt(jnp.mean(o_h * o_h, axis=1, keepdims=True) + LN_EPS) * ng
            outs.append(o_h)
        o = jnp.concatenate(outs, axis=1) * jax.nn.silu(gate)
        o_ref[0, pl.ds(r0, C), :] = o.astype(BF16)
        return carry

    lax.fori_loop(0, hg_ref.shape[1] // C, chunk, 0)


def _hgrn2(hg, hg_lb, norm_g_l, layer):
    B, S, _ = hg.shape
    tc = HG_TC
    return pl.pallas_call(
        functools.partial(_hgrn_kernel, layer),
        grid=(B, S // tc),
        in_specs=[pl.BlockSpec((1, tc, 2048), lambda b, i: (b, i, 0)),
                  pl.BlockSpec(hg_lb.shape, lambda b, i: (0, 0)),
                  pl.BlockSpec((1, HG_DV), lambda b, i: (0, 0))],
        out_specs=pl.BlockSpec((1, tc, HG_WIDTH), lambda b, i: (b, i, 0)),
        out_shape=jax.ShapeDtypeStruct((B, S, HG_WIDTH), BF16),
        scratch_shapes=[pltpu.VMEM((HG_HEADS, HG_DV, HG_DK), F32)],
        compiler_params=_cparams(("parallel", "arbitrary")),
        name="hgrn2",
    )(hg, hg_lb.astype(F32), norm_g_l.reshape(1, HG_DV).astype(F32))


def _compress_kernel(c_ref, pe_ref, w1_ref, b1_ref, w2_ref, w2T_ref, b2_ref, b2T_ref, o_ref, oT_ref):
    half = CMP_STRIDE * NSA_HD
    c = c_ref[0, 0, 0].astype(F32)
    pe = pe_ref[0]
    ca = (c + pe[:, 0:half]).astype(BF16)
    cb = (c + pe[:, half:2 * half]).astype(BF16)
    pa = _nn(ca, w1_ref[0, 0:half, :])
    pb = _nn(cb, w1_ref[0, half:2 * half, :])
    nbp = pa.shape[0]
    h = pa + pltpu.roll(pb, nbp - 1, 0) + b1_ref[0]
    h = jax.nn.gelu(h).astype(BF16)
    o_ref[0, 0, 0] = (_nn(h, w2_ref[0]) + b2_ref[0]).astype(BF16)
    oT_ref[0, 0, 0] = (_nt(w2T_ref[0], h) + b2T_ref[0]).astype(BF16)


def _compress(kn, cmpk, cmpv):
    B, _, G, S, HD = kn.shape
    nbp = S // CMP_STRIDE
    c = kn[:, 0:2].reshape(B, 2, G, nbp, CMP_STRIDE * HD)
    stack = lambda a, b, f: jnp.stack([f(a), f(b)], axis=0)
    pe = stack(cmpk[0], cmpv[0], lambda t: t.reshape(1, CMP_LEN * HD).astype(F32))
    w1 = stack(cmpk[1], cmpv[1], lambda t: t.astype(BF16))
    b1 = stack(cmpk[2], cmpv[2], lambda t: t.reshape(1, CMP_HIDDEN).astype(F32))
    w2 = stack(cmpk[3], cmpv[3], lambda t: t.astype(BF16))
    w2T = stack(cmpk[3], cmpv[3], lambda t: t.astype(BF16).T)
    b2 = stack(cmpk[4], cmpv[4], lambda t: t.reshape(1, HD).astype(F32))
    b2T = stack(cmpk[4], cmpv[4], lambda t: t.reshape(HD, 1).astype(F32))
    wspec = lambda a: pl.BlockSpec((1,) + a.shape[1:], lambda b, t, g: (t,) + (0,) * (a.ndim - 1))
    return pl.pallas_call(
        _compress_kernel,
        grid=(B, 2, G),
        in_specs=[pl.BlockSpec((1, 1, 1, nbp, CMP_STRIDE * HD), lambda b, t, g: (b, t, g, 0, 0)),
                  wspec(pe), wspec(w1), wspec(b1), wspec(w2), wspec(w2T), wspec(b2), wspec(b2T)],
        out_specs=(pl.BlockSpec((1, 1, 1, nbp, HD), lambda b, t, g: (b, t, g, 0, 0)),
                   pl.BlockSpec((1, 1, 1, HD, nbp), lambda b, t, g: (b, t, g, 0, 0))),
        out_shape=(jax.ShapeDtypeStruct((B, 2, G, nbp, HD), BF16),
                   jax.ShapeDtypeStruct((B, 2, G, HD, nbp), BF16)),
        compiler_params=_cparams(("parallel", "parallel", "parallel")),
        name="kv_compress",
    )(c, pe, w1, b1, w2, w2T, b2, b2T)


def _topk_mask_rows(score, k):
    R = score.shape[0]
    r_io = lax.broadcasted_iota(jnp.int32, score.shape, 0)
    s = score
    for _ in range(k):
        m = jnp.max(s, axis=0, keepdims=True)
        idx = jnp.min(jnp.where(s == m, r_io, R), axis=0, keepdims=True)
        s = jnp.where(r_io == idx, -jnp.inf, s)
    return (s == -jnp.inf).astype(F32)


def _with_ones_row(vT):
    pad = (lax.broadcasted_iota(jnp.int32, (16, vT.shape[1]), 0) == 0).astype(BF16)
    return jnp.concatenate([vT, pad], axis=0)


def _nsa_kernel(q_off, qT_ref, kc_ref, vcT_ref, ks_ref, vsT_ref, kw_ref, vwT_ref, gT_ref, ovT_ref,
                o_ref, sel_ref):
    TQ, G, HD, KT, NG = ATT_TQ, NSA_GROUP, NSA_HD, ATT_KT, NSA_KV_HEADS
    qi = pl.program_id(1) + q_off
    t0 = qi * TQ
    pos1 = t0 + lax.broadcasted_iota(jnp.int32, (1, TQ), 1)
    pos4 = jnp.concatenate([pos1] * G, axis=1)
    nbp = kc_ref.shape[3]
    ns = ovT_ref.shape[0]
    n_io = lax.broadcasted_iota(jnp.int32, (nbp, 1), 0)
    mask_c = (n_io * CMP_STRIDE + (CMP_LEN - 1)) <= pos4
    j_io = lax.broadcasted_iota(jnp.int32, (ns, 1), 0)
    cur = pos1 >> int(math.log2(SLC_LEN))
    forced = (j_io == 0) | (j_io == cur) | (j_io == cur - 1)
    causal_blk = j_io * SLC_LEN <= pos1

    q4s, o_cs = [], []
    for g in range(NG):
        qT = qT_ref[0, g * G * HD:(g + 1) * G * HD, :]
        q4 = jnp.concatenate([qT[r * HD:(r + 1) * HD, :] for r in range(G)], axis=1)
        s_c = jnp.where(mask_c, _nn(kc_ref[0, 0, g], q4), NEG)
        p_c = jnp.where(mask_c, jnp.exp2(s_c - jnp.max(s_c, axis=0, keepdims=True)), 0.0)
        p_c = p_c / jnp.maximum(jnp.sum(p_c, axis=0, keepdims=True), 1e-30)
        o_cs.append(_nn(vcT_ref[0, 0, g], p_c.astype(BF16)))
        p_sum = p_c[:, 0:TQ]
        for r in range(1, G):
            p_sum = p_sum + p_c[:, r * TQ:(r + 1) * TQ]
        imp = _nn(ovT_ref[...], p_sum.astype(BF16))
        score = jnp.where(forced, FORCE_SCORE, jnp.where(causal_blk, imp, -1.0))
        sel_ref[g] = (_topk_mask_rows(score, min(SLC_TOPK, ns)) - 1.0) * (-NEG)
        q4s.append(q4)

    KS = ATT_KS
    bps = KS // SLC_LEN
    k_io = lax.broadcasted_iota(jnp.int32, (KS, 1), 0)
    zrows = jnp.zeros((HD - bps, G * TQ), BF16)

    def sel_step(ks_i, carry, diagonal):
        k0 = pl.multiple_of(ks_i * KS, KS)
        out = []
        for g in range(NG):
            m, acc = carry[g]
            bias = sel_ref[g, pl.ds(pl.multiple_of(ks_i * bps, bps), bps), :].astype(BF16)
            rhs = jnp.concatenate([q4s[g], jnp.concatenate([bias] * G, axis=1), zrows], axis=0)
            s = _nn(ks_ref[0, g, pl.ds(k0, KS), :], rhs)
            if diagonal:
                s = jnp.where((k0 + k_io) <= pos4, s, NEG)
            m_new = jnp.maximum(m, jnp.max(s, axis=0, keepdims=True))
            p = jnp.exp2(s - m_new).astype(BF16)
            acc = jnp.exp2(m - m_new) * acc
            for c in range(KS // KT):
                acc = acc + _nn(_with_ones_row(vsT_ref[0, g, ks_i * (KS // KT) + c]), p[c * KT:(c + 1) * KT])
            out.append((m_new, acc))
        return tuple(out)

    n_ks = (t0 + TQ - 1) // KS + 1
    init = tuple((jnp.full((1, G * TQ), NEG, F32), jnp.zeros((HD + 16, G * TQ), F32)) for _ in range(NG))
    carry = lax.fori_loop(0, n_ks - 1, lambda i, c: sel_step(i, c, False), init)
    carry = sel_step(n_ks - 1, carry, True)

    nwt = WIN // TQ + 1
    r_io = lax.broadcasted_iota(jnp.int32, (TQ, 1), 0)
    idxs = [qi - (nwt - 1) + i for i in range(nwt)]
    kpos = jnp.concatenate([idx * TQ + r_io for idx in idxs], axis=0)
    delta = pos4 - kpos
    mask_w = (delta >= 0) & (delta < WIN) & (kpos >= 0)
    outs = []
    for g in range(NG):
        acc_s = carry[g][1]
        o_s = acc_s[0:HD] / jnp.maximum(acc_s[HD:HD + 1], 1e-30)
        kw_t = [kw_ref[0, 0, g, pl.ds(pl.multiple_of(jnp.maximum(idx, 0) * TQ, TQ), TQ), :] for idx in idxs]
        vw_t = [vwT_ref[0, g, jnp.maximum(idx, 0)] for idx in idxs]
        s_w = jnp.where(mask_w, _nn(jnp.concatenate(kw_t, axis=0), q4s[g]), NEG)
        p_w = jnp.exp2(s_w - jnp.max(s_w, axis=0, keepdims=True)).astype(BF16)
        acc_w = _nn(_with_ones_row(jnp.concatenate(vw_t, axis=1)), p_w)
        o_w = acc_w[0:HD] / jnp.maximum(acc_w[HD:HD + 1], 1e-30)
        gs = jax.nn.sigmoid(gT_ref[0, g])
        gate = lambda br: jnp.concatenate([gs[r * 3 + br:r * 3 + br + 1, :] for r in range(G)], axis=1)
        o = gate(0) * o_cs[g] + gate(1) * o_s + gate(2) * o_w
        outs.extend(o[:, r * TQ:(r + 1) * TQ].T for r in range(G))
    o_ref[0] = jnp.concatenate(outs, axis=1).astype(BF16)


def _overlap_T(S):
    nbp = S // CMP_STRIDE
    n_slc = S // SLC_LEN
    cs = np.arange(nbp) * CMP_STRIDE
    ss = np.arange(n_slc) * SLC_LEN
    ov = np.clip(np.minimum(cs[:, None] + CMP_LEN, ss[None, :] + SLC_LEN)
                 - np.maximum(cs[:, None], ss[None, :]), 0, None) / CMP_LEN
    ov[nbp - 1:, :] = 0.0
    return jnp.asarray(ov.T, BF16)


def _nsa(qT, cmp_n, cmp_T, kn, ksE, vsT, vwT, gT, q_lo, q_len):
    B, _, S = qT.shape
    G2, TQ, HD = NSA_KV_HEADS, ATT_TQ, NSA_HD
    nbp = S // CMP_STRIDE
    ns = S // SLC_LEN
    ovT = _overlap_T(S)
    q_off = q_lo // TQ
    return pl.pallas_call(
        functools.partial(_nsa_kernel, q_off),
        grid=(B, q_len // TQ),
        in_specs=[
            pl.BlockSpec((1, NSA_WIDTH, TQ), lambda b, i: (b, 0, i + q_off)),
            pl.BlockSpec((1, 1, G2, nbp, HD), lambda b, i: (b, 0, 0, 0, 0)),
            pl.BlockSpec((1, 1, G2, HD, nbp), lambda b, i: (b, 1, 0, 0, 0)),
            pl.BlockSpec((1, G2, S, 2 * HD), lambda b, i: (b, 0, 0, 0)),
            pl.BlockSpec((1, G2, S // ATT_KT, HD, ATT_KT), lambda b, i: (b, 0, 0, 0, 0)),
            pl.BlockSpec((1, 1, G2, S, HD), lambda b, i: (b, 2, 0, 0, 0)),
            pl.BlockSpec((1, G2, S // LANES, HD, LANES), lambda b, i: (b, 0, 0, 0, 0)),
            pl.BlockSpec((1, G2, 16, TQ), lambda b, i: (b, 0, 0, i + q_off)),
            pl.BlockSpec((ns, nbp), lambda b, i: (0, 0)),
        ],
        out_specs=pl.BlockSpec((1, TQ, NSA_WIDTH), lambda b, i: (b, i, 0)),
        out_shape=jax.ShapeDtypeStruct((B, q_len, NSA_WIDTH), BF16),
        scratch_shapes=[pltpu.VMEM((G2, ns, TQ), F32)],
        compiler_params=_cparams(("parallel", "arbitrary")),
        name="nsa_attn",
    )(qT, cmp_n, cmp_T, ksE, vsT, kn, vwT, gT, ovT)


def _layernorm(t, g, b):
    mu = jnp.mean(t, axis=-1, keepdims=True)
    d = t - mu
    var = jnp.mean(d * d, axis=-1, keepdims=True)
    return d * lax.rsqrt(var + LN_EPS) * g + b


def _outproj_kernel(alpha, ohg_ref, onsa_ref, x_ref, w_ref, g_ref, b_ref, h_ref):
    mix = _nn(ohg_ref[...], w_ref[0:HG_WIDTH, :]) + _nn(onsa_ref[...], w_ref[HG_WIDTH:HG_WIDTH + NSA_WIDTH, :])
    h_ref[...] = _layernorm(alpha * x_ref[...] + mix, g_ref[...], b_ref[...])


def _outproj(alpha, o_hg, o_nsa, x2, w_out_l, g, b):
    n, D = x2.shape
    tm = OUT_TM
    row = lambda w: pl.BlockSpec((tm, w), lambda i: (i, 0))
    const = lambda shp: pl.BlockSpec(shp, lambda i: (0, 0))
    return pl.pallas_call(
        functools.partial(_outproj_kernel, alpha),
        grid=(n // tm,),
        in_specs=[row(HG_WIDTH), row(NSA_WIDTH), row(D), const(w_out_l.shape), const((1, D)), const((1, D))],
        out_specs=row(D),
        out_shape=jax.ShapeDtypeStruct((n, D), F32),
        compiler_params=_cparams(("parallel",)),
        name="out_proj_ln",
    )(o_hg, o_nsa, x2, w_out_l.astype(BF16), g.reshape(1, D).astype(F32), b.reshape(1, D).astype(F32))


def _topk_rows(s, k):
    R = s.shape[0]
    r_io = lax.broadcasted_iota(jnp.int32, s.shape, 0)
    vals, idxs = [], []
    for _ in range(k):
        m = jnp.max(s, axis=0, keepdims=True)
        idx = jnp.min(jnp.where(s == m, r_io, R), axis=0, keepdims=True)
        vals.append(m)
        idxs.append(idx)
        s = jnp.where(r_io == idx, -jnp.inf, s)
    return jnp.concatenate(vals, axis=0), jnp.concatenate(idxs, axis=0)


def _route_kernel(h_ref, wq_ref, k1_ref, k2_ref, e_ref, g_ref):
    K, half = PEER_TOPK, PEER_DQ // 2
    q = _nn(h_ref[...].astype(BF16), wq_ref[...]).astype(BF16)
    for h in range(PEER_HEADS):
        s1 = _nt(k1_ref[...], q[:, h * PEER_DQ:h * PEER_DQ + half])
        s2 = _nt(k2_ref[...], q[:, h * PEER_DQ + half:(h + 1) * PEER_DQ])
        v1, i1 = _topk_rows(s1, K)
        v2, i2 = _topk_rows(s2, K)
        sub = 8
        b_io = lax.broadcasted_iota(jnp.int32, (sub, 1), 0)
        cands = [v1[0:1, :] + v2]
        ids = [i1[0:1, :] * PEER_NKEYS + i2]
        for a in range(1, sub):
            cands.append(jnp.where(b_io < K // (a + 1), v1[a:a + 1, :] + v2[0:sub, :], -jnp.inf))
            ids.append(i1[a:a + 1, :] * PEER_NKEYS + i2[0:sub, :])
        cands.append(v1[sub:K, :] + v2[0:1, :])
        ids.append(i1[sub:K, :] * PEER_NKEYS + i2[0:1, :])
        cand = jnp.concatenate(cands, axis=0)
        cand_id = jnp.concatenate(ids, axis=0)
        r_io = lax.broadcasted_iota(jnp.int32, cand.shape, 0)
        top_s, top_e = [], []
        for _ in range(K):
            m = jnp.max(cand, axis=0, keepdims=True)
            hit = r_io == jnp.min(jnp.where(cand == m, r_io, cand.shape[0]), axis=0, keepdims=True)
            top_s.append(m)
            top_e.append(jnp.sum(jnp.where(hit, cand_id, 0), axis=0, keepdims=True))
            cand = jnp.where(hit, -jnp.inf, cand)
        top_s = jnp.concatenate(top_s, axis=0)
        e_ref[h * K:(h + 1) * K, :] = jnp.concatenate(top_e, axis=0)
        ex = jnp.exp(top_s - top_s[0:1, :])
        g_ref[h * K:(h + 1) * K, :] = ex / jnp.sum(ex, axis=0, keepdims=True)


def _route(h1, wq, k1, k2):
    n, D = h1.shape
    tr = ROUTE_TR
    hk = PEER_HEADS * PEER_TOPK
    const = lambda shp: pl.BlockSpec(shp, lambda i: (0, 0))
    return pl.pallas_call(
        _route_kernel,
        grid=(n // tr,),
        in_specs=[pl.BlockSpec((tr, D), lambda i: (i, 0)), const(wq.shape), const(k1.shape), const(k2.shape)],
        out_specs=(pl.BlockSpec((hk, tr), lambda i: (0, i)), pl.BlockSpec((hk, tr), lambda i: (0, i))),
        out_shape=(jax.ShapeDtypeStruct((hk, n), jnp.int32), jax.ShapeDtypeStruct((hk, n), F32)),
        compiler_params=_cparams(("parallel",)),
        name="peer_route",
    )(h1, wq.astype(BF16), k1.astype(BF16), k2.astype(BF16))


def _pack_rows(t):
    half = t.shape[1] // 2
    tb = t.astype(BF16)
    lo = lax.bitcast_convert_type(tb[:, :half], jnp.uint16).astype(jnp.uint32)
    hi = lax.bitcast_convert_type(tb[:, half:], jnp.uint16).astype(jnp.uint32)
    return lo | (hi << 16)


def _sc_peer(utab, vtab, idx, x):
    n = x.shape[0]
    w = utab.shape[1]
    hk = PEER_HEADS * PEER_TOPK
    L = SC_LANES
    T = n // SC_WORKERS
    uwin = SC_UWIN
    upt = hk // uwin
    vpt = hk // SC_VWIN
    per_u = vpt // upt
    vlead = SC_VSLOT // 2
    assert n % SC_WORKERS == 0 and T % SC_XB == 0 and uwin % L == 0
    assert upt % SC_USLOT == 0 and vpt % SC_VSLOT == 0 and vpt % upt == 0
    vsteps = T * vpt
    mesh = plsc.VectorSubcoreMesh(core_axis_name="c", subcore_axis_name="s")
    cp = dataclasses.replace(pltpu.CompilerParams(), needs_layout_passes=False)

    @pl.kernel(out_type=(jax.ShapeDtypeStruct((n * hk,), F32), jax.ShapeDtypeStruct((n * hk, w), utab.dtype)),
               mesh=mesh, compiler_params=cp,
               scratch_types=[pltpu.VMEM((T * hk,), jnp.int32),
                              pltpu.VMEM((SC_XB, 2 * w), F32),
                              pltpu.VMEM((SC_USLOT, uwin, w), utab.dtype),
                              pltpu.VMEM((SC_VSLOT, SC_VWIN, w), vtab.dtype),
                              pltpu.VMEM((L, L), F32),
                              pltpu.VMEM((SC_XB * hk,), F32),
                              pltpu.SemaphoreType.DMA((SC_USLOT,)),
                              pltpu.SemaphoreType.DMA((SC_VSLOT,)),
                              pltpu.SemaphoreType.DMA((SC_VSLOT,))])
    def peer(u_hbm, v_hbm, i_hbm, x_hbm, h_hbm, vg_hbm, idx_v, x_v, ubuf, vbuf, acc_s, h_v, usem, vsem, wsem):
        wid = lax.axis_index("c") * (SC_WORKERS // 2) + lax.axis_index("s")
        tok0 = wid * T
        row0 = tok0 * hk
        pltpu.sync_copy(i_hbm.at[pl.ds(row0, T * hk)], idx_v)

        def ufetch(t, j):
            slot = j % SC_USLOT
            return pltpu.make_async_copy(u_hbm.at[idx_v.at[pl.ds(t * hk + j * uwin, uwin)]], ubuf.at[slot], usem.at[slot])

        def vfetch(s, slot):
            return pltpu.make_async_copy(v_hbm.at[idx_v.at[pl.ds(s * SC_VWIN, SC_VWIN)]], vbuf.at[slot], vsem.at[slot])

        def vflush(s, slot):
            return pltpu.make_async_copy(vbuf.at[slot], vg_hbm.at[pl.ds(row0 + s * SC_VWIN, SC_VWIN)], wsem.at[slot])

        for j in range(SC_USLOT):
            ufetch(0, j).start()
        for j in range(vlead):
            vfetch(j, j).start()

        def vstep(s, slot):
            other = (slot + vlead) % SC_VSLOT
            vfetch(s, slot).wait()
            vflush(s, slot).start()

            @pl.when(s >= vlead)
            def _():
                vflush(s - vlead, other).wait()

            @pl.when(s + vlead < vsteps)
            def _():
                vfetch(s + vlead, other).start()

        lane = lax.iota(jnp.int32, L)

        @pl.loop(0, T // SC_XB)
        def _(tb):
            pltpu.sync_copy(x_hbm.at[pl.ds(pl.multiple_of(tok0 + tb * SC_XB, SC_XB), SC_XB)], x_v)

            @pl.loop(0, SC_XB)
            def _(tt):
                t = tb * SC_XB + tt
                for j in range(upt):
                    for i in range(per_u):
                        vstep(t * vpt + j * per_u + i, (j * per_u + i) % SC_VSLOT)
                    ufetch(t, j).wait()

                    def dot_body(jj, accs):
                        off = pl.multiple_of(jj * L, L)
                        xlo = x_v[tt, pl.ds(off, L)]
                        xhi = x_v[tt, pl.ds(w + off, L)]
                        out = []
                        for r in range(uwin):
                            wv = ubuf[j % SC_USLOT, r, pl.ds(off, L)]
                            lo = plsc.bitcast(wv << 16, F32)
                            hi = plsc.bitcast(wv & jnp.uint32(0xFFFF0000), F32)
                            out.append(accs[r] + lo * xlo + hi * xhi)
                        return tuple(out)

                    accs = lax.fori_loop(0, w // L, dot_body, tuple(jnp.zeros((L,), F32) for _ in range(uwin)))

                    if j + SC_USLOT < upt:
                        ufetch(t, j + SC_USLOT).start()
                    else:
                        @pl.when(t + 1 < T)
                        def _():
                            ufetch(t + 1, j + SC_USLOT - upt).start()

                    for g in range(uwin // L):
                        for r in range(L):
                            acc_s[r, :] = accs[g * L + r]
                        tot = jnp.zeros((L,), F32)
                        for c in range(L):
                            tot = tot + plsc.load_gather(acc_s, [lane, jnp.full((L,), c, jnp.int32)])
                        h_v[pl.ds(tt * hk + j * uwin + g * L, L)] = tot

            pltpu.sync_copy(h_v, h_hbm.at[pl.ds(pl.multiple_of(row0 + tb * SC_XB * hk, SC_XB * hk), SC_XB * hk)])

        for s in range(vsteps - vlead, vsteps):
            vflush(s, s % SC_VSLOT).wait()

    return peer(utab, vtab, idx, x)


def _unpack(w):
    lo = lax.bitcast_convert_type(w << 16, F32)
    hi = lax.bitcast_convert_type(w & jnp.uint32(0xFFFF0000), F32)
    return lo, hi


def _mix_kernel(alpha, h_ref, s_ref, gate_ref, vg_ref, g_ref, b_ref, o_ref):
    TT, HK = MIX_TT, PEER_HEADS * PEER_TOPK
    x = h_ref[...]
    wrow = gate_ref[...] * jax.nn.gelu(s_ref[...])
    eye = lax.broadcasted_iota(jnp.int32, (HK, HK), 0) == lax.broadcasted_iota(jnp.int32, (HK, HK), 1)
    rows = []
    for t in range(TT):
        wcol = jnp.sum(jnp.where(eye, jnp.broadcast_to(wrow[t:t + 1, :], (HK, HK)), 0.0),
                       axis=1, keepdims=True)
        vlo, vhi = _unpack(vg_ref[t * HK:(t + 1) * HK, :])
        rows.append(jnp.concatenate([jnp.sum(wcol * vlo, axis=0, keepdims=True),
                                     jnp.sum(wcol * vhi, axis=0, keepdims=True)], axis=1))
    ffn = jnp.concatenate(rows, axis=0)
    o_ref[...] = _layernorm(alpha * x + ffn, g_ref[...], b_ref[...])


def _mix(alpha, h1, scores, gates, vg, g, b):
    n, D = h1.shape
    tt = MIX_TT
    hk = PEER_HEADS * PEER_TOPK
    const = lambda shp: pl.BlockSpec(shp, lambda i: (0, 0))
    return pl.pallas_call(
        functools.partial(_mix_kernel, alpha),
        grid=(n // tt,),
        in_specs=[pl.BlockSpec((tt, D), lambda i: (i, 0)), pl.BlockSpec((tt, hk), lambda i: (i, 0)),
                  pl.BlockSpec((tt, hk), lambda i: (i, 0)), pl.BlockSpec((tt * hk, D // 2), lambda i: (i, 0)),
                  const((1, D)), const((1, D))],
        out_specs=pl.BlockSpec((tt, D), lambda i: (i, 0)),
        out_shape=jax.ShapeDtypeStruct((n, D), F32),
        compiler_params=_cparams(("parallel",)),
        name="peer_mix_ln",
    )(h1, scores, gates, vg, g.reshape(1, D).astype(F32), b.reshape(1, D).astype(F32))


PEER_CHUNK = 4096
SEQ_PIECE = 4096


def _peer_ffn_ln(alpha, h1, wq, k1, k2, up, vp, g, b):
    n, D = h1.shape
    hk = PEER_HEADS * PEER_TOPK
    eT, gT = _route(h1, wq, k1, k2)
    experts = eT.T
    gates = gT.T
    ch = min(PEER_CHUNK, n)
    outs = []
    for c in range(n // ch):
        hc = h1[c * ch:(c + 1) * ch]
        scores, vg = _sc_peer(up, vp, experts[c * ch:(c + 1) * ch].reshape(ch * hk), hc)
        outs.append(_mix(alpha, hc, scores.reshape(ch, hk), gates[c * ch:(c + 1) * ch], vg, g, b))
    return jnp.concatenate(outs, axis=0)


def kernel(x, w_in, w_out, hg_lb, hg_norm_g, cmpk_pe, cmpk_w1, cmpk_b1, cmpk_w2, cmpk_b2,
           cmpv_pe, cmpv_w1, cmpv_b1, cmpv_w2, cmpv_b2, ln1_g, ln1_b,
           peer_wq, peer_k1, peer_k2, peer_u, peer_v, ln2_g, ln2_b):
    B, S, D = x.shape
    depth = w_in.shape[0]
    alpha = (2.0 * depth) ** 0.25
    h = x
    for l in range(depth):
        up, vp = _pack_rows(peer_u[l]), _pack_rows(peer_v[l])
        cmpk = (cmpk_pe[l], cmpk_w1[l], cmpk_b1[l], cmpk_w2[l], cmpk_b2[l])
        cmpv = (cmpv_pe[l], cmpv_w1[l], cmpv_b1[l], cmpv_w2[l], cmpv_b2[l])
        outs = []
        for bi in range(B):
            hb = h[bi:bi + 1]
            hg, qT, kn, ksE, vsT, vwT, gT = _project(hb, w_in[l])
            o_hg = _hgrn2(hg, hg_lb, hg_norm_g[l], l)
            cmp_n, cmp_T = _compress(kn, cmpk, cmpv)
            piece = min(SEQ_PIECE, S)
            for lo in range(0, S, piece):
                o_nsa = _nsa(qT, cmp_n, cmp_T, kn, ksE, vsT, vwT, gT, lo, piece)
                h1 = _outproj(alpha, o_hg[0, lo:lo + piece], o_nsa.reshape(piece, NSA_WIDTH),
                              hb[0, lo:lo + piece], w_out[l], ln1_g[l], ln1_b[l])
                outs.append(_peer_ffn_ln(alpha, h1, peer_wq[l], peer_k1[l], peer_k2[l], up, vp, ln2_g[l], ln2_b[l]))
        h = jnp.concatenate(outs, axis=0).reshape(B, S, D)
    return h
```

```python
import dataclasses
import functools
import math

import jax
import jax.numpy as jnp
import numpy as np
from jax import lax
from jax.experimental import pallas as pl
from jax.experimental.pallas import tpu as pltpu
from jax.experimental.pallas import tpu_sc as plsc

F32 = jnp.float32
BF16 = jnp.bfloat16

HG_HEADS = 4
HG_DK = 128
HG_DV = 128
HG_CHUNK = 64
HG_SUB = 16
HG_WIDTH = HG_HEADS * HG_DV
NSA_HEADS = 8
NSA_KV_HEADS = 2
NSA_GROUP = NSA_HEADS // NSA_KV_HEADS
NSA_HD = 64
NSA_WIDTH = NSA_HEADS * NSA_HD
CMP_LEN = 32
CMP_STRIDE = 16
CMP_HIDDEN = 256
SLC_LEN = 64
SLC_TOPK = 16
WIN = 512
FORCE_SCORE = 1.0e4
PEER_HEADS = 8
PEER_DQ = 256
PEER_NKEYS = 128
PEER_TOPK = 16
LN_EPS = 1e-5
NEG = -1e30
LOG2E = math.log2(math.e)

LANES = 128
PROJ_TM = 512
HG_TC = 256
ATT_TQ = 128
ATT_KT = 512
ATT_KS = 1024
OUT_TM = 512
ROUTE_TR = 256
MIX_TT = 16
SC_WORKERS = 32
SC_LANES = 16
SC_UWIN = 16
SC_USLOT = 4
SC_VWIN = 16
SC_VSLOT = 8
SC_XB = 8
VMEM_LIMIT = 56 * 1024 * 1024


def _cparams(sem):
    return pltpu.CompilerParams(dimension_semantics=sem, vmem_limit_bytes=VMEM_LIMIT)


def _nt(a, b):
    return lax.dot_general(a, b, (((1,), (1,)), ((), ())), preferred_element_type=F32)


def _tn(a, b):
    return lax.dot_general(a, b, (((0,), (0,)), ((), ())), preferred_element_type=F32)


def _nn(a, b):
    return jnp.dot(a, b, preferred_element_type=F32)


def _proj_kernel(x_ref, whg_ref, wqT_ref, wkn_ref, wvT_ref, wgT_ref,
                 hg_ref, qT_ref, kn_ref, ksE_ref, vsT_ref, vwT_ref, gT_ref):
    xb = x_ref[0].astype(BF16)
    hg_ref[0] = _nn(xb, whg_ref[...])
    qT_ref[0] = (_nt(wqT_ref[...], xb) * (NSA_HD ** -0.5 * LOG2E)).astype(BF16)
    kn = _nn(xb, wkn_ref[...])
    for j in range(3):
        for g in range(NSA_KV_HEADS):
            lo = j * 2 * NSA_HD + g * NSA_HD
            kn_ref[0, j, g] = kn[:, lo:lo + NSA_HD].astype(BF16)
    row = pl.program_id(1) * kn.shape[0] + lax.broadcasted_iota(jnp.int32, (kn.shape[0], NSA_HD), 0)
    blk = (row >> int(math.log2(SLC_LEN))) & (ATT_KS // SLC_LEN - 1)
    onehot = (lax.broadcasted_iota(jnp.int32, (kn.shape[0], NSA_HD), 1) == blk).astype(F32)
    for g in range(NSA_KV_HEADS):
        lo = 3 * 2 * NSA_HD + g * NSA_HD
        ksE_ref[0, g] = jnp.concatenate([kn[:, lo:lo + NSA_HD], onehot], axis=1).astype(BF16)
    vT = _nt(wvT_ref[...], xb).astype(BF16)
    tm = vT.shape[1]
    for g in range(NSA_KV_HEADS):
        for c in range(tm // ATT_KT):
            vsT_ref[0, g, c] = vT[g * NSA_HD:(g + 1) * NSA_HD, c * ATT_KT:(c + 1) * ATT_KT]
        for c in range(tm // LANES):
            vwT_ref[0, g, c] = vT[(2 + g) * NSA_HD:(3 + g) * NSA_HD, c * LANES:(c + 1) * LANES]
    gT = _nt(wgT_ref[...], xb)
    gT_ref[0, 0] = gT[0:16]
    gT_ref[0, 1] = gT[16:32]


def _project(x, w_in_l):
    B, S, D = x.shape
    tm = PROJ_TM
    o = np.cumsum((0, 512, 512, 512, 512, 512, 128, 128, 128, 128, 128, 128, 24))
    wb = w_in_l.astype(BF16)
    whg = wb[:, o[0]:o[4]]
    wqT = wb[:, o[4]:o[5]].T
    wkn = jnp.concatenate([wb[:, o[5]:o[6]], wb[:, o[6]:o[7]], wb[:, o[9]:o[10]], wb[:, o[7]:o[8]]], axis=1)
    wvT = jnp.concatenate([wb[:, o[8]:o[9]], wb[:, o[10]:o[11]]], axis=1).T
    wg = wb[:, o[11]:o[12]].reshape(D, NSA_KV_HEADS, NSA_GROUP * 3)
    wgT = jnp.pad(wg, ((0, 0), (0, 0), (0, 16 - NSA_GROUP * 3))).reshape(D, 32).T
    const = lambda shp: pl.BlockSpec(shp, lambda b, i: (0,) * len(shp))
    out_shape = (
        jax.ShapeDtypeStruct((B, S, 4 * 512), F32),
        jax.ShapeDtypeStruct((B, NSA_WIDTH, S), BF16),
        jax.ShapeDtypeStruct((B, 3, NSA_KV_HEADS, S, NSA_HD), BF16),
        jax.ShapeDtypeStruct((B, NSA_KV_HEADS, S, 2 * NSA_HD), BF16),
        jax.ShapeDtypeStruct((B, NSA_KV_HEADS, S // ATT_KT, NSA_HD, ATT_KT), BF16),
        jax.ShapeDtypeStruct((B, NSA_KV_HEADS, S // LANES, NSA_HD, LANES), BF16),
        jax.ShapeDtypeStruct((B, NSA_KV_HEADS, 16, S), F32),
    )
    out_specs = (
        pl.BlockSpec((1, tm, 2048), lambda b, i: (b, i, 0)),
        pl.BlockSpec((1, NSA_WIDTH, tm), lambda b, i: (b, 0, i)),
        pl.BlockSpec((1, 3, NSA_KV_HEADS, tm, NSA_HD), lambda b, i: (b, 0, 0, i, 0)),
        pl.BlockSpec((1, NSA_KV_HEADS, tm, 2 * NSA_HD), lambda b, i: (b, 0, i, 0)),
        pl.BlockSpec((1, NSA_KV_HEADS, tm // ATT_KT, NSA_HD, ATT_KT), lambda b, i: (b, 0, i, 0, 0)),
        pl.BlockSpec((1, NSA_KV_HEADS, tm // LANES, NSA_HD, LANES), lambda b, i: (b, 0, i, 0, 0)),
        pl.BlockSpec((1, NSA_KV_HEADS, 16, tm), lambda b, i: (b, 0, 0, i)),
    )
    return pl.pallas_call(
        _proj_kernel,
        grid=(B, S // tm),
        in_specs=[pl.BlockSpec((1, tm, D), lambda b, i: (b, i, 0)),
                  const(whg.shape), const(wqT.shape), const(wkn.shape), const(wvT.shape), const(wgT.shape)],
        out_specs=out_specs,
        out_shape=out_shape,
        compiler_params=_cparams(("parallel", "arbitrary")),
        name="in_proj",
    )(x, whg, wqT, wkn, wvT, wgT)


def _split3(a):
    hi = a.astype(BF16)
    r = a - hi.astype(F32)
    mid = r.astype(BF16)
    lo = (r - mid.astype(F32)).astype(BF16)
    return hi, mid, lo


def _hgrn_kernel(layer, hg_ref, lb_ref, ng_ref, o_ref, st_ref):
    C, SB, H, DK = HG_CHUNK, HG_SUB, HG_HEADS, HG_DK
    nsb = C // SB

    @pl.when(pl.program_id(1) == 0)
    def _():
        st_ref[...] = jnp.zeros_like(st_ref)

    lbp = lb_ref[...]
    e = jnp.exp(lbp - jnp.max(lbp, axis=0, keepdims=True))
    lb = jnp.sum(e[:layer + 1], axis=0, keepdims=True) / jnp.sum(e, axis=0, keepdims=True)

    ri = lax.broadcasted_iota(jnp.int32, (C, C), 0)
    ci = lax.broadcasted_iota(jnp.int32, (C, C), 1)
    tril = (ci <= ri).astype(BF16)
    t_sub = lax.broadcasted_iota(jnp.int32, (SB, H * DK), 0)
    lane16 = lax.broadcasted_iota(jnp.int32, (SB, SB), 1)
    ng = ng_ref[...]

    def chunk(c, carry):
        r0 = pl.multiple_of(c * C, C)
        blk = hg_ref[0, pl.ds(r0, C), :]
        q = blk[:, 0:512]
        f = lb + (1.0 - lb) * jax.nn.sigmoid(blk[:, 512:1024])
        lf = jnp.log(f)
        k = 1.0 - f
        v = blk[:, 1024:1536]
        gate = blk[:, 1536:2048]
        hi, mid, lo = _split3(lf)
        b = _nn(tril, hi) + _nn(tril, mid) + _nn(tril, lo)
        b_last = b[C - 1:C, :]
        qe = (q * jnp.exp(b)).astype(BF16)
        ks = (k * jnp.exp(b_last - b)).astype(BF16)
        dec = jnp.exp(b_last)
        vb = v.astype(BF16)
        rblk = jnp.concatenate([jnp.broadcast_to(b[i * SB:i * SB + 1, :], (SB, H * DK)) for i in range(nsb)], axis=0)
        qn = (q * jnp.exp(b - rblk)).astype(BF16)
        outs = []
        for h in range(H):
            sl = slice(h * DK, (h + 1) * DK)
            stT = st_ref[h]
            o_h = _nt(qe[:, sl], stT.astype(BF16))
            rows = []
            for i in range(nsb):
                rs = slice(i * SB, (i + 1) * SB)
                acc = jnp.zeros((SB, HG_DV), F32)
                if i > 0:
                    ke = (k[0:i * SB, sl] * jnp.exp(b[i * SB:i * SB + 1, sl] - b[0:i * SB, sl])).astype(BF16)
                    a_off = _nt(qn[rs, sl], ke)
                    acc = acc + _nn(a_off.astype(BF16), vb[0:i * SB, sl])
                qi, ki, bi = q[rs, sl], k[rs, sl], b[rs, sl]
                a_d = jnp.zeros((SB, SB), F32)
                for s in range(SB):
                    w = qi * ki[s:s + 1, :] * jnp.exp(jnp.minimum(bi - bi[s:s + 1, :], 0.0))
                    col = jnp.sum(w, axis=1, keepdims=True)
                    a_d = a_d + jnp.where((lane16 == s) & (t_sub[:, 0:SB] >= s), col, 0.0)
                acc = acc + _nn(a_d.astype(BF16), vb[rs, sl])
                rows.append(acc)
            o_h = o_h + jnp.concatenate(rows, axis=0)
            st_ref[h] = dec[:, sl] * stT + _tn(vb[:, sl], ks[:, sl])
            o_h = o_h * lax.rsqrt(jnp.mean(o_h * o_h, axis=1, keepdims=True) + LN_EPS) * ng
            outs.append(o_h)
        o = jnp.concatenate(outs, axis=1) * jax.nn.silu(gate)
        o_ref[0, pl.ds(r0, C), :] = o.astype(BF16)
        return carry

    lax.fori_loop(0, hg_ref.shape[1] // C, chunk, 0)


def _hgrn2(hg, hg_lb, norm_g_l, layer):
    B, S, _ = hg.shape
    tc = HG_TC
    return pl.pallas_call(
        functools.partial(_hgrn_kernel, layer),
        grid=(B, S // tc),
        in_specs=[pl.BlockSpec((1, tc, 2048), lambda b, i: (b, i, 0)),
                  pl.BlockSpec(hg_lb.shape, lambda b, i: (0, 0)),
                  pl.BlockSpec((1, HG_DV), lambda b, i: (0, 0))],
        out_specs=pl.BlockSpec((1, tc, HG_WIDTH), lambda b, i: (b, i, 0)),
        out_shape=jax.ShapeDtypeStruct((B, S, HG_WIDTH), BF16),
        scratch_shapes=[pltpu.VMEM((HG_HEADS, HG_DV, HG_DK), F32)],
        compiler_params=_cparams(("parallel", "arbitrary")),
        name="hgrn2",
    )(hg, hg_lb.astype(F32), norm_g_l.reshape(1, HG_DV).astype(F32))


def _compress_kernel(c_ref, pe_ref, w1_ref, b1_ref, w2_ref, w2T_ref, b2_ref, b2T_ref, o_ref, oT_ref):
    half = CMP_STRIDE * NSA_HD
    c = c_ref[0, 0, 0].astype(F32)
    pe = pe_ref[0]
    ca = (c + pe[:, 0:half]).astype(BF16)
    cb = (c + pe[:, half:2 * half]).astype(BF16)
    pa = _nn(ca, w1_ref[0, 0:half, :])
    pb = _nn(cb, w1_ref[0, half:2 * half, :])
    nbp = pa.shape[0]
    h = pa + pltpu.roll(pb, nbp - 1, 0) + b1_ref[0]
    h = jax.nn.gelu(h).astype(BF16)
    o_ref[0, 0, 0] = (_nn(h, w2_ref[0]) + b2_ref[0]).astype(BF16)
    oT_ref[0, 0, 0] = (_nt(w2T_ref[0], h) + b2T_ref[0]).astype(BF16)


def _compress(kn, cmpk, cmpv):
    B, _, G, S, HD = kn.shape
    nbp = S // CMP_STRIDE
    c = kn[:, 0:2].reshape(B, 2, G, nbp, CMP_STRIDE * HD)
    stack = lambda a, b, f: jnp.stack([f(a), f(b)], axis=0)
    pe = stack(cmpk[0], cmpv[0], lambda t: t.reshape(1, CMP_LEN * HD).astype(F32))
    w1 = stack(cmpk[1], cmpv[1], lambda t: t.astype(BF16))
    b1 = stack(cmpk[2], cmpv[2], lambda t: t.reshape(1, CMP_HIDDEN).astype(F32))
    w2 = stack(cmpk[3], cmpv[3], lambda t: t.astype(BF16))
    w2T = stack(cmpk[3], cmpv[3], lambda t: t.astype(BF16).T)
    b2 = stack(cmpk[4], cmpv[4], lambda t: t.reshape(1, HD).astype(F32))
    b2T = stack(cmpk[4], cmpv[4], lambda t: t.reshape(HD, 1).astype(F32))
    wspec = lambda a: pl.BlockSpec((1,) + a.shape[1:], lambda b, t, g: (t,) + (0,) * (a.ndim - 1))
    return pl.pallas_call(
        _compress_kernel,
        grid=(B, 2, G),
        in_specs=[pl.BlockSpec((1, 1, 1, nbp, CMP_STRIDE * HD), lambda b, t, g: (b, t, g, 0, 0)),
                  wspec(pe), wspec(w1), wspec(b1), wspec(w2), wspec(w2T), wspec(b2), wspec(b2T)],
        out_specs=(pl.BlockSpec((1, 1, 1, nbp, HD), lambda b, t, g: (b, t, g, 0, 0)),
                   pl.BlockSpec((1, 1, 1, HD, nbp), lambda b, t, g: (b, t, g, 0, 0))),
        out_shape=(jax.ShapeDtypeStruct((B, 2, G, nbp, HD), BF16),
                   jax.ShapeDtypeStruct((B, 2, G, HD, nbp), BF16)),
        compiler_params=_cparams(("parallel", "parallel", "parallel")),
        name="kv_compress",
    )(c, pe, w1, b1, w2, w2T, b2, b2T)


def _topk_mask_rows(score, k):
    R = score.shape[0]
    r_io = lax.broadcasted_iota(jnp.int32, score.shape, 0)
    s = score
    for _ in range(k):
        m = jnp.max(s, axis=0, keepdims=True)
        idx = jnp.min(jnp.where(s == m, r_io, R), axis=0, keepdims=True)
        s = jnp.where(r_io == idx, -jnp.inf, s)
    return (s == -jnp.inf).astype(F32)


def _with_ones_row(vT):
    pad = (lax.broadcasted_iota(jnp.int32, (16, vT.shape[1]), 0) == 0).astype(BF16)
    return jnp.concatenate([vT, pad], axis=0)


def _nsa_kernel(q_off, qT_ref, kc_ref, vcT_ref, ks_ref, vsT_ref, kw_ref, vwT_ref, gT_ref, ovT_ref,
                o_ref, sel_ref):
    TQ, G, HD, KT, NG = ATT_TQ, NSA_GROUP, NSA_HD, ATT_KT, NSA_KV_HEADS
    qi = pl.program_id(1) + q_off
    t0 = qi * TQ
    pos1 = t0 + lax.broadcasted_iota(jnp.int32, (1, TQ), 1)
    pos4 = jnp.concatenate([pos1] * G, axis=1)
    nbp = kc_ref.shape[3]
    ns = ovT_ref.shape[0]
    n_io = lax.broadcasted_iota(jnp.int32, (nbp, 1), 0)
    mask_c = (n_io * CMP_STRIDE + (CMP_LEN - 1)) <= pos4
    j_io = lax.broadcasted_iota(jnp.int32, (ns, 1), 0)
    cur = pos1 >> int(math.log2(SLC_LEN))
    forced = (j_io == 0) | (j_io == cur) | (j_io == cur - 1)
    causal_blk = j_io * SLC_LEN <= pos1

    q4s, o_cs = [], []
    for g in range(NG):
        qT = qT_ref[0, g * G * HD:(g + 1) * G * HD, :]
        q4 = jnp.concatenate([qT[r * HD:(r + 1) * HD, :] for r in range(G)], axis=1)
        s_c = jnp.where(mask_c, _nn(kc_ref[0, 0, g], q4), NEG)
        p_c = jnp.where(mask_c, jnp.exp2(s_c - jnp.max(s_c, axis=0, keepdims=True)), 0.0)
        p_c = p_c / jnp.maximum(jnp.sum(p_c, axis=0, keepdims=True), 1e-30)
        o_cs.append(_nn(vcT_ref[0, 0, g], p_c.astype(BF16)))
        p_sum = p_c[:, 0:TQ]
        for r in range(1, G):
            p_sum = p_sum + p_c[:, r * TQ:(r + 1) * TQ]
        imp = _nn(ovT_ref[...], p_sum.astype(BF16))
        score = jnp.where(forced, FORCE_SCORE, jnp.where(causal_blk, imp, -1.0))
        sel_ref[g] = (_topk_mask_rows(score, min(SLC_TOPK, ns)) - 1.0) * (-NEG)
        q4s.append(q4)

    KS = ATT_KS
    bps = KS // SLC_LEN
    k_io = lax.broadcasted_iota(jnp.int32, (KS, 1), 0)
    zrows = jnp.zeros((HD - bps, G * TQ), BF16)

    def sel_step(ks_i, carry, diagonal):
        k0 = pl.multiple_of(ks_i * KS, KS)
        out = []
        for g in range(NG):
            m, acc = carry[g]
            bias = sel_ref[g, pl.ds(pl.multiple_of(ks_i * bps, bps), bps), :].astype(BF16)
            rhs = jnp.concatenate([q4s[g], jnp.concatenate([bias] * G, axis=1), zrows], axis=0)
            s = _nn(ks_ref[0, g, pl.ds(k0, KS), :], rhs)
            if diagonal:
                s = jnp.where((k0 + k_io) <= pos4, s, NEG)
            m_new = jnp.maximum(m, jnp.max(s, axis=0, keepdims=True))
            p = jnp.exp2(s - m_new).astype(BF16)
            acc = jnp.exp2(m - m_new) * acc
            for c in range(KS // KT):
                acc = acc + _nn(_with_ones_row(vsT_ref[0, g, ks_i * (KS // KT) + c]), p[c * KT:(c + 1) * KT])
            out.append((m_new, acc))
        return tuple(out)

    n_ks = (t0 + TQ - 1) // KS + 1
    init = tuple((jnp.full((1, G * TQ), NEG, F32), jnp.zeros((HD + 16, G * TQ), F32)) for _ in range(NG))
    carry = lax.fori_loop(0, n_ks - 1, lambda i, c: sel_step(i, c, False), init)
    carry = sel_step(n_ks - 1, carry, True)

    nwt = WIN // TQ + 1
    r_io = lax.broadcasted_iota(jnp.int32, (TQ, 1), 0)
    idxs = [qi - (nwt - 1) + i for i in range(nwt)]
    kpos = jnp.concatenate([idx * TQ + r_io for idx in idxs], axis=0)
    delta = pos4 - kpos
    mask_w = (delta >= 0) & (delta < WIN) & (kpos >= 0)
    outs = []
    for g in range(NG):
        acc_s = carry[g][1]
        o_s = acc_s[0:HD] / jnp.maximum(acc_s[HD:HD + 1], 1e-30)
        kw_t = [kw_ref[0, 0, g, pl.ds(pl.multiple_of(jnp.maximum(idx, 0) * TQ, TQ), TQ), :] for idx in idxs]
        vw_t = [vwT_ref[0, g, jnp.maximum(idx, 0)] for idx in idxs]
        s_w = jnp.where(mask_w, _nn(jnp.concatenate(kw_t, axis=0), q4s[g]), NEG)
        p_w = jnp.exp2(s_w - jnp.max(s_w, axis=0, keepdims=True)).astype(BF16)
        acc_w = _nn(_with_ones_row(jnp.concatenate(vw_t, axis=1)), p_w)
        o_w = acc_w[0:HD] / jnp.maximum(acc_w[HD:HD + 1], 1e-30)
        gs = jax.nn.sigmoid(gT_ref[0, g])
        gate = lambda br: jnp.concatenate([gs[r * 3 + br:r * 3 + br + 1, :] for r in range(G)], axis=1)
        o = gate(0) * o_cs[g] + gate(1) * o_s + gate(2) * o_w
        outs.extend(o[:, r * TQ:(r + 1) * TQ].T for r in range(G))
    o_ref[0] = jnp.concatenate(outs, axis=1).astype(BF16)


def _overlap_T(S):
    nbp = S // CMP_STRIDE
    n_slc = S // SLC_LEN
    cs = np.arange(nbp) * CMP_STRIDE
    ss = np.arange(n_slc) * SLC_LEN
    ov = np.clip(np.minimum(cs[:, None] + CMP_LEN, ss[None, :] + SLC_LEN)
                 - np.maximum(cs[:, None], ss[None, :]), 0, None) / CMP_LEN
    ov[nbp - 1:, :] = 0.0
    return jnp.asarray(ov.T, BF16)


def _nsa(qT, cmp_n, cmp_T, kn, ksE, vsT, vwT, gT, q_lo, q_len):
    B, _, S = qT.shape
    G2, TQ, HD = NSA_KV_HEADS, ATT_TQ, NSA_HD
    nbp = S // CMP_STRIDE
    ns = S // SLC_LEN
    ovT = _overlap_T(S)
    q_off = q_lo // TQ
    return pl.pallas_call(
        functools.partial(_nsa_kernel, q_off),
        grid=(B, q_len // TQ),
        in_specs=[
            pl.BlockSpec((1, NSA_WIDTH, TQ), lambda b, i: (b, 0, i + q_off)),
            pl.BlockSpec((1, 1, G2, nbp, HD), lambda b, i: (b, 0, 0, 0, 0)),
            pl.BlockSpec((1, 1, G2, HD, nbp), lambda b, i: (b, 1, 0, 0, 0)),
            pl.BlockSpec((1, G2, S, 2 * HD), lambda b, i: (b, 0, 0, 0)),
            pl.BlockSpec((1, G2, S // ATT_KT, HD, ATT_KT), lambda b, i: (b, 0, 0, 0, 0)),
            pl.BlockSpec((1, 1, G2, S, HD), lambda b, i: (b, 2, 0, 0, 0)),
            pl.BlockSpec((1, G2, S // LANES, HD, LANES), lambda b, i: (b, 0, 0, 0, 0)),
            pl.BlockSpec((1, G2, 16, TQ), lambda b, i: (b, 0, 0, i + q_off)),
            pl.BlockSpec((ns, nbp), lambda b, i: (0, 0)),
        ],
        out_specs=pl.BlockSpec((1, TQ, NSA_WIDTH), lambda b, i: (b, i, 0)),
        out_shape=jax.ShapeDtypeStruct((B, q_len, NSA_WIDTH), BF16),
        scratch_shapes=[pltpu.VMEM((G2, ns, TQ), F32)],
        compiler_params=_cparams(("parallel", "arbitrary")),
        name="nsa_attn",
    )(qT, cmp_n, cmp_T, ksE, vsT, kn, vwT, gT, ovT)


def _layernorm(t, g, b):
    mu = jnp.mean(t, axis=-1, keepdims=True)
    d = t - mu
    var = jnp.mean(d * d, axis=-1, keepdims=True)
    return d * lax.rsqrt(var + LN_EPS) * g + b


def _outproj_kernel(alpha, ohg_ref, onsa_ref, x_ref, w_ref, g_ref, b_ref, h_ref):
    mix = _nn(ohg_ref[...], w_ref[0:HG_WIDTH, :]) + _nn(onsa_ref[...], w_ref[HG_WIDTH:HG_WIDTH + NSA_WIDTH, :])
    h_ref[...] = _layernorm(alpha * x_ref[...] + mix, g_ref[...], b_ref[...])


def _outproj(alpha, o_hg, o_nsa, x2, w_out_l, g, b):
    n, D = x2.shape
    tm = OUT_TM
    row = lambda w: pl.BlockSpec((tm, w), lambda i: (i, 0))
    const = lambda shp: pl.BlockSpec(shp, lambda i: (0, 0))
    return pl.pallas_call(
        functools.partial(_outproj_kernel, alpha),
        grid=(n // tm,),
        in_specs=[row(HG_WIDTH), row(NSA_WIDTH), row(D), const(w_out_l.shape), const((1, D)), const((1, D))],
        out_specs=row(D),
        out_shape=jax.ShapeDtypeStruct((n, D), F32),
        compiler_params=_cparams(("parallel",)),
        name="out_proj_ln",
    )(o_hg, o_nsa, x2, w_out_l.astype(BF16), g.reshape(1, D).astype(F32), b.reshape(1, D).astype(F32))


def _topk_rows(s, k):
    R = s.shape[0]
    r_io = lax.broadcasted_iota(jnp.int32, s.shape, 0)
    vals, idxs = [], []
    for _ in range(k):
        m = jnp.max(s, axis=0, keepdims=True)
        idx = jnp.min(jnp.where(s == m, r_io, R), axis=0, keepdims=True)
        vals.append(m)
        idxs.append(idx)
        s = jnp.where(r_io == idx, -jnp.inf, s)
    return jnp.concatenate(vals, axis=0), jnp.concatenate(idxs, axis=0)


def _route_kernel(h_ref, wq_ref, k1_ref, k2_ref, e_ref, g_ref):
    K, half = PEER_TOPK, PEER_DQ // 2
    q = _nn(h_ref[...].astype(BF16), wq_ref[...]).astype(BF16)
    for h in range(PEER_HEADS):
        s1 = _nt(k1_ref[...], q[:, h * PEER_DQ:h * PEER_DQ + half])
        s2 = _nt(k2_ref[...], q[:, h * PEER_DQ + half:(h + 1) * PEER_DQ])
        v1, i1 = _topk_rows(s1, K)
        v2, i2 = _topk_rows(s2, K)
        sub = 8
        b_io = lax.broadcasted_iota(jnp.int32, (sub, 1), 0)
        cands = [v1[0:1, :] + v2]
        ids = [i1[0:1, :] * PEER_NKEYS + i2]
        for a in range(1, sub):
            cands.append(jnp.where(b_io < K // (a + 1), v1[a:a + 1, :] + v2[0:sub, :], -jnp.inf))
            ids.append(i1[a:a + 1, :] * PEER_NKEYS + i2[0:sub, :])
        cands.append(v1[sub:K, :] + v2[0:1, :])
        ids.append(i1[sub:K, :] * PEER_NKEYS + i2[0:1, :])
        cand = jnp.concatenate(cands, axis=0)
        cand_id = jnp.concatenate(ids, axis=0)
        r_io = lax.broadcasted_iota(jnp.int32, cand.shape, 0)
        top_s, top_e = [], []
        for _ in range(K):
            m = jnp.max(cand, axis=0, keepdims=True)
            hit = r_io == jnp.min(jnp.where(cand == m, r_io, cand.shape[0]), axis=0, keepdims=True)
            top_s.append(m)
            top_e.append(jnp.sum(jnp.where(hit, cand_id, 0), axis=0, keepdims=True))
            cand = jnp.where(hit, -jnp.inf, cand)
        top_s = jnp.concatenate(top_s, axis=0)
        e_ref[h * K:(h + 1) * K, :] = jnp.concatenate(top_e, axis=0)
        ex = jnp.exp(top_s - top_s[0:1, :])
        g_ref[h * K:(h + 1) * K, :] = ex / jnp.sum(ex, axis=0, keepdims=True)


def _outproj_route_kernel(alpha, ohg_ref, onsa_ref, x_ref, w_ref, lg_ref, lb_ref, wq_ref, k1_ref, k2_ref,
                          h_ref, e_ref, g_ref):
    _outproj_kernel(alpha, ohg_ref, onsa_ref, x_ref, w_ref, lg_ref, lb_ref, h_ref)
    _route_kernel(h_ref, wq_ref, k1_ref, k2_ref, e_ref, g_ref)


def _outproj_route(alpha, o_hg, o_nsa, x2, w_out_l, g, b, wq, k1, k2):
    n, D = x2.shape
    tr = ROUTE_TR
    hk = PEER_HEADS * PEER_TOPK
    row = lambda w: pl.BlockSpec((tr, w), lambda i: (i, 0))
    const = lambda shp: pl.BlockSpec(shp, lambda i: (0, 0))
    return pl.pallas_call(
        functools.partial(_outproj_route_kernel, alpha),
        grid=(n // tr,),
        in_specs=[row(HG_WIDTH), row(NSA_WIDTH), row(D), const(w_out_l.shape), const((1, D)), const((1, D)),
                  const(wq.shape), const(k1.shape), const(k2.shape)],
        out_specs=(row(D), pl.BlockSpec((hk, tr), lambda i: (0, i)), pl.BlockSpec((hk, tr), lambda i: (0, i))),
        out_shape=(jax.ShapeDtypeStruct((n, D), F32), jax.ShapeDtypeStruct((hk, n), jnp.int32),
                   jax.ShapeDtypeStruct((hk, n), F32)),
        compiler_params=_cparams(("parallel",)),
        name="out_proj_route",
    )(o_hg, o_nsa, x2, w_out_l.astype(BF16), g.reshape(1, D).astype(F32), b.reshape(1, D).astype(F32),
      wq.astype(BF16), k1.astype(BF16), k2.astype(BF16))


def _route(h1, wq, k1, k2):
    n, D = h1.shape
    tr = ROUTE_TR
    hk = PEER_HEADS * PEER_TOPK
    const = lambda shp: pl.BlockSpec(shp, lambda i: (0, 0))
    return pl.pallas_call(
        _route_kernel,
        grid=(n // tr,),
        in_specs=[pl.BlockSpec((tr, D), lambda i: (i, 0)), const(wq.shape), const(k1.shape), const(k2.shape)],
        out_specs=(pl.BlockSpec((hk, tr), lambda i: (0, i)), pl.BlockSpec((hk, tr), lambda i: (0, i))),
        out_shape=(jax.ShapeDtypeStruct((hk, n), jnp.int32), jax.ShapeDtypeStruct((hk, n), F32)),
        compiler_params=_cparams(("parallel",)),
        name="peer_route",
    )(h1, wq.astype(BF16), k1.astype(BF16), k2.astype(BF16))


def _pack_rows(t):
    half = t.shape[1] // 2
    tb = t.astype(BF16)
    lo = lax.bitcast_convert_type(tb[:, :half], jnp.uint16).astype(jnp.uint32)
    hi = lax.bitcast_convert_type(tb[:, half:], jnp.uint16).astype(jnp.uint32)
    return lo | (hi << 16)


def _sc_peer(utab, vtab, idx, x):
    n = x.shape[0]
    w = utab.shape[1]
    hk = PEER_HEADS * PEER_TOPK
    L = SC_LANES
    T = n // SC_WORKERS
    uwin = SC_UWIN
    upt = hk // uwin
    vpt = hk // SC_VWIN
    per_u = vpt // upt
    vlead = SC_VSLOT // 2
    assert n % SC_WORKERS == 0 and T % SC_XB == 0 and uwin % L == 0
    assert upt % SC_USLOT == 0 and vpt % SC_VSLOT == 0 and vpt % upt == 0
    vsteps = T * vpt
    mesh = plsc.VectorSubcoreMesh(core_axis_name="c", subcore_axis_name="s")
    cp = dataclasses.replace(pltpu.CompilerParams(), needs_layout_passes=False)

    @pl.kernel(out_type=(jax.ShapeDtypeStruct((n * hk,), F32), jax.ShapeDtypeStruct((n * hk, w), utab.dtype)),
               mesh=mesh, compiler_params=cp,
               scratch_types=[pltpu.VMEM((T * hk,), jnp.int32),
                              pltpu.VMEM((SC_XB, 2 * w), F32),
                              pltpu.VMEM((SC_USLOT, uwin, w), utab.dtype),
                              pltpu.VMEM((SC_VSLOT, SC_VWIN, w), vtab.dtype),
                              pltpu.VMEM((L, L), F32),
                              pltpu.VMEM((SC_XB * hk,), F32),
                              pltpu.SemaphoreType.DMA((SC_USLOT,)),
                              pltpu.SemaphoreType.DMA((SC_VSLOT,)),
                              pltpu.SemaphoreType.DMA((SC_VSLOT,))])
    def peer(u_hbm, v_hbm, i_hbm, x_hbm, h_hbm, vg_hbm, idx_v, x_v, ubuf, vbuf, acc_s, h_v, usem, vsem, wsem):
        wid = lax.axis_index("c") * (SC_WORKERS // 2) + lax.axis_index("s")
        tok0 = wid * T
        row0 = tok0 * hk
        pltpu.sync_copy(i_hbm.at[pl.ds(row0, T * hk)], idx_v)

        def ufetch(t, j):
            slot = j % SC_USLOT
            return pltpu.make_async_copy(u_hbm.at[idx_v.at[pl.ds(t * hk + j * uwin, uwin)]], ubuf.at[slot], usem.at[slot])

        def vfetch(s, slot):
            return pltpu.make_async_copy(v_hbm.at[idx_v.at[pl.ds(s * SC_VWIN, SC_VWIN)]], vbuf.at[slot], vsem.at[slot])

        def vflush(s, slot):
            return pltpu.make_async_copy(vbuf.at[slot], vg_hbm.at[pl.ds(row0 + s * SC_VWIN, SC_VWIN)], wsem.at[slot])

        for j in range(SC_USLOT):
            ufetch(0, j).start()
        for j in range(vlead):
            vfetch(j, j).start()

        def vstep(s, slot):
            other = (slot + vlead) % SC_VSLOT
            vfetch(s, slot).wait()
            vflush(s, slot).start()

            @pl.when(s >= vlead)
            def _():
                vflush(s - vlead, other).wait()

            @pl.when(s + vlead < vsteps)
            def _():
                vfetch(s + vlead, other).start()

        lane = lax.iota(jnp.int32, L)

        @pl.loop(0, T // SC_XB)
        def _(tb):
            pltpu.sync_copy(x_hbm.at[pl.ds(pl.multiple_of(tok0 + tb * SC_XB, SC_XB), SC_XB)], x_v)

            @pl.loop(0, SC_XB)
            def _(tt):
                t = tb * SC_XB + tt
                for j in range(upt):
                    for i in range(per_u):
                        vstep(t * vpt + j * per_u + i, (j * per_u + i) % SC_VSLOT)
                    ufetch(t, j).wait()

                    def dot_body(jj, accs):
                        off = pl.multiple_of(jj * L, L)
                        xlo = x_v[tt, pl.ds(off, L)]
                        xhi = x_v[tt, pl.ds(w + off, L)]
                        out = []
                        for r in range(uwin):
                            wv = ubuf[j % SC_USLOT, r, pl.ds(off, L)]
                            lo = plsc.bitcast(wv << 16, F32)
                            hi = plsc.bitcast(wv & jnp.uint32(0xFFFF0000), F32)
                            out.append(accs[r] + lo * xlo + hi * xhi)
                        return tuple(out)

                    accs = lax.fori_loop(0, w // L, dot_body, tuple(jnp.zeros((L,), F32) for _ in range(uwin)))

                    if j + SC_USLOT < upt:
                        ufetch(t, j + SC_USLOT).start()
                    else:
                        @pl.when(t + 1 < T)
                        def _():
                            ufetch(t + 1, j + SC_USLOT - upt).start()

                    for g in range(uwin // L):
                        for r in range(L):
                            acc_s[r, :] = accs[g * L + r]
                        tot = jnp.zeros((L,), F32)
                        for c in range(L):
                            tot = tot + plsc.load_gather(acc_s, [lane, jnp.full((L,), c, jnp.int32)])
                        h_v[pl.ds(tt * hk + j * uwin + g * L, L)] = tot

            pltpu.sync_copy(h_v, h_hbm.at[pl.ds(pl.multiple_of(row0 + tb * SC_XB * hk, SC_XB * hk), SC_XB * hk)])

        for s in range(vsteps - vlead, vsteps):
            vflush(s, s % SC_VSLOT).wait()

    return peer(utab, vtab, idx, x)


def _unpack(w):
    lo = lax.bitcast_convert_type(w << 16, F32)
    hi = lax.bitcast_convert_type(w & jnp.uint32(0xFFFF0000), F32)
    return lo, hi


def _mix_kernel(alpha, h_ref, s_ref, gate_ref, vg_ref, g_ref, b_ref, o_ref):
    TT, HK = MIX_TT, PEER_HEADS * PEER_TOPK
    x = h_ref[...]
    wrow = gate_ref[...] * jax.nn.gelu(s_ref[...])
    eye = lax.broadcasted_iota(jnp.int32, (HK, HK), 0) == lax.broadcasted_iota(jnp.int32, (HK, HK), 1)
    rows = []
    for t in range(TT):
        wcol = jnp.sum(jnp.where(eye, jnp.broadcast_to(wrow[t:t + 1, :], (HK, HK)), 0.0),
                       axis=1, keepdims=True)
        vlo, vhi = _unpack(vg_ref[t * HK:(t + 1) * HK, :])
        rows.append(jnp.concatenate([jnp.sum(wcol * vlo, axis=0, keepdims=True),
                                     jnp.sum(wcol * vhi, axis=0, keepdims=True)], axis=1))
    ffn = jnp.concatenate(rows, axis=0)
    o_ref[...] = _layernorm(alpha * x + ffn, g_ref[...], b_ref[...])


def _mix(alpha, h1, scores, gates, vg, g, b):
    n, D = h1.shape
    tt = MIX_TT
    hk = PEER_HEADS * PEER_TOPK
    const = lambda shp: pl.BlockSpec(shp, lambda i: (0, 0))
    return pl.pallas_call(
        functools.partial(_mix_kernel, alpha),
        grid=(n // tt,),
        in_specs=[pl.BlockSpec((tt, D), lambda i: (i, 0)), pl.BlockSpec((tt, hk), lambda i: (i, 0)),
                  pl.BlockSpec((tt, hk), lambda i: (i, 0)), pl.BlockSpec((tt * hk, D // 2), lambda i: (i, 0)),
                  const((1, D)), const((1, D))],
        out_specs=pl.BlockSpec((tt, D), lambda i: (i, 0)),
        out_shape=jax.ShapeDtypeStruct((n, D), F32),
        compiler_params=_cparams(("parallel",)),
        name="peer_mix_ln",
    )(h1, scores, gates, vg, g.reshape(1, D).astype(F32), b.reshape(1, D).astype(F32))


PEER_CHUNK = 4096
SEQ_PIECE = 4096


def _peer_ffn_ln(alpha, h1, wq, k1, k2, up, vp, g, b, routed=None):
    n, D = h1.shape
    hk = PEER_HEADS * PEER_TOPK
    eT, gT = routed if routed is not None else _route(h1, wq, k1, k2)
    experts = eT.T
    gates = gT.T
    ch = min(PEER_CHUNK, n)
    outs = []
    for c in range(n // ch):
        hc = h1[c * ch:(c + 1) * ch]
        scores, vg = _sc_peer(up, vp, experts[c * ch:(c + 1) * ch].reshape(ch * hk), hc)
        outs.append(_mix(alpha, hc, scores.reshape(ch, hk), gates[c * ch:(c + 1) * ch], vg, g, b))
    return jnp.concatenate(outs, axis=0)


def kernel(x, w_in, w_out, hg_lb, hg_norm_g, cmpk_pe, cmpk_w1, cmpk_b1, cmpk_w2, cmpk_b2,
           cmpv_pe, cmpv_w1, cmpv_b1, cmpv_w2, cmpv_b2, ln1_g, ln1_b,
           peer_wq, peer_k1, peer_k2, peer_u, peer_v, ln2_g, ln2_b):
    B, S, D = x.shape
    depth = w_in.shape[0]
    alpha = (2.0 * depth) ** 0.25
    h = x
    for l in range(depth):
        up, vp = _pack_rows(peer_u[l]), _pack_rows(peer_v[l])
        cmpk = (cmpk_pe[l], cmpk_w1[l], cmpk_b1[l], cmpk_w2[l], cmpk_b2[l])
        cmpv = (cmpv_pe[l], cmpv_w1[l], cmpv_b1[l], cmpv_w2[l], cmpv_b2[l])
        outs = []
        for bi in range(B):
            hb = h[bi:bi + 1]
            hg, qT, kn, ksE, vsT, vwT, gT = _project(hb, w_in[l])
            o_hg = _hgrn2(hg, hg_lb, hg_norm_g[l], l)
            cmp_n, cmp_T = _compress(kn, cmpk, cmpv)
            piece = min(SEQ_PIECE, S)
            for lo in range(0, S, piece):
                o_nsa = _nsa(qT, cmp_n, cmp_T, kn, ksE, vsT, vwT, gT, lo, piece)
                h1, eT, egT = _outproj_route(alpha, o_hg[0, lo:lo + piece], o_nsa.reshape(piece, NSA_WIDTH),
                                             hb[0, lo:lo + piece], w_out[l], ln1_g[l], ln1_b[l],
                                             peer_wq[l], peer_k1[l], peer_k2[l])
                outs.append(_peer_ffn_ln(alpha, h1, peer_wq[l], peer_k1[l], peer_k2[l], up, vp, ln2_g[l], ln2_b[l],
                                         routed=(eT, egT)))
        h = jnp.concatenate(outs, axis=0).reshape(B, S, D)
    return h
```
